```python
import jax, jax.numpy as jnp
from jax import lax
import numpy as np

D_MODEL = 1024
BATCH = 4
SEQ = 8192
DEPTH = 1

HEAD_DIM = 64
SWA_HEADS = 8
SWA_KV_HEADS = 2
SB_HEADS = 8
SWA_WIDTH = SWA_HEADS * HEAD_DIM
SWA_KV_WIDTH = SWA_KV_HEADS * HEAD_DIM
SB_WIDTH = SB_HEADS * HEAD_DIM
MIX_WIDTH = SWA_WIDTH + SB_WIDTH
IN_WIDTH = SWA_WIDTH + 2 * SWA_KV_WIDTH + 3 * SB_WIDTH
WINDOW = 128
SB_BLOCK = 128
ROPE_THETA = 10000.0
N_EXPERTS = 256
TOP_K = 8
N_GROUPS = 8
TOPK_GROUPS = 4
EXPERT_FF = 256
ROUTED_SCALE = 2.5
EXPERT_BLOCK = 128
LN_EPS = 1e-5
DEEPNORM_ALPHA = (2 * DEPTH) ** 0.25
DEEPNORM_BETA = (8 * DEPTH) ** -0.25

kernel_name = "hymba_swa_stickbreak_moe_deepnorm_adaln"


def _layer_norm(x, gain=None, bias=None):
    xf = x.astype(jnp.float32)
    mu = jnp.mean(xf, axis=-1, keepdims=True)
    var = jnp.mean(jnp.square(xf - mu), axis=-1, keepdims=True)
    y = (xf - mu) * lax.rsqrt(var + LN_EPS)
    if gain is not None:
        y = y * gain.astype(jnp.float32) + bias.astype(jnp.float32)
    return y.astype(x.dtype)


def _rms_norm(x, gain):
    xf = x.astype(jnp.float32)
    y = xf * lax.rsqrt(jnp.mean(jnp.square(xf), axis=-1, keepdims=True) + LN_EPS)
    return (y * gain.astype(jnp.float32)).astype(x.dtype)


def _rope(x, positions):
    half = HEAD_DIM // 2
    inv_freq = ROPE_THETA ** (-jnp.arange(half, dtype=jnp.float32) * 2.0 / HEAD_DIM)
    ang = positions.astype(jnp.float32)[..., None] * inv_freq
    cos = jnp.cos(ang)[:, :, None, :]
    sin = jnp.sin(ang)[:, :, None, :]
    xf = x.astype(jnp.float32)
    x1, x2 = xf[..., :half], xf[..., half:]
    return jnp.concatenate([x1 * cos - x2 * sin, x2 * cos + x1 * sin], axis=-1).astype(x.dtype)


def _sliding_window_attention(q, k, v, sinks):
    b, s = q.shape[0], q.shape[1]
    nb = s // WINDOW
    g = SWA_HEADS // SWA_KV_HEADS
    qb = q.reshape(b, nb, WINDOW, SWA_KV_HEADS, g, HEAD_DIM)

    def band(t):
        tp = jnp.pad(t, ((0, 0), (WINDOW, 0), (0, 0), (0, 0)))
        tp = tp.reshape(b, nb + 1, WINDOW, SWA_KV_HEADS, HEAD_DIM)
        return jnp.concatenate([tp[:, :-1], tp[:, 1:]], axis=2)

    kb, vb = band(k), band(v)
    scores = jnp.einsum('bnikgd,bnjkd->bnkgij', qb, kb,
                        preferred_element_type=jnp.float32) * (HEAD_DIM ** -0.5)
    qi = jnp.arange(WINDOW)[:, None]
    kj = jnp.arange(2 * WINDOW)[None, :]
    dist = qi + WINDOW - kj
    in_band = (dist >= 0) & (dist < WINDOW)
    key_pos = jnp.arange(nb)[:, None] * WINDOW - WINDOW + jnp.arange(2 * WINDOW)[None, :]
    mask = in_band[None, :, :] & (key_pos >= 0)[:, None, :]
    mask = mask[None, :, None, None]
    sink = sinks.astype(jnp.float32).reshape(1, 1, SWA_KV_HEADS, g, 1, 1)
    scores = jnp.where(mask, scores, -jnp.inf)
    m = jnp.maximum(jnp.max(scores, axis=-1, keepdims=True), sink)
    p = jnp.exp(scores - m)
    p = p / (jnp.sum(p, axis=-1, keepdims=True) + jnp.exp(sink - m))
    o = jnp.einsum('bnkgij,bnjkd->bnikgd', p.astype(v.dtype), vb)
    return o.reshape(b, s, SWA_WIDTH)


def _stick_breaking_attention(q, k, v):
    b, s = q.shape[0], q.shape[1]
    nb = s // SB_BLOCK
    qb = q.reshape(b, nb, SB_BLOCK, SB_HEADS, HEAD_DIM).transpose(1, 0, 3, 2, 4)
    kt = k.transpose(0, 2, 1, 3)
    vt = v.transpose(0, 2, 1, 3)
    key_pos = jnp.arange(s)

    def block(args):
        q_blk, n = args
        z = jnp.einsum('bhid,bhsd->bhis', q_blk, kt,
                       preferred_element_type=jnp.float32) * (HEAD_DIM ** -0.5)
        q_pos = n * SB_BLOCK + jnp.arange(SB_BLOCK)
        before = key_pos[None, :] < q_pos[:, None]
        log_keep = jnp.where(before, jax.nn.log_sigmoid(-z), 0.0)
        later = lax.cumsum(log_keep, axis=3, reverse=True) - log_keep
        w = jnp.where(before, jnp.exp(jax.nn.log_sigmoid(z) + later), 0.0)
        return jnp.einsum('bhis,bhsd->bhid', w.astype(v.dtype), vt)

    o = lax.map(block, (qb, jnp.arange(nb)))
    return o.transpose(1, 0, 3, 2, 4).reshape(b, s, SB_WIDTH)


def _mixer(u, positions, w_in, sinks, g_swa, g_sb, w_out):
    b, s, _ = u.shape
    h = jnp.matmul(u, w_in)
    o1 = SWA_WIDTH
    o2 = o1 + SWA_KV_WIDTH
    o3 = o2 + SWA_KV_WIDTH
    o4 = o3 + SB_WIDTH
    o5 = o4 + SB_WIDTH
    qa = _rope(h[..., :o1].reshape(b, s, SWA_HEADS, HEAD_DIM), positions)
    ka = _rope(h[..., o1:o2].reshape(b, s, SWA_KV_HEADS, HEAD_DIM), positions)
    va = h[..., o2:o3].reshape(b, s, SWA_KV_HEADS, HEAD_DIM)
    qs = h[..., o3:o4].reshape(b, s, SB_HEADS, HEAD_DIM)
    ks = h[..., o4:o5].reshape(b, s, SB_HEADS, HEAD_DIM)
    vs = h[..., o5:].reshape(b, s, SB_HEADS, HEAD_DIM)
    oa = _rms_norm(_sliding_window_attention(qa, ka, va, sinks), g_swa)
    ob = _rms_norm(_stick_breaking_attention(qs, ks, vs), g_sb)
    return jnp.matmul(jnp.concatenate([oa, ob], axis=-1), w_out)


def _swiglu(x, w_gate, w_up, w_down):
    return jnp.matmul(jax.nn.silu(jnp.matmul(x, w_gate)) * jnp.matmul(x, w_up), w_down)


def _moe(u, w_router, e_bias, w1, w3, w2, ws1, ws3, ws2):
    b, s, d = u.shape
    n = b * s
    uf = u.reshape(n, d)
    scores = jax.nn.sigmoid(jnp.matmul(uf, w_router, preferred_element_type=jnp.float32))
    biased = scores + e_bias.astype(jnp.float32)
    grouped = biased.reshape(n, N_GROUPS, N_EXPERTS // N_GROUPS)
    group_score = jnp.sum(lax.top_k(grouped, 2)[0], axis=-1)
    _, g_idx = lax.top_k(group_score, TOPK_GROUPS)
    g_mask = jnp.sum(jax.nn.one_hot(g_idx, N_GROUPS, dtype=jnp.float32), axis=1) > 0
    e_mask = jnp.repeat(g_mask, N_EXPERTS // N_GROUPS, axis=1)
    _, e_idx = lax.top_k(jnp.where(e_mask, biased, -jnp.inf), TOP_K)
    gate = jnp.take_along_axis(scores, e_idx, axis=1)
    gate = gate / jnp.sum(gate, axis=-1, keepdims=True) * ROUTED_SCALE

    nk = n * TOP_K
    e_flat = e_idx.reshape(nk)
    tok_flat = jnp.repeat(jnp.arange(n, dtype=jnp.int32), TOP_K)
    w_flat = gate.reshape(nk)
    order = jnp.argsort(e_flat, stable=True)
    e_sorted = e_flat[order]
    counts = jnp.bincount(e_flat, length=N_EXPERTS)
    padded = (counts + EXPERT_BLOCK - 1) // EXPERT_BLOCK * EXPERT_BLOCK
    start = jnp.cumsum(counts) - counts
    pend = jnp.cumsum(padded)
    pstart = pend - padded
    dest = pstart[e_sorted] + (jnp.arange(nk) - start[e_sorted])
    n_rows = nk + N_EXPERTS * EXPERT_BLOCK
    n_blocks = n_rows // EXPERT_BLOCK
    row_tok = jnp.full((n_rows,), n, dtype=jnp.int32).at[dest].set(tok_flat[order])
    row_w = jnp.zeros((n_rows,), jnp.float32).at[dest].set(w_flat[order])
    blk_e = jnp.minimum(jnp.searchsorted(pend, jnp.arange(n_blocks) * EXPERT_BLOCK, side='right'),
                        N_EXPERTS - 1)
    u_pad = jnp.concatenate([uf, jnp.zeros((1, d), uf.dtype)], axis=0)

    def expert_block(args):
        rows, rw, e = args
        xb = u_pad[rows]
        yb = _swiglu(xb, w1[e], w3[e], w2[e])
        return yb * rw[:, None].astype(yb.dtype)

    out = lax.map(expert_block, (row_tok.reshape(n_blocks, EXPERT_BLOCK),
                                 row_w.reshape(n_blocks, EXPERT_BLOCK), blk_e))
    routed = jnp.zeros((n + 1, d), out.dtype).at[row_tok].add(out.reshape(n_rows, d))[:n]
    shared = _swiglu(uf, ws1, ws3, ws2)
    return (shared + routed.astype(shared.dtype)).reshape(b, s, d)


def setup_inputs(seed: int = 0) -> dict:
    key = jax.random.key(seed)
    ks = jax.random.split(key, 24)
    L, D, E, F = DEPTH, D_MODEL, N_EXPERTS, EXPERT_FF
    beta = DEEPNORM_BETA
    nrm = jax.random.normal
    col_scale = jnp.concatenate([
        jnp.ones((SWA_WIDTH + SWA_KV_WIDTH,), jnp.float32),
        jnp.full((SWA_KV_WIDTH,), beta, jnp.float32),
        jnp.ones((2 * SB_WIDTH,), jnp.float32),
        jnp.full((SB_WIDTH,), beta, jnp.float32),
    ])
    offsets = jax.random.randint(ks[2], (BATCH, 1), 0, 1024, dtype=jnp.int32)
    return {
        "x": nrm(ks[0], (BATCH, SEQ, D), jnp.float32),
        "c": nrm(ks[1], (BATCH, D), jnp.float32),
        "positions": offsets + jnp.arange(SEQ, dtype=jnp.int32)[None, :],
        "w_ada": nrm(ks[3], (L, D, 6 * D), jnp.float32) * (0.1 * D ** -0.5),
        "b_ada": nrm(ks[4], (L, 6 * D), jnp.float32) * 0.01,
        "w_in": nrm(ks[5], (L, D, IN_WIDTH), jnp.float32) * (D ** -0.5) * col_scale,
        "attn_sinks": nrm(ks[6], (L, SWA_HEADS), jnp.float32) * 0.5,
        "g_swa": 1.0 + 0.02 * nrm(ks[7], (L, SWA_WIDTH), jnp.float32),
        "g_sb": 1.0 + 0.02 * nrm(ks[8], (L, SB_WIDTH), jnp.float32),
        "w_out": nrm(ks[9], (L, MIX_WIDTH, D), jnp.float32) * (MIX_WIDTH ** -0.5) * beta,
        "ln1_g": 1.0 + 0.02 * nrm(ks[10], (L, D), jnp.float32),
        "ln1_b": 0.02 * nrm(ks[11], (L, D), jnp.float32),
        "w_router": nrm(ks[12], (L, D, E), jnp.float32) * (D ** -0.5),
        "e_bias": 0.01 * nrm(ks[13], (L, E), jnp.float32),
        "w1": nrm(ks[14], (L, E, D, F), jnp.float32) * (D ** -0.5) * beta,
        "w3": nrm(ks[15], (L, E, D, F), jnp.float32) * (D ** -0.5) * beta,
        "w2": nrm(ks[16], (L, E, F, D), jnp.float32) * (F ** -0.5) * beta,
        "ws1": nrm(ks[17], (L, D, F), jnp.float32) * (D ** -0.5) * beta,
        "ws3": nrm(ks[18], (L, D, F), jnp.float32) * (D ** -0.5) * beta,
        "ws2": nrm(ks[19], (L, F, D), jnp.float32) * (F ** -0.5) * beta,
        "ln2_g": 1.0 + 0.02 * nrm(ks[20], (L, D), jnp.float32),
        "ln2_b": 0.02 * nrm(ks[21], (L, D), jnp.float32),
    }


def reference(x, c, positions, w_ada, b_ada, w_in, attn_sinks, g_swa, g_sb, w_out,
              ln1_g, ln1_b, w_router, e_bias, w1, w3, w2, ws1, ws3, ws2, ln2_g, ln2_b):
    for l in range(DEPTH):
        ada = jnp.matmul(jax.nn.silu(c), w_ada[l]) + b_ada[l]
        shift1, scale1, gate1, shift2, scale2, gate2 = [t[:, None, :] for t in jnp.split(ada, 6, axis=-1)]
        u = _layer_norm(x) * (1.0 + scale1) + shift1
        y = _mixer(u, positions, w_in[l], attn_sinks[l], g_swa[l], g_sb[l], w_out[l])
        x = _layer_norm(DEEPNORM_ALPHA * x + (1.0 + gate1) * y, ln1_g[l], ln1_b[l])
        u = _layer_norm(x) * (1.0 + scale2) + shift2
        y = _moe(u, w_router[l], e_bias[l], w1[l], w3[l], w2[l], ws1[l], ws3[l], ws2[l])
        x = _layer_norm(DEEPNORM_ALPHA * x + (1.0 + gate2) * y, ln2_g[l], ln2_b[l])
    return x
```

```python
import functools

import numpy as np
import jax
import jax.numpy as jnp
from jax import lax
from jax.experimental import pallas as pl
from jax.experimental.pallas import tpu as pltpu

F32 = jnp.float32
BF16 = jnp.bfloat16
U32 = jnp.uint32
I32 = jnp.int32

HEAD_DIM = 64
SWA_HEADS = 8
SWA_KV_HEADS = 2
SB_HEADS = 8
SWA_WIDTH = SWA_HEADS * HEAD_DIM
SWA_KV_WIDTH = SWA_KV_HEADS * HEAD_DIM
SB_WIDTH = SB_HEADS * HEAD_DIM
WINDOW = 128
ROPE_THETA = 10000.0
N_EXPERTS = 256
TOP_K = 8
N_GROUPS = 8
TOPK_GROUPS = 4
GROUP_SIZE = N_EXPERTS // N_GROUPS
ROUTED_SCALE = 2.5
EXPERT_BLOCK = 128
LN_EPS = 1e-5
DEPTH = 1
DEEPNORM_ALPHA = (2 * DEPTH) ** 0.25

LANES = 128
SB_BLOCK = 128
SB_UNDERFLOW = 110.0
NEG_BIG = -1e30
VMEM_LIMIT = 56 * 1024 * 1024

TM_PROJ = 512
TQ_SWA = 512
TT_ROUTE = 512
TD_DISPATCH = 256
TF_FINAL = 256


def _cparams(sem, vmem=VMEM_LIMIT):
    return pltpu.CompilerParams(dimension_semantics=sem, vmem_limit_bytes=vmem)


def _layer_norm(x):
    mu = jnp.mean(x, axis=-1, keepdims=True)
    xc = x - mu
    var = jnp.mean(xc * xc, axis=-1, keepdims=True)
    return xc * lax.rsqrt(var + LN_EPS)


def _silu(x):
    return x * jax.nn.sigmoid(x)


def _dot(a, b):
    return jnp.dot(a, b, preferred_element_type=F32)


def _dot_nt(a, b):
    return lax.dot_general(a, b, (((1,), (1,)), ((), ())), preferred_element_type=F32)


def _pack_bf16_pair(lo, hi):
    lo_b = pltpu.bitcast(lo.astype(BF16).astype(F32), U32)
    hi_b = pltpu.bitcast(hi.astype(BF16).astype(F32), U32)
    return (hi_b & jnp.uint32(0xFFFF0000)) | (lo_b >> jnp.uint32(16))


def _unpack_bf16_pair(p):
    lo = pltpu.bitcast(p << jnp.uint32(16), F32)
    hi = pltpu.bitcast(p & jnp.uint32(0xFFFF0000), F32)
    return lo, hi


def _ada_kernel(c_ref, w_ref, b_ref, o_ref):
    sc = _silu(c_ref[...])
    o_ref[...] = _dot(sc.astype(BF16), w_ref[...].astype(BF16)) + b_ref[...]


def _ada(c, w_ada, b_ada):
    b, d = c.shape
    n_out = w_ada.shape[1]
    rows = 8
    c_pad = jnp.zeros((rows, d), F32).at[:b].set(c)
    out = pl.pallas_call(
        _ada_kernel,
        out_shape=jax.ShapeDtypeStruct((rows, n_out), F32),
        grid=(n_out // d,),
        in_specs=[
            pl.BlockSpec((rows, d), lambda j: (0, 0)),
            pl.BlockSpec((d, d), lambda j: (0, j)),
            pl.BlockSpec((1, d), lambda j: (0, j)),
        ],
        out_specs=pl.BlockSpec((rows, d), lambda j: (0, j)),
        compiler_params=_cparams(("arbitrary",)),
        name="ada",
    )(c_pad, w_ada, b_ada.reshape(1, n_out))
    return out[:b].reshape(b, n_out // d, d)


def _inproj_kernel(x_ref, ada_ref, pos_ref, invf_ref, w_ref,
                   qa_ref, ka_ref, va_ref, qs_ref, ks_ref, vs_ref):
    x = x_ref[...]
    shift = ada_ref[0, 0:1, :]
    scale = ada_ref[0, 1:2, :]
    u = _layer_norm(x) * (1.0 + scale) + shift
    h = _dot(u.astype(BF16), w_ref[...])

    ang = pos_ref[...] * invf_ref[...]
    cs = jnp.cos(ang)
    sn = jnp.sin(ang)
    lane = lax.broadcasted_iota(I32, (1, LANES), 1)
    first = (lane & (HEAD_DIM // 2)) == 0
    sn_signed = jnp.where(first, -sn, sn)

    def rope(hc):
        partner = jnp.where(first, pltpu.roll(hc, LANES - HEAD_DIM // 2, 1),
                            pltpu.roll(hc, HEAD_DIM // 2, 1))
        return hc * cs + partner * sn_signed

    q_scale = HEAD_DIM ** -0.5
    o = 0
    for c in range(SWA_WIDTH // LANES):
        qa_ref[:, c * LANES:(c + 1) * LANES] = (rope(h[:, o:o + LANES]) * q_scale).astype(BF16)
        o += LANES
    ka_ref[...] = rope(h[:, o:o + SWA_KV_WIDTH]).astype(BF16)
    o += SWA_KV_WIDTH
    va_ref[...] = h[:, o:o + SWA_KV_WIDTH].astype(BF16)
    o += SWA_KV_WIDTH
    qs_ref[...] = h[:, o:o + SB_WIDTH].astype(BF16)
    o += SB_WIDTH
    ks_ref[...] = h[:, o:o + SB_WIDTH].astype(BF16)
    o += SB_WIDTH
    vs_ref[...] = h[:, o:o + SB_WIDTH].astype(BF16)


def _inproj(x2, ada, pos_col, invf, w_in_b, seq):
    n, d = x2.shape
    tm = TM_PROJ
    steps_per_seq = seq // tm
    widths = (SWA_WIDTH, SWA_KV_WIDTH, SWA_KV_WIDTH, SB_WIDTH, SB_WIDTH, SB_WIDTH)
    return pl.pallas_call(
        _inproj_kernel,
        out_shape=[jax.ShapeDtypeStruct((n, w), BF16) for w in widths],
        grid=(n // tm,),
        in_specs=[
            pl.BlockSpec((tm, d), lambda i: (i, 0)),
            pl.BlockSpec((1,) + ada.shape[1:], lambda i: (i // steps_per_seq, 0, 0)),
            pl.BlockSpec((tm, 1), lambda i: (i, 0)),
            pl.BlockSpec((1, LANES), lambda i: (0, 0)),
            pl.BlockSpec(w_in_b.shape, lambda i: (0, 0)),
        ],
        out_specs=[pl.BlockSpec((tm, w), lambda i: (i, 0)) for w in widths],
        compiler_params=_cparams(("arbitrary",)),
        name="inproj",
    )(x2, ada, pos_col, invf, w_in_b)


def _swa_kernel(q_ref, kc_ref, kp_ref, vc_ref, vp_ref, bias0_ref, bias_ref, sink_ref,
                o_ref, kall, vall):
    tq = q_ref.shape[0]
    kall[0:WINDOW, :] = kp_ref[...]
    kall[WINDOW:, :] = kc_ref[...]
    vall[0:WINDOW, :] = vp_ref[...]
    vall[WINDOW:, :] = vc_ref[...]
    lane = lax.broadcasted_iota(I32, (1, LANES), 1)
    half0 = lane < HEAD_DIM
    half1 = lane >= HEAD_DIM
    sink = sink_ref[...]
    n_col = SWA_WIDTH // LANES
    for j in range(tq // WINDOW):
        r0 = j * WINDOW
        q = q_ref[r0:r0 + WINDOW, :]
        parts = []
        for half in (half0, half1):
            for c in range(n_col):
                parts.append(jnp.where(half, q[:, c * LANES:(c + 1) * LANES], jnp.zeros((), BF16)))
        qm = jnp.concatenate(parts, axis=0)
        kb = kall[r0:r0 + 2 * WINDOW, :]
        vb = vall[r0:r0 + 2 * WINDOW, :]
        s = _dot_nt(qm, kb)
        s = s + (bias0_ref[0] if j == 0 else bias_ref[...])
        m = jnp.maximum(jnp.max(s, axis=1, keepdims=True), sink)
        p = jnp.exp(s - m)
        den = jnp.sum(p, axis=1, keepdims=True) + jnp.exp(sink - m)
        o = _dot(p.astype(BF16), vb) / den
        for c in range(n_col):
            lo = o[c * WINDOW:(c + 1) * WINDOW]
            hi = o[(n_col + c) * WINDOW:(n_col + c + 1) * WINDOW]
            o_ref[r0:r0 + WINDOW, c * LANES:(c + 1) * LANES] = jnp.where(half0, lo, hi)


def _swa_bias():
    qi = np.arange(SWA_HEADS * WINDOW)[:, None] % WINDOW
    kj = np.arange(2 * WINDOW)[None, :]
    dist = qi + WINDOW - kj
    band = (dist >= 0) & (dist < WINDOW)
    bias = np.where(band, 0.0, NEG_BIG).astype(np.float32)
    first = np.where(band & (kj >= WINDOW), 0.0, NEG_BIG).astype(np.float32)
    return jnp.asarray(np.stack([bias, first])), jnp.asarray(bias)


def _swa(qa, ka, va, sink_col, batch, seq):
    n = qa.shape[0]
    tq = TQ_SWA
    nt = seq // tq
    blocks_per_tile = tq // WINDOW
    bias_pair, bias = _swa_bias()

    def cur(b, i):
        return (b * nt + i, 0)

    def prev(b, i):
        return (jnp.maximum((b * nt + i) * blocks_per_tile - 1, 0), 0)

    return pl.pallas_call(
        _swa_kernel,
        out_shape=jax.ShapeDtypeStruct((n, SWA_WIDTH), F32),
        grid=(batch, nt),
        in_specs=[
            pl.BlockSpec((tq, SWA_WIDTH), cur),
            pl.BlockSpec((tq, SWA_KV_WIDTH), cur),
            pl.BlockSpec((WINDOW, SWA_KV_WIDTH), prev),
            pl.BlockSpec((tq, SWA_KV_WIDTH), cur),
            pl.BlockSpec((WINDOW, SWA_KV_WIDTH), prev),
            pl.BlockSpec((1,) + bias.shape, lambda b, i: (jnp.where(i == 0, 1, 0), 0, 0)),
            pl.BlockSpec(bias.shape, lambda b, i: (0, 0)),
            pl.BlockSpec(sink_col.shape, lambda b, i: (0, 0)),
        ],
        out_specs=pl.BlockSpec((tq, SWA_WIDTH), cur),
        scratch_shapes=[pltpu.VMEM((tq + WINDOW, SWA_KV_WIDTH), BF16),
                        pltpu.VMEM((tq + WINDOW, SWA_KV_WIDTH), BF16)],
        compiler_params=_cparams(("arbitrary", "arbitrary")),
        name="swa",
    )(qa, ka, ka, va, va, bias_pair, bias, sink_col)


def _sb_kernel(q_ref, k_ref, v_ref, uu_ref, o_ref, acc, run):
    i = pl.program_id(2)
    blk = SB_BLOCK
    lane = lax.broadcasted_iota(I32, (1, LANES), 1)
    half0 = lane < HEAD_DIM
    half1 = lane >= HEAD_DIM
    q = q_ref[...]
    zero = jnp.zeros((), BF16)
    qm = jnp.concatenate([jnp.where(half0, q, zero), jnp.where(half1, q, zero)], axis=0)
    acc[...] = jnp.zeros_like(acc)
    run[...] = jnp.zeros_like(run)
    row = lax.broadcasted_iota(I32, (2 * blk, blk), 0) & (blk - 1)
    col = lax.broadcasted_iota(I32, (2 * blk, blk), 1)
    strictly_causal = col < row

    def body(carry):
        j, _ = carry
        off = pl.multiple_of(j * blk, blk)
        kb = k_ref[pl.ds(off, blk), :]
        vb = v_ref[pl.ds(off, blk), :]
        z = _dot_nt(qm, kb)
        valid = jnp.logical_or(strictly_causal, j < i)
        sp = jnp.maximum(z, 0.0) + jnp.log1p(jnp.exp(-jnp.abs(z)))
        mass = jnp.where(valid, sp, 0.0)
        mass_hi = mass.astype(BF16)
        mass_lo = (mass - mass_hi.astype(F32)).astype(BF16)
        cc = _dot(jnp.concatenate([mass_hi, mass_lo], axis=1), uu_ref[...])
        before = run[...]
        c = cc[:, :blk] + before
        w = jnp.where(valid, jnp.exp(z - c), 0.0)
        acc[...] += _dot(w.astype(BF16), vb)
        after = before + cc[:, blk:]
        run[...] = after
        return j - 1, jnp.min(after) > SB_UNDERFLOW

    lax.while_loop(lambda c: jnp.logical_and(c[0] >= 0, jnp.logical_not(c[1])),
                   body, (i, jnp.bool_(False)))
    a = acc[...]
    o_ref[...] = jnp.where(half0, a[:blk], a[blk:])


def _sb_suffix_matrix():
    j = np.arange(2 * SB_BLOCK)[:, None] % SB_BLOCK
    s = np.arange(2 * SB_BLOCK)[None, :]
    m = np.where(s < SB_BLOCK, (j >= s), True)
    return jnp.asarray(m.astype(np.float32)).astype(BF16)


def _sb(qs, ks, vs, batch, seq):
    n = qs.shape[0]
    blk = SB_BLOCK
    nq = seq // blk
    pairs = SB_WIDTH // LANES
    uu = _sb_suffix_matrix()
    return pl.pallas_call(
        _sb_kernel,
        out_shape=jax.ShapeDtypeStruct((n, SB_WIDTH), F32),
        grid=(batch, pairs, nq),
        in_specs=[
            pl.BlockSpec((blk, LANES), lambda b, p, i: (b * nq + i, p)),
            pl.BlockSpec((seq, LANES), lambda b, p, i: (b, p)),
            pl.BlockSpec((seq, LANES), lambda b, p, i: (b, p)),
            pl.BlockSpec(uu.shape, lambda b, p, i: (0, 0)),
        ],
        out_specs=pl.BlockSpec((blk, LANES), lambda b, p, i: (b * nq + i, p)),
        scratch_shapes=[pltpu.VMEM((2 * blk, LANES), F32), pltpu.VMEM((2 * blk, LANES), F32)],
        compiler_params=_cparams(("arbitrary", "arbitrary", "arbitrary")),
        name="sb",
    )(qs, ks, vs, uu)


def _post_kernel(oa_ref, ob_ref, x_ref, ada_ref, ga_ref, gb_ref, wo_ref, g1_ref, b1_ref,
                 wrh_ref, wrl_ref, x1_ref, u2p_ref, lg_ref):
    def rms(o, g):
        return o * lax.rsqrt(jnp.mean(o * o, axis=-1, keepdims=True) + LN_EPS) * g

    na = rms(oa_ref[...], ga_ref[...]).astype(BF16)
    nb = rms(ob_ref[...], gb_ref[...]).astype(BF16)
    y = _dot(na, wo_ref[0:SWA_WIDTH, :]) + _dot(nb, wo_ref[SWA_WIDTH:, :])
    gate1 = ada_ref[0, 2:3, :]
    shift2 = ada_ref[0, 3:4, :]
    scale2 = ada_ref[0, 4:5, :]
    x1 = _layer_norm(DEEPNORM_ALPHA * x_ref[...] + (1.0 + gate1) * y) * g1_ref[...] + b1_ref[...]
    x1_ref[...] = x1
    u2 = _layer_norm(x1) * (1.0 + scale2) + shift2
    half = u2.shape[1] // 2
    for s in range(half // LANES):
        lo = u2[:, s * LANES:(s + 1) * LANES]
        hi = u2[:, half + s * LANES:half + (s + 1) * LANES]
        u2p_ref[:, s, :] = _pack_bf16_pair(lo, hi)
    u_hi = u2.astype(BF16)
    u_lo = (u2 - u_hi.astype(F32)).astype(BF16)
    wrh = wrh_ref[...]
    lg_ref[...] = _dot_nt(wrh, u_hi) + (_dot_nt(wrh, u_lo) + _dot_nt(wrl_ref[...], u_hi))


def _post(oa, ob, x2, ada, g_a, g_b, w_out_b, g1, b1, wr_hi, wr_lo, seq):
    n, d = x2.shape
    tm = TM_PROJ
    steps_per_seq = seq // tm
    e = wr_hi.shape[0]
    slabs = d // 2 // LANES
    const2 = lambda i: (0, 0)
    return pl.pallas_call(
        _post_kernel,
        out_shape=[jax.ShapeDtypeStruct((n, d), F32),
                   jax.ShapeDtypeStruct((n, slabs, LANES), U32),
                   jax.ShapeDtypeStruct((e, n), F32)],
        grid=(n // tm,),
        in_specs=[
            pl.BlockSpec((tm, SWA_WIDTH), lambda i: (i, 0)),
            pl.BlockSpec((tm, SB_WIDTH), lambda i: (i, 0)),
            pl.BlockSpec((tm, d), lambda i: (i, 0)),
            pl.BlockSpec((1,) + ada.shape[1:], lambda i: (i // steps_per_seq, 0, 0)),
            pl.BlockSpec(g_a.shape, const2),
            pl.BlockSpec(g_b.shape, const2),
            pl.BlockSpec(w_out_b.shape, const2),
            pl.BlockSpec(g1.shape, const2),
            pl.BlockSpec(b1.shape, const2),
            pl.BlockSpec(wr_hi.shape, const2),
            pl.BlockSpec(wr_lo.shape, const2),
        ],
        out_specs=[pl.BlockSpec((tm, d), lambda i: (i, 0)),
                   pl.BlockSpec((tm, slabs, LANES), lambda i: (i, 0, 0)),
                   pl.BlockSpec((e, tm), lambda i: (0, i))],
        compiler_params=_cparams(("arbitrary",)),
        name="post",
    )(oa, ob, x2, ada, g_a, g_b, w_out_b, g1, b1, wr_hi, wr_lo)


def _route_kernel(lg_ref, bias_ref, su_ref, eidx_ref, gate_ref, rank_ref, cnt_ref, seen):
    i = pl.program_id(0)

    @pl.when(i == 0)
    def _():
        seen[...] = jnp.zeros_like(seen)

    tt = lg_ref.shape[1]
    ninf = -jnp.inf
    scores = jax.nn.sigmoid(lg_ref[...])
    biased = scores + bias_ref[...]

    iog32 = lax.broadcasted_iota(I32, (GROUP_SIZE, tt), 0)
    groups = [biased[g * GROUP_SIZE:(g + 1) * GROUP_SIZE, :] for g in range(N_GROUPS)]
    gs_rows = []
    for blk in groups:
        m1 = jnp.max(blk, axis=0, keepdims=True)
        i1 = jnp.min(jnp.where(blk == m1, iog32, GROUP_SIZE), axis=0, keepdims=True)
        m2 = jnp.max(jnp.where(iog32 == i1, ninf, blk), axis=0, keepdims=True)
        gs_rows.append(m1 + m2)
    gs = jnp.concatenate(gs_rows, axis=0)

    iog = lax.broadcasted_iota(I32, gs.shape, 0)
    gsel = jnp.zeros(gs.shape, F32)
    cur = gs
    for _ in range(TOPK_GROUPS):
        m = jnp.max(cur, axis=0, keepdims=True)
        idx = jnp.min(jnp.where(cur == m, iog, N_GROUPS), axis=0, keepdims=True)
        hit = iog == idx
        gsel = jnp.where(hit, 1.0, gsel)
        cur = jnp.where(hit, ninf, cur)

    cand = jnp.concatenate(
        [jnp.where(gsel[g:g + 1, :] > 0.5, groups[g], ninf) for g in range(N_GROUPS)], axis=0)
    ioe = lax.broadcasted_iota(I32, cand.shape, 0)
    chosen = jnp.zeros(cand.shape, F32)
    idxs, gates = [], []
    for _ in range(TOP_K):
        m = jnp.max(cand, axis=0, keepdims=True)
        idx = jnp.min(jnp.where(cand == m, ioe, N_EXPERTS), axis=0, keepdims=True)
        hit = ioe == idx
        gates.append(jnp.sum(jnp.where(hit, scores, 0.0), axis=0, keepdims=True))
        cand = jnp.where(hit, ninf, cand)
        chosen = jnp.where(hit, 1.0, chosen)
        idxs.append(idx)
    gsum = gates[0]
    for g in gates[1:]:
        gsum = gsum + g
    gates = [g / gsum * ROUTED_SCALE for g in gates]

    rk = _dot(chosen.astype(BF16), su_ref[...])
    before = seen[...]
    rank_mat = rk[:, :tt] + jnp.concatenate([before] * (tt // LANES), axis=1)
    after = before + rk[:, tt:]
    seen[...] = after
    cnt_ref[...] = after
    ranks = [jnp.sum(jnp.where(ioe == idx, rank_mat, 0.0), axis=0, keepdims=True) for idx in idxs]
    eidx_ref[...] = jnp.concatenate(idxs, axis=0)
    gate_ref[...] = jnp.concatenate(gates, axis=0)
    rank_ref[...] = jnp.concatenate(ranks, axis=0).astype(I32)


def _route_prefix_matrix(tt):
    a = np.arange(tt)[:, None] < np.arange(tt)[None, :]
    m = np.concatenate([a, np.ones((tt, LANES), bool)], axis=1)
    return jnp.asarray(m.astype(np.float32)).astype(BF16)


def _route(logits_t, e_bias):
    e, n = logits_t.shape
    tt = TT_ROUTE
    bias_b = jnp.broadcast_to(e_bias.astype(F32)[:, None], (e, tt))
    su = _route_prefix_matrix(tt)
    tok = lambda i: (0, i)
    return pl.pallas_call(
        _route_kernel,
        out_shape=[jax.ShapeDtypeStruct((TOP_K, n), I32),
                   jax.ShapeDtypeStruct((TOP_K, n), F32),
                   jax.ShapeDtypeStruct((TOP_K, n), I32),
                   jax.ShapeDtypeStruct((e, LANES), F32)],
        grid=(n // tt,),
        in_specs=[pl.BlockSpec((e, tt), tok),
                  pl.BlockSpec((e, tt), lambda i: (0, 0)),
                  pl.BlockSpec(su.shape, lambda i: (0, 0))],
        out_specs=[pl.BlockSpec((TOP_K, tt), tok), pl.BlockSpec((TOP_K, tt), tok),
                   pl.BlockSpec((TOP_K, tt), tok), pl.BlockSpec((e, LANES), lambda i: (0, 0))],
        scratch_shapes=[pltpu.VMEM((e, LANES), F32)],
        compiler_params=_cparams(("arbitrary",)),
        name="route",
    )(logits_t, bias_b, su)


def _row_copy_all(rows_ref, table_ref, sem):
    t = rows_ref.shape[0]
    return pltpu.make_async_copy(table_ref.at[pl.ds(0, t)], rows_ref, sem)


def _dispatch_kernel(pstart_ref, pend_ref, nused_ref, eidx_ref, rank_ref, u_ref, xs_ref, zbuf, sem):
    i = pl.program_id(0)
    td = u_ref.shape[0]
    n_blocks = xs_ref.shape[0] // EXPERT_BLOCK

    @pl.when(i == 0)
    def _():
        zbuf[...] = jnp.zeros_like(zbuf)

        def unused(b):
            return pltpu.make_async_copy(
                zbuf, xs_ref.at[pl.ds(pl.multiple_of(b * EXPERT_BLOCK, EXPERT_BLOCK), EXPERT_BLOCK)], sem)

        def start_unused(b, c):
            unused(b).start()
            return c

        def wait_unused(b, c):
            unused(b).wait()
            return c

        lax.fori_loop(nused_ref[0], n_blocks, start_unused, 0)
        lax.fori_loop(nused_ref[0], n_blocks, wait_unused, 0)

        def tail(e):
            return pltpu.make_async_copy(
                zbuf, xs_ref.at[pl.ds(pend_ref[e] - EXPERT_BLOCK, EXPERT_BLOCK)], sem)

        def start(e, c):
            @pl.when(pend_ref[e] > pstart_ref[e])
            def _():
                tail(e).start()
            return c

        def wait(e, c):
            @pl.when(pend_ref[e] > pstart_ref[e])
            def _():
                tail(e).wait()
            return c

        lax.fori_loop(0, N_EXPERTS, start, 0)
        lax.fori_loop(0, N_EXPERTS, wait, 0)

    def issue(t, c):
        for k in range(TOP_K):
            dest = pstart_ref[eidx_ref[k, t]] + rank_ref[k, t]
            pltpu.make_async_copy(u_ref.at[pl.ds(t, 1)], xs_ref.at[pl.ds(dest, 1)], sem).start()
        return c

    lax.fori_loop(0, td, issue, 0)
    for _ in range(TOP_K):
        pltpu.make_async_copy(u_ref, xs_ref.at[pl.ds(0, td)], sem).wait()


def _dispatch(pstart, pend, nused, eidx_t, rank_t, u2p, n_rows):
    n, slabs, _ = u2p.shape
    td = TD_DISPATCH
    smem_tok = pl.BlockSpec((TOP_K, td), lambda i, *_: (0, i), memory_space=pltpu.SMEM)
    return pl.pallas_call(
        _dispatch_kernel,
        out_shape=jax.ShapeDtypeStruct((n_rows, slabs, LANES), U32),
        grid_spec=pltpu.PrefetchScalarGridSpec(
            num_scalar_prefetch=3,
            grid=(n // td,),
            in_specs=[smem_tok, smem_tok,
                      pl.BlockSpec((td, slabs, LANES), lambda i, *_: (i, 0, 0))],
            out_specs=pl.BlockSpec(memory_space=pl.ANY),
            scratch_shapes=[pltpu.VMEM((EXPERT_BLOCK, slabs, LANES), U32),
                            pltpu.SemaphoreType.DMA],
        ),
        compiler_params=_cparams(("arbitrary",)),
        name="dispatch",
    )(pstart, pend, nused, eidx_t, rank_t, u2p)


def _expert_kernel(blk_e_ref, nused_ref, x_ref, w1_ref, w3_ref, w2_ref, y_ref, w1b, w3b, w2b):
    i = pl.program_id(0)

    @pl.when(i < nused_ref[0])
    def _():
        e = blk_e_ref[i]
        prev = blk_e_ref[jnp.maximum(i - 1, 0)]

        @pl.when(jnp.logical_or(i == 0, e != prev))
        def _():
            w1b[...] = w1_ref[0].astype(BF16)
            w3b[...] = w3_ref[0].astype(BF16)
            w2b[...] = w2_ref[0].astype(BF16)

        slabs = x_ref.shape[1]
        los, his = [], []
        for s in range(slabs):
            lo, hi = _unpack_bf16_pair(x_ref[:, s, :])
            los.append(lo.astype(BF16))
            his.append(hi.astype(BF16))
        x = jnp.concatenate(los + his, axis=1)
        h1 = _dot(x, w1b[...])
        h3 = _dot(x, w3b[...])
        a = (_silu(h1) * h3).astype(BF16)
        y = _dot(a, w2b[...])
        half = y.shape[1] // 2
        for s in range(slabs):
            y_ref[:, s, :] = _pack_bf16_pair(y[:, s * LANES:(s + 1) * LANES],
                                             y[:, half + s * LANES:half + (s + 1) * LANES])

    @pl.when(i >= nused_ref[0])
    def _():
        y_ref[...] = jnp.zeros_like(y_ref)


def _experts(blk_e, nused, xs, w1, w3, w2):
    n_rows, slabs, _ = xs.shape
    e, d, f = w1.shape
    nb = n_rows // EXPERT_BLOCK

    def row_blk(i, blk_e_ref, nused_ref):
        return (jnp.minimum(i, nused_ref[0] - 1), 0, 0)

    def w_blk(i, blk_e_ref, nused_ref):
        return (blk_e_ref[jnp.minimum(i, nused_ref[0] - 1)], 0, 0)

    return pl.pallas_call(
        _expert_kernel,
        out_shape=jax.ShapeDtypeStruct((n_rows, slabs, LANES), U32),
        grid_spec=pltpu.PrefetchScalarGridSpec(
            num_scalar_prefetch=2,
            grid=(nb,),
            in_specs=[pl.BlockSpec((EXPERT_BLOCK, slabs, LANES), row_blk),
                      pl.BlockSpec((1, d, f), w_blk),
                      pl.BlockSpec((1, d, f), w_blk),
                      pl.BlockSpec((1, f, d), w_blk)],
            out_specs=pl.BlockSpec((EXPERT_BLOCK, slabs, LANES), lambda i, *_: (i, 0, 0)),
            scratch_shapes=[pltpu.VMEM((d, f), BF16), pltpu.VMEM((d, f), BF16),
                            pltpu.VMEM((f, d), BF16)],
        ),
        compiler_params=_cparams(("arbitrary",)),
        name="experts",
    )(blk_e, nused, xs, w1, w3, w2)


def _final_kernel(pstart_ref, eidx_ref, rank_ref, gate_ref, u_ref, x1_ref, ada_ref,
                  ws1_ref, ws3_ref, ws2_ref, g2_ref, b2_ref, ys_ref, o_ref, buf, sem):
    tf = u_ref.shape[0]
    slabs = u_ref.shape[1]

    def issue(t, c):
        for k in range(TOP_K):
            src = pstart_ref[eidx_ref[k, t]] + rank_ref[k, t]
            pltpu.make_async_copy(ys_ref.at[pl.ds(src, 1)], buf.at[k, pl.ds(t, 1)], sem).start()
        return c

    lax.fori_loop(0, tf, issue, 0)

    los, his = [], []
    for s in range(slabs):
        lo, hi = _unpack_bf16_pair(u_ref[:, s, :])
        los.append(lo.astype(BF16))
        his.append(hi.astype(BF16))
    u = jnp.concatenate(los + his, axis=1)
    shared = _dot((_silu(_dot(u, ws1_ref[...])) * _dot(u, ws3_ref[...])).astype(BF16), ws2_ref[...])

    for k in range(TOP_K):
        _row_copy_all(buf.at[k], ys_ref, sem).wait()

    gate = gate_ref[...]
    lo_acc = [None] * slabs
    hi_acc = [None] * slabs
    for k in range(TOP_K):
        gk = gate[:, k:k + 1]
        for s in range(slabs):
            lo, hi = _unpack_bf16_pair(buf[k, :, s, :])
            lo_acc[s] = gk * lo if k == 0 else lo_acc[s] + gk * lo
            hi_acc[s] = gk * hi if k == 0 else hi_acc[s] + gk * hi
    routed = jnp.concatenate(lo_acc + hi_acc, axis=1)
    gate2 = ada_ref[0, 5:6, :]
    y = shared + routed
    o_ref[...] = _layer_norm(DEEPNORM_ALPHA * x1_ref[...] + (1.0 + gate2) * y) * g2_ref[...] + b2_ref[...]


def _final(pstart, eidx_t, rank_t, gate, u2p, x1, ada, ws1b, ws3b, ws2b, g2, b2, ys, seq):
    n, d = x1.shape
    slabs = u2p.shape[1]
    tf = TF_FINAL
    steps_per_seq = seq // tf
    smem_tok = pl.BlockSpec((TOP_K, tf), lambda i, *_: (0, i), memory_space=pltpu.SMEM)
    const2 = lambda i, *_: (0, 0)
    return pl.pallas_call(
        _final_kernel,
        out_shape=jax.ShapeDtypeStruct((n, d), F32),
        grid_spec=pltpu.PrefetchScalarGridSpec(
            num_scalar_prefetch=1,
            grid=(n // tf,),
            in_specs=[smem_tok, smem_tok,
                      pl.BlockSpec((tf, TOP_K), lambda i, *_: (i, 0)),
                      pl.BlockSpec((tf, slabs, LANES), lambda i, *_: (i, 0, 0)),
                      pl.BlockSpec((tf, d), lambda i, *_: (i, 0)),
                      pl.BlockSpec((1,) + ada.shape[1:], lambda i, *_: (i // steps_per_seq, 0, 0)),
                      pl.BlockSpec(ws1b.shape, const2),
                      pl.BlockSpec(ws3b.shape, const2),
                      pl.BlockSpec(ws2b.shape, const2),
                      pl.BlockSpec(g2.shape, const2),
                      pl.BlockSpec(b2.shape, const2),
                      pl.BlockSpec(memory_space=pl.ANY)],
            out_specs=pl.BlockSpec((tf, d), lambda i, *_: (i, 0)),
            scratch_shapes=[pltpu.VMEM((TOP_K, tf, slabs, LANES), U32), pltpu.SemaphoreType.DMA],
        ),
        compiler_params=_cparams(("arbitrary",)),
        name="final",
    )(pstart, eidx_t, rank_t, gate, u2p, x1, ada, ws1b, ws3b, ws2b, g2, b2, ys)


def _swa_head_permutation():
    per_group = SWA_HEADS // SWA_KV_HEADS
    cols = []
    for c in range(per_group):
        for g in range(SWA_KV_HEADS):
            h = c + per_group * g
            cols.extend(range(h * HEAD_DIM, (h + 1) * HEAD_DIM))
    return np.asarray(cols)


def _layer(x, c, positions, w_ada, b_ada, w_in, sinks, g_swa, g_sb, w_out, ln1_g, ln1_b,
           w_router, e_bias, w1, w3, w2, ws1, ws3, ws2, ln2_g, ln2_b):
    batch, seq, d = x.shape
    n = batch * seq
    x2 = x.reshape(n, d)
    perm = _swa_head_permutation()
    per_group = SWA_HEADS // SWA_KV_HEADS

    ada = _ada(c, w_ada, b_ada)

    o_q = SWA_WIDTH + 2 * SWA_KV_WIDTH
    w_in_b = jnp.concatenate(
        [w_in[:, :SWA_WIDTH][:, perm], w_in[:, SWA_WIDTH:o_q],
         w_in[:, o_q:o_q + SB_WIDTH] * (HEAD_DIM ** -0.5), w_in[:, o_q + SB_WIDTH:]], axis=1).astype(BF16)
    half = HEAD_DIM // 2
    inv_freq = ROPE_THETA ** (-jnp.arange(half, dtype=F32) * 2.0 / HEAD_DIM)
    invf = jnp.tile(inv_freq, LANES // half).reshape(1, LANES)
    pos_col = positions.reshape(n, 1).astype(F32)
    qa, ka, va, qs, ks, vs = _inproj(x2, ada, pos_col, invf, w_in_b, seq)

    head_of_block = np.asarray([c_ + per_group * g for g in range(SWA_KV_HEADS) for c_ in range(per_group)])
    sink_col = jnp.repeat(sinks.astype(F32)[head_of_block], WINDOW).reshape(SWA_HEADS * WINDOW, 1)
    oa = _swa(qa, ka, va, sink_col, batch, seq)
    ob = _sb(qs, ks, vs, batch, seq)

    w_out_b = jnp.concatenate([w_out[:SWA_WIDTH][perm], w_out[SWA_WIDTH:]], axis=0).astype(BF16)
    wr_t = w_router.T.astype(F32)
    wr_hi = wr_t.astype(BF16)
    wr_lo = (wr_t - wr_hi.astype(F32)).astype(BF16)
    x1, u2p, logits_t = _post(oa, ob, x2, ada, g_swa[perm].reshape(1, -1), g_sb.reshape(1, -1), w_out_b,
                              ln1_g.reshape(1, d), ln1_b.reshape(1, d), wr_hi, wr_lo, seq)

    eidx_t, gate_t, rank_t, cnt = _route(logits_t, e_bias)

    counts = cnt[:, 0].astype(I32)
    padded = (counts + EXPERT_BLOCK - 1) // EXPERT_BLOCK * EXPERT_BLOCK
    pend = jnp.cumsum(padded).astype(I32)
    pstart = pend - padded
    n_rows = n * TOP_K + N_EXPERTS * EXPERT_BLOCK
    n_blocks = n_rows // EXPERT_BLOCK
    blk_e = jnp.minimum(jnp.searchsorted(pend, jnp.arange(n_blocks, dtype=I32) * EXPERT_BLOCK, side='right'),
                        N_EXPERTS - 1).astype(I32)
    nused = (pend[-1:] // EXPERT_BLOCK).astype(I32)

    xs = _dispatch(pstart, pend, nused, eidx_t, rank_t, u2p, n_rows)
    ys = _experts(blk_e, nused, xs, w1, w3, w2)
    out = _final(pstart, eidx_t, rank_t, gate_t.T, u2p, x1, ada, ws1.astype(BF16), ws3.astype(BF16),
                 ws2.astype(BF16), ln2_g.reshape(1, d), ln2_b.reshape(1, d), ys, seq)
    return out.reshape(batch, seq, d)


def kernel(x, c, positions, w_ada, b_ada, w_in, attn_sinks, g_swa, g_sb, w_out, ln1_g, ln1_b,
           w_router, e_bias, w1, w3, w2, ws1, ws3, ws2, ln2_g, ln2_b):
    assert w_ada.shape[0] == DEPTH
    for l in range(DEPTH):
        x = _layer(x, c, positions, w_ada[l], b_ada[l], w_in[l], attn_sinks[l], g_swa[l], g_sb[l],
                   w_out[l], ln1_g[l], ln1_b[l], w_router[l], e_bias[l], w1[l], w3[l], w2[l],
                   ws1[l], ws3[l], ws2[l], ln2_g[l], ln2_b[l])
    return x
```

```python
import functools

import numpy as np
import jax
import jax.numpy as jnp
from jax import lax
from jax.experimental import pallas as pl
from jax.experimental.pallas import tpu as pltpu

F32 = jnp.float32
BF16 = jnp.bfloat16
I32 = jnp.int32

HEAD_DIM = 64
SWA_HEADS = 8
SWA_KV_HEADS = 2
SB_HEADS = 8
SWA_WIDTH = SWA_HEADS * HEAD_DIM
SWA_KV_WIDTH = SWA_KV_HEADS * HEAD_DIM
SB_WIDTH = SB_HEADS * HEAD_DIM
WINDOW = 128
ROPE_THETA = 10000.0
N_EXPERTS = 256
TOP_K = 8
N_GROUPS = 8
TOPK_GROUPS = 4
GROUP_SIZE = N_EXPERTS // N_GROUPS
ROUTED_SCALE = 2.5
LN_EPS = 1e-5
DEPTH = 1
DEEPNORM_ALPHA = (2 * DEPTH) ** 0.25

LANES = 128
SB_BLOCK = 128
SB_TILE_BLOCKS = 3
SB_UNDERFLOW = 110.0
NEG_BIG = -1e30
VMEM_LIMIT = 56 * 1024 * 1024

TM_PROJ = 512
TQ_SWA = 512
TT_ROUTE = 512
EXPERT_BLOCK = 512
ZERO_ROWS = 128
TD_DISPATCH = 256
TF_FINAL = 256


def _cparams(sem, vmem=VMEM_LIMIT):
    return pltpu.CompilerParams(dimension_semantics=sem, vmem_limit_bytes=vmem)


def _layer_norm(x):
    mu = jnp.mean(x, axis=-1, keepdims=True)
    xc = x - mu
    var = jnp.mean(xc * xc, axis=-1, keepdims=True)
    return xc * lax.rsqrt(var + LN_EPS)


def _silu(x):
    return x * jax.nn.sigmoid(x)


def _dot(a, b):
    return jnp.dot(a, b, preferred_element_type=F32)


def _dot_nt(a, b):
    return lax.dot_general(a, b, (((1,), (1,)), ((), ())), preferred_element_type=F32)


def _store_slab_rows(ref, x):
    t, d = x.shape
    slabs = d // LANES
    for s in range(slabs):
        ref[pl.ds(s, t, stride=slabs), :] = x[:, s * LANES:(s + 1) * LANES]


def _load_slab_rows(ref, slabs):
    t = ref.shape[0] // slabs
    return [ref[pl.ds(s, t, stride=slabs), :] for s in range(slabs)]


def _ada_kernel(c_ref, w_ref, b_ref, o_ref):
    sc = _silu(c_ref[...])
    o_ref[...] = _dot(sc.astype(BF16), w_ref[...].astype(BF16)) + b_ref[...]


def _ada(c, w_ada, b_ada):
    b, d = c.shape
    n_out = w_ada.shape[1]
    rows = 8
    c_pad = jnp.zeros((rows, d), F32).at[:b].set(c)
    out = pl.pallas_call(
        _ada_kernel,
        out_shape=jax.ShapeDtypeStruct((rows, n_out), F32),
        grid=(n_out // d,),
        in_specs=[
            pl.BlockSpec((rows, d), lambda j: (0, 0)),
            pl.BlockSpec((d, d), lambda j: (0, j)),
            pl.BlockSpec((1, d), lambda j: (0, j)),
        ],
        out_specs=pl.BlockSpec((rows, d), lambda j: (0, j)),
        compiler_params=_cparams(("arbitrary",)),
        name="ada",
    )(c_pad, w_ada, b_ada.reshape(1, n_out))
    return out[:b].reshape(b, n_out // d, d)


def _inproj_kernel(x_ref, ada_ref, pos_ref, invf_ref, w_ref,
                   qa_ref, ka_ref, va_ref, qs_ref, ks_ref, vs_ref):
    x = x_ref[...]
    shift = ada_ref[0, 0:1, :]
    scale = ada_ref[0, 1:2, :]
    u = _layer_norm(x) * (1.0 + scale) + shift
    h = _dot(u.astype(BF16), w_ref[...])

    ang = pos_ref[...] * invf_ref[...]
    cs = jnp.cos(ang)
    sn = jnp.sin(ang)
    lane = lax.broadcasted_iota(I32, (1, LANES), 1)
    first = (lane & (HEAD_DIM // 2)) == 0
    sn_signed = jnp.where(first, -sn, sn)

    def rope(hc):
        partner = jnp.where(first, pltpu.roll(hc, LANES - HEAD_DIM // 2, 1),
                            pltpu.roll(hc, HEAD_DIM // 2, 1))
        return hc * cs + partner * sn_signed

    q_scale = HEAD_DIM ** -0.5
    o = 0
    for c in range(SWA_WIDTH // LANES):
        qa_ref[:, c * LANES:(c + 1) * LANES] = (rope(h[:, o:o + LANES]) * q_scale).astype(BF16)
        o += LANES
    ka_ref[...] = rope(h[:, o:o + SWA_KV_WIDTH]).astype(BF16)
    o += SWA_KV_WIDTH
    va_ref[...] = h[:, o:o + SWA_KV_WIDTH].astype(BF16)
    o += SWA_KV_WIDTH
    qs_ref[...] = h[:, o:o + SB_WIDTH].astype(BF16)
    o += SB_WIDTH
    ks_ref[...] = h[:, o:o + SB_WIDTH].astype(BF16)
    o += SB_WIDTH
    vs_ref[...] = h[:, o:o + SB_WIDTH].astype(BF16)


def _inproj(x2, ada, pos_col, invf, w_in_b, seq):
    n, d = x2.shape
    tm = TM_PROJ
    steps_per_seq = seq // tm
    widths = (SWA_WIDTH, SWA_KV_WIDTH, SWA_KV_WIDTH, SB_WIDTH, SB_WIDTH, SB_WIDTH)
    return pl.pallas_call(
        _inproj_kernel,
        out_shape=[jax.ShapeDtypeStruct((n, w), BF16) for w in widths],
        grid=(n // tm,),
        in_specs=[
            pl.BlockSpec((tm, d), lambda i: (i, 0)),
            pl.BlockSpec((1,) + ada.shape[1:], lambda i: (i // steps_per_seq, 0, 0)),
            pl.BlockSpec((tm, 1), lambda i: (i, 0)),
            pl.BlockSpec((1, LANES), lambda i: (0, 0)),
            pl.BlockSpec(w_in_b.shape, lambda i: (0, 0)),
        ],
        out_specs=[pl.BlockSpec((tm, w), lambda i: (i, 0)) for w in widths],
        compiler_params=_cparams(("arbitrary",)),
        name="inproj",
    )(x2, ada, pos_col, invf, w_in_b)


def _swa_kernel(q_ref, kc_ref, kp_ref, vc_ref, vp_ref, bias0_ref, bias_ref, sink_ref,
                o_ref, kall, vall):
    tq = q_ref.shape[0]
    kall[0:WINDOW, :] = kp_ref[...]
    kall[WINDOW:, :] = kc_ref[...]
    vall[0:WINDOW, :] = vp_ref[...]
    vall[WINDOW:, :] = vc_ref[...]
    lane = lax.broadcasted_iota(I32, (1, LANES), 1)
    half0 = lane < HEAD_DIM
    half1 = lane >= HEAD_DIM
    sink = sink_ref[...]
    n_col = SWA_WIDTH // LANES
    for j in range(tq // WINDOW):
        r0 = j * WINDOW
        q = q_ref[r0:r0 + WINDOW, :]
        parts = []
        for half in (half0, half1):
            for c in range(n_col):
                parts.append(jnp.where(half, q[:, c * LANES:(c + 1) * LANES], jnp.zeros((), BF16)))
        qm = jnp.concatenate(parts, axis=0)
        kb = kall[r0:r0 + 2 * WINDOW, :]
        vb = vall[r0:r0 + 2 * WINDOW, :]
        s = _dot_nt(qm, kb)
        s = s + (bias0_ref[0] if j == 0 else bias_ref[...])
        m = jnp.maximum(jnp.max(s, axis=1, keepdims=True), sink)
        p = jnp.exp(s - m)
        den = jnp.sum(p, axis=1, keepdims=True) + jnp.exp(sink - m)
        o = _dot(p.astype(BF16), vb) / den
        for c in range(n_col):
            lo = o[c * WINDOW:(c + 1) * WINDOW]
            hi = o[(n_col + c) * WINDOW:(n_col + c + 1) * WINDOW]
            o_ref[r0:r0 + WINDOW, c * LANES:(c + 1) * LANES] = jnp.where(half0, lo, hi)


def _swa_bias():
    qi = np.arange(SWA_HEADS * WINDOW)[:, None] % WINDOW
    kj = np.arange(2 * WINDOW)[None, :]
    dist = qi + WINDOW - kj
    band = (dist >= 0) & (dist < WINDOW)
    bias = np.where(band, 0.0, NEG_BIG).astype(np.float32)
    first = np.where(band & (kj >= WINDOW), 0.0, NEG_BIG).astype(np.float32)
    return jnp.asarray(np.stack([bias, first])), jnp.asarray(bias)


def _swa(qa, ka, va, sink_col, batch, seq):
    n = qa.shape[0]
    tq = TQ_SWA
    nt = seq // tq
    blocks_per_tile = tq // WINDOW
    bias_pair, bias = _swa_bias()

    def cur(b, i):
        return (b * nt + i, 0)

    def prev(b, i):
        return (jnp.maximum((b * nt + i) * blocks_per_tile - 1, 0), 0)

    return pl.pallas_call(
        _swa_kernel,
        out_shape=jax.ShapeDtypeStruct((n, SWA_WIDTH), F32),
        grid=(batch, nt),
        in_specs=[
            pl.BlockSpec((tq, SWA_WIDTH), cur),
            pl.BlockSpec((tq, SWA_KV_WIDTH), cur),
            pl.BlockSpec((WINDOW, SWA_KV_WIDTH), prev),
            pl.BlockSpec((tq, SWA_KV_WIDTH), cur),
            pl.BlockSpec((WINDOW, SWA_KV_WIDTH), prev),
            pl.BlockSpec((1,) + bias.shape, lambda b, i: (jnp.where(i == 0, 1, 0), 0, 0)),
            pl.BlockSpec(bias.shape, lambda b, i: (0, 0)),
            pl.BlockSpec(sink_col.shape, lambda b, i: (0, 0)),
        ],
        out_specs=pl.BlockSpec((tq, SWA_WIDTH), cur),
        scratch_shapes=[pltpu.VMEM((tq + WINDOW, SWA_KV_WIDTH), BF16),
                        pltpu.VMEM((tq + WINDOW, SWA_KV_WIDTH), BF16)],
        compiler_params=_cparams(("arbitrary", "arbitrary")),
        name="swa",
    )(qa, ka, ka, va, va, bias_pair, bias, sink_col)


def _sb_kernel(q_ref, k_ref, v_ref, uu_ref, o_ref, qm, acc, run):
    i = pl.program_id(1)
    blk = SB_BLOCK
    n_sub = SB_TILE_BLOCKS
    tile = n_sub * blk
    pairs = q_ref.shape[1] // LANES
    lane = lax.broadcasted_iota(I32, (1, LANES), 1)
    half0 = lane < HEAD_DIM
    half1 = lane >= HEAD_DIM
    zero = jnp.zeros((), BF16)
    for p in range(pairs):
        q = q_ref[:, p * LANES:(p + 1) * LANES]
        qm[p] = jnp.concatenate([jnp.where(half0, q, zero), jnp.where(half1, q, zero)], axis=0)
    acc[...] = jnp.zeros_like(acc)
    run[...] = jnp.zeros_like(run)
    q_pos = i * blk + (lax.broadcasted_iota(I32, (2 * blk, blk), 0) & (blk - 1))
    col = lax.broadcasted_iota(I32, (2 * blk, blk), 1)

    def body(carry):
        end, _ = carry
        start = pl.multiple_of(jnp.maximum(end - tile, 0), blk)
        limit = jnp.minimum(end, q_pos) - start
        least = None
        for p in range(pairs):
            kt = k_ref[pl.ds(start, tile), p * LANES:(p + 1) * LANES]
            vt = v_ref[pl.ds(start, tile), p * LANES:(p + 1) * LANES]
            z = _dot_nt(qm[p], kt)
            mass_seen = run[p]
            ws = [None] * n_sub
            for b in reversed(range(n_sub)):
                zb = jnp.where(col < limit - b * blk, z[:, b * blk:(b + 1) * blk], NEG_BIG)
                mass = jnp.maximum(zb, 0.0) + jnp.log(1.0 + jnp.exp(-jnp.abs(zb)))
                mass_hi = mass.astype(BF16)
                mass_lo = (mass - mass_hi.astype(F32)).astype(BF16)
                cc = _dot(jnp.concatenate([mass_hi, mass_lo], axis=1), uu_ref[...])
                ws[b] = jnp.exp(zb - (cc[:, :blk] + mass_seen)).astype(BF16)
                mass_seen = mass_seen + cc[:, blk:]
            acc[p] += _dot(jnp.concatenate(ws, axis=1), vt)
            run[p] = mass_seen
            least = mass_seen if least is None else jnp.minimum(least, mass_seen)
        return start, jnp.min(least) > SB_UNDERFLOW

    lax.while_loop(lambda c: jnp.logical_and(c[0] > 0, jnp.logical_not(c[1])),
                   body, ((i + 1) * blk, jnp.bool_(False)))
    for p in range(pairs):
        a = acc[p]
        o_ref[:, p * LANES:(p + 1) * LANES] = jnp.where(half0, a[:blk], a[blk:])


def _sb_suffix_matrix():
    j = np.arange(2 * SB_BLOCK)[:, None] % SB_BLOCK
    s = np.arange(2 * SB_BLOCK)[None, :]
    m = np.where(s < SB_BLOCK, (j >= s), True)
    return jnp.asarray(m.astype(np.float32)).astype(BF16)


def _sb(qs, ks, vs, batch, seq):
    n = qs.shape[0]
    blk = SB_BLOCK
    nq = seq // blk
    pairs = SB_WIDTH // LANES
    uu = _sb_suffix_matrix()
    return pl.pallas_call(
        _sb_kernel,
        out_shape=jax.ShapeDtypeStruct((n, SB_WIDTH), F32),
        grid=(batch, nq),
        in_specs=[
            pl.BlockSpec((blk, SB_WIDTH), lambda b, i: (b * nq + i, 0)),
            pl.BlockSpec((seq, SB_WIDTH), lambda b, i: (b, 0)),
            pl.BlockSpec((seq, SB_WIDTH), lambda b, i: (b, 0)),
            pl.BlockSpec(uu.shape, lambda b, i: (0, 0)),
        ],
        out_specs=pl.BlockSpec((blk, SB_WIDTH), lambda b, i: (b * nq + i, 0)),
        scratch_shapes=[pltpu.VMEM((pairs, 2 * blk, LANES), BF16),
                        pltpu.VMEM((pairs, 2 * blk, LANES), F32),
                        pltpu.VMEM((pairs, 2 * blk, LANES), F32)],
        compiler_params=_cparams(("arbitrary", "arbitrary")),
        name="sb",
    )(qs, ks, vs, uu)


def _post_kernel(oa_ref, ob_ref, x_ref, ada_ref, ga_ref, gb_ref, wo_ref, g1_ref, b1_ref,
                 wrh_ref, wrl_ref, x1_ref, u2s_ref, lg_ref):
    def rms(o, g):
        return o * lax.rsqrt(jnp.mean(o * o, axis=-1, keepdims=True) + LN_EPS) * g

    na = rms(oa_ref[...], ga_ref[...]).astype(BF16)
    nb = rms(ob_ref[...], gb_ref[...]).astype(BF16)
    y = _dot(na, wo_ref[0:SWA_WIDTH, :]) + _dot(nb, wo_ref[SWA_WIDTH:, :])
    gate1 = ada_ref[0, 2:3, :]
    shift2 = ada_ref[0, 3:4, :]
    scale2 = ada_ref[0, 4:5, :]
    x1 = _layer_norm(DEEPNORM_ALPHA * x_ref[...] + (1.0 + gate1) * y) * g1_ref[...] + b1_ref[...]
    x1_ref[...] = x1
    u2 = _layer_norm(x1) * (1.0 + scale2) + shift2
    _store_slab_rows(u2s_ref, u2)
    u_hi = u2.astype(BF16)
    u_lo = (u2 - u_hi.astype(F32)).astype(BF16)
    wrh = wrh_ref[...]
    lg_ref[...] = _dot_nt(wrh, u_hi) + (_dot_nt(wrh, u_lo) + _dot_nt(wrl_ref[...], u_hi))


def _post(oa, ob, x2, ada, g_a, g_b, w_out_b, g1, b1, wr_hi, wr_lo, seq):
    n, d = x2.shape
    tm = TM_PROJ
    steps_per_seq = seq // tm
    e = wr_hi.shape[0]
    slabs = d // LANES
    const2 = lambda i: (0, 0)
    return pl.pallas_call(
        _post_kernel,
        out_shape=[jax.ShapeDtypeStruct((n, d), F32),
                   jax.ShapeDtypeStruct((n * slabs, LANES), F32),
                   jax.ShapeDtypeStruct((e, n), F32)],
        grid=(n // tm,),
        in_specs=[
            pl.BlockSpec((tm, SWA_WIDTH), lambda i: (i, 0)),
            pl.BlockSpec((tm, SB_WIDTH), lambda i: (i, 0)),
            pl.BlockSpec((tm, d), lambda i: (i, 0)),
            pl.BlockSpec((1,) + ada.shape[1:], lambda i: (i // steps_per_seq, 0, 0)),
            pl.BlockSpec(g_a.shape, const2),
            pl.BlockSpec(g_b.shape, const2),
            pl.BlockSpec(w_out_b.shape, const2),
            pl.BlockSpec(g1.shape, const2),
            pl.BlockSpec(b1.shape, const2),
            pl.BlockSpec(wr_hi.shape, const2),
            pl.BlockSpec(wr_lo.shape, const2),
        ],
        out_specs=[pl.BlockSpec((tm, d), lambda i: (i, 0)),
                   pl.BlockSpec((tm * slabs, LANES), lambda i: (i, 0)),
                   pl.BlockSpec((e, tm), lambda i: (0, i))],
        compiler_params=_cparams(("arbitrary",)),
        name="post",
    )(oa, ob, x2, ada, g_a, g_b, w_out_b, g1, b1, wr_hi, wr_lo)


def _route_kernel(lg_ref, bias_ref, su_ref, eidx_ref, gate_ref, rank_ref, cnt_ref, seen):
    i = pl.program_id(0)

    @pl.when(i == 0)
    def _():
        seen[...] = jnp.zeros_like(seen)

    tt = lg_ref.shape[1]
    ninf = -jnp.inf
    scores = jax.nn.sigmoid(lg_ref[...])
    biased = scores + bias_ref[...]

    iog32 = lax.broadcasted_iota(I32, (GROUP_SIZE, tt), 0)
    groups = [biased[g * GROUP_SIZE:(g + 1) * GROUP_SIZE, :] for g in range(N_GROUPS)]
    gs_rows = []
    for blk in groups:
        m1 = jnp.max(blk, axis=0, keepdims=True)
        i1 = jnp.min(jnp.where(blk == m1, iog32, GROUP_SIZE), axis=0, keepdims=True)
        m2 = jnp.max(jnp.where(iog32 == i1, ninf, blk), axis=0, keepdims=True)
        gs_rows.append(m1 + m2)
    gs = jnp.concatenate(gs_rows, axis=0)

    iog = lax.broadcasted_iota(I32, gs.shape, 0)
    gsel = jnp.zeros(gs.shape, F32)
    cur = gs
    for _ in range(TOPK_GROUPS):
        m = jnp.max(cur, axis=0, keepdims=True)
        idx = jnp.min(jnp.where(cur == m, iog, N_GROUPS), axis=0, keepdims=True)
        hit = iog == idx
        gsel = jnp.where(hit, 1.0, gsel)
        cur = jnp.where(hit, ninf, cur)

    cand = jnp.concatenate(
        [jnp.where(gsel[g:g + 1, :] > 0.5, groups[g], ninf) for g in range(N_GROUPS)], axis=0)
    ioe = lax.broadcasted_iota(I32, cand.shape, 0)
    chosen = jnp.zeros(cand.shape, F32)
    idxs, gates = [], []
    for _ in range(TOP_K):
        m = jnp.max(cand, axis=0, keepdims=True)
        idx = jnp.min(jnp.where(cand == m, ioe, N_EXPERTS), axis=0, keepdims=True)
        hit = ioe == idx
        gates.append(jnp.sum(jnp.where(hit, scores, 0.0), axis=0, keepdims=True))
        cand = jnp.where(hit, ninf, cand)
        chosen = jnp.where(hit, 1.0, chosen)
        idxs.append(idx)
    gsum = gates[0]
    for g in gates[1:]:
        gsum = gsum + g
    gates = [g / gsum * ROUTED_SCALE for g in gates]

    rk = _dot(chosen.astype(BF16), su_ref[...])
    before = seen[...]
    rank_mat = rk[:, :tt] + jnp.concatenate([before] * (tt // LANES), axis=1)
    after = before + rk[:, tt:]
    seen[...] = after
    cnt_ref[...] = after
    ranks = [jnp.sum(jnp.where(ioe == idx, rank_mat, 0.0), axis=0, keepdims=True) for idx in idxs]
    eidx_ref[...] = jnp.concatenate(idxs, axis=0)
    gate_ref[...] = jnp.concatenate(gates, axis=0)
    rank_ref[...] = jnp.concatenate(ranks, axis=0).astype(I32)


def _route_prefix_matrix(tt):
    a = np.arange(tt)[:, None] < np.arange(tt)[None, :]
    m = np.concatenate([a, np.ones((tt, LANES), bool)], axis=1)
    return jnp.asarray(m.astype(np.float32)).astype(BF16)


def _route(logits_t, e_bias):
    e, n = logits_t.shape
    tt = TT_ROUTE
    bias_b = jnp.broadcast_to(e_bias.astype(F32)[:, None], (e, tt))
    su = _route_prefix_matrix(tt)
    tok = lambda i: (0, i)
    return pl.pallas_call(
        _route_kernel,
        out_shape=[jax.ShapeDtypeStruct((TOP_K, n), I32),
                   jax.ShapeDtypeStruct((TOP_K, n), F32),
                   jax.ShapeDtypeStruct((TOP_K, n), I32),
                   jax.ShapeDtypeStruct((e, LANES), F32)],
        grid=(n // tt,),
        in_specs=[pl.BlockSpec((e, tt), tok),
                  pl.BlockSpec((e, tt), lambda i: (0, 0)),
                  pl.BlockSpec(su.shape, lambda i: (0, 0))],
        out_specs=[pl.BlockSpec((TOP_K, tt), tok), pl.BlockSpec((TOP_K, tt), tok),
                   pl.BlockSpec((TOP_K, tt), tok), pl.BlockSpec((e, LANES), lambda i: (0, 0))],
        scratch_shapes=[pltpu.VMEM((e, LANES), F32)],
        compiler_params=_cparams(("arbitrary",)),
        name="route",
    )(logits_t, bias_b, su)


def _slab(ref, row, slabs):
    return ref.at[pl.ds(pl.multiple_of(row * slabs, slabs), slabs)]


def _dispatch_kernel(pstart_ref, pvalid_ref, pend_ref, nused_ref, eidx_ref, rank_ref, u_ref, xs_ref,
                     zbuf, sem, *, slabs):
    i = pl.program_id(0)
    td = u_ref.shape[0] // slabs
    zrows = zbuf.shape[0] // slabs
    chunks_per_block = EXPERT_BLOCK // zrows
    n_chunks = xs_ref.shape[0] // slabs // zrows

    @pl.when(i == 0)
    def _():
        zbuf[...] = jnp.zeros_like(zbuf)

        def zero_chunk(row0):
            return pltpu.make_async_copy(
                zbuf, xs_ref.at[pl.ds(pl.multiple_of(row0 * slabs, zrows * slabs), zrows * slabs)], sem)

        def start_unused(b, c):
            zero_chunk(b * zrows).start()
            return c

        def wait_unused(b, c):
            zero_chunk(b * zrows).wait()
            return c

        first_unused = nused_ref[0] * chunks_per_block
        lax.fori_loop(first_unused, n_chunks, start_unused, 0)
        lax.fori_loop(first_unused, n_chunks, wait_unused, 0)

        def tails(e, c, *, wait):
            for g in range(chunks_per_block):
                row0 = pend_ref[e] - zrows * (g + 1)

                @pl.when(row0 + zrows > pvalid_ref[e])
                def _():
                    if wait:
                        zero_chunk(row0).wait()
                    else:
                        zero_chunk(row0).start()
            return c

        lax.fori_loop(0, N_EXPERTS, functools.partial(tails, wait=False), 0)
        lax.fori_loop(0, N_EXPERTS, functools.partial(tails, wait=True), 0)

    def issue(t, c):
        for k in range(TOP_K):
            dest = pstart_ref[eidx_ref[k, t]] + rank_ref[k, t]
            pltpu.make_async_copy(_slab(u_ref, t, slabs), _slab(xs_ref, dest, slabs), sem).start(priority=k % 2)
        return c

    lax.fori_loop(0, td, issue, 0)
    for _ in range(TOP_K):
        pltpu.make_async_copy(u_ref, xs_ref.at[pl.ds(0, td * slabs)], sem).wait()


def _dispatch(pstart, pvalid, pend, nused, eidx_t, rank_t, u2s, n_rows, slabs):
    n = u2s.shape[0] // slabs
    td = TD_DISPATCH
    smem_tok = pl.BlockSpec((TOP_K, td), lambda i, *_: (0, i), memory_space=pltpu.SMEM)
    return pl.pallas_call(
        functools.partial(_dispatch_kernel, slabs=slabs),
        out_shape=jax.ShapeDtypeStruct((n_rows * slabs, LANES), F32),
        grid_spec=pltpu.PrefetchScalarGridSpec(
            num_scalar_prefetch=4,
            grid=(n // td,),
            in_specs=[smem_tok, smem_tok,
                      pl.BlockSpec((td * slabs, LANES), lambda i, *_: (i, 0))],
            out_specs=pl.BlockSpec(memory_space=pl.ANY),
            scratch_shapes=[pltpu.VMEM((ZERO_ROWS * slabs, LANES), F32),
                            pltpu.SemaphoreType.DMA],
        ),
        compiler_params=_cparams(("arbitrary",)),
        name="dispatch",
    )(pstart, pvalid, pend, nused, eidx_t, rank_t, u2s)


def _expert_kernel(blk_e_ref, nused_ref, x_ref, w1_ref, w3_ref, w2_ref, y_ref, w1b, w3b, w2b, *, slabs):
    i = pl.program_id(0)

    @pl.when(i < nused_ref[0])
    def _():
        e = blk_e_ref[i]
        prev = blk_e_ref[jnp.maximum(i - 1, 0)]

        @pl.when(jnp.logical_or(i == 0, e != prev))
        def _():
            w1b[...] = w1_ref[0].astype(BF16)
            w3b[...] = w3_ref[0].astype(BF16)
            w2b[...] = w2_ref[0].astype(BF16)

        x = jnp.concatenate([c.astype(BF16) for c in _load_slab_rows(x_ref, slabs)], axis=1)
        h1 = _dot(x, w1b[...])
        h3 = _dot(x, w3b[...])
        a = (_silu(h1) * h3).astype(BF16)
        _store_slab_rows(y_ref, _dot(a, w2b[...]))

    @pl.when(i >= nused_ref[0])
    def _():
        y_ref[...] = jnp.zeros_like(y_ref)


def _experts(blk_e, nused, xs, w1, w3, w2, slabs):
    n_rows = xs.shape[0] // slabs
    e, d, f = w1.shape
    nb = n_rows // EXPERT_BLOCK

    def row_blk(i, blk_e_ref, nused_ref):
        return (jnp.minimum(i, nused_ref[0] - 1), 0)

    def w_blk(i, blk_e_ref, nused_ref):
        return (blk_e_ref[jnp.minimum(i, nused_ref[0] - 1)], 0, 0)

    return pl.pallas_call(
        functools.partial(_expert_kernel, slabs=slabs),
        out_shape=jax.ShapeDtypeStruct((n_rows * slabs, LANES), F32),
        grid_spec=pltpu.PrefetchScalarGridSpec(
            num_scalar_prefetch=2,
            grid=(nb,),
            in_specs=[pl.BlockSpec((EXPERT_BLOCK * slabs, LANES), row_blk),
                      pl.BlockSpec((1, d, f), w_blk),
                      pl.BlockSpec((1, d, f), w_blk),
                      pl.BlockSpec((1, f, d), w_blk)],
            out_specs=pl.BlockSpec((EXPERT_BLOCK * slabs, LANES), lambda i, *_: (i, 0)),
            scratch_shapes=[pltpu.VMEM((d, f), BF16), pltpu.VMEM((d, f), BF16),
                            pltpu.VMEM((f, d), BF16)],
        ),
        compiler_params=_cparams(("arbitrary",)),
        name="experts",
    )(blk_e, nused, xs, w1, w3, w2)


def _final_kernel(pstart_ref, eidx_ref, rank_ref, gate_ref, u_ref, x1_ref, ada_ref,
                  ws1_ref, ws3_ref, ws2_ref, g2_ref, b2_ref, ys_ref, o_ref, buf, sem, *, slabs):
    tf = u_ref.shape[0] // slabs

    def issue(t, c):
        for k in range(TOP_K):
            src = pstart_ref[eidx_ref[k, t]] + rank_ref[k, t]
            pltpu.make_async_copy(_slab(ys_ref, src, slabs), _slab(buf.at[k], t, slabs), sem).start(priority=k % 2)
        return c

    lax.fori_loop(0, tf, issue, 0)

    u = jnp.concatenate([c.astype(BF16) for c in _load_slab_rows(u_ref, slabs)], axis=1)
    shared = _dot((_silu(_dot(u, ws1_ref[...])) * _dot(u, ws3_ref[...])).astype(BF16), ws2_ref[...])

    for k in range(TOP_K):
        pltpu.make_async_copy(ys_ref.at[pl.ds(0, tf * slabs)], buf.at[k], sem).wait()

    gate = gate_ref[...]
    acc = None
    for k in range(TOP_K):
        gk = gate[:, k:k + 1]
        chunks = _load_slab_rows(buf.at[k], slabs)
        acc = [gk * c for c in chunks] if acc is None else [a + gk * c for a, c in zip(acc, chunks)]
    routed = jnp.concatenate(acc, axis=1)
    gate2 = ada_ref[0, 5:6, :]
    y = shared + routed
    o_ref[...] = _layer_norm(DEEPNORM_ALPHA * x1_ref[...] + (1.0 + gate2) * y) * g2_ref[...] + b2_ref[...]


def _final(pstart, eidx_t, rank_t, gate, u2s, x1, ada, ws1b, ws3b, ws2b, g2, b2, ys, seq, slabs):
    n, d = x1.shape
    tf = TF_FINAL
    steps_per_seq = seq // tf
    smem_tok = pl.BlockSpec((TOP_K, tf), lambda i, *_: (0, i), memory_space=pltpu.SMEM)
    const2 = lambda i, *_: (0, 0)
    return pl.pallas_call(
        functools.partial(_final_kernel, slabs=slabs),
        out_shape=jax.ShapeDtypeStruct((n, d), F32),
        grid_spec=pltpu.PrefetchScalarGridSpec(
            num_scalar_prefetch=1,
            grid=(n // tf,),
            in_specs=[smem_tok, smem_tok,
                      pl.BlockSpec((tf, TOP_K), lambda i, *_: (i, 0)),
                      pl.BlockSpec((tf * slabs, LANES), lambda i, *_: (i, 0)),
                      pl.BlockSpec((tf, d), lambda i, *_: (i, 0)),
                      pl.BlockSpec((1,) + ada.shape[1:], lambda i, *_: (i // steps_per_seq, 0, 0)),
                      pl.BlockSpec(ws1b.shape, const2),
                      pl.BlockSpec(ws3b.shape, const2),
                      pl.BlockSpec(ws2b.shape, const2),
                      pl.BlockSpec(g2.shape, const2),
                      pl.BlockSpec(b2.shape, const2),
                      pl.BlockSpec(memory_space=pl.ANY)],
            out_specs=pl.BlockSpec((tf, d), lambda i, *_: (i, 0)),
            scratch_shapes=[pltpu.VMEM((TOP_K, tf * slabs, LANES), F32), pltpu.SemaphoreType.DMA],
        ),
        compiler_params=_cparams(("arbitrary",)),
        name="final",
    )(pstart, eidx_t, rank_t, gate, u2s, x1, ada, ws1b, ws3b, ws2b, g2, b2, ys)


def _swa_head_permutation():
    per_group = SWA_HEADS // SWA_KV_HEADS
    cols = []
    for c in range(per_group):
        for g in range(SWA_KV_HEADS):
            h = c + per_group * g
            cols.extend(range(h * HEAD_DIM, (h + 1) * HEAD_DIM))
    return np.asarray(cols)


def _layer(x, c, positions, w_ada, b_ada, w_in, sinks, g_swa, g_sb, w_out, ln1_g, ln1_b,
           w_router, e_bias, w1, w3, w2, ws1, ws3, ws2, ln2_g, ln2_b):
    batch, seq, d = x.shape
    n = batch * seq
    x2 = x.reshape(n, d)
    perm = _swa_head_permutation()
    per_group = SWA_HEADS // SWA_KV_HEADS

    ada = _ada(c, w_ada, b_ada)

    o_q = SWA_WIDTH + 2 * SWA_KV_WIDTH
    w_in_b = jnp.concatenate(
        [w_in[:, :SWA_WIDTH][:, perm], w_in[:, SWA_WIDTH:o_q],
         w_in[:, o_q:o_q + SB_WIDTH] * (HEAD_DIM ** -0.5), w_in[:, o_q + SB_WIDTH:]], axis=1).astype(BF16)
    half = HEAD_DIM // 2
    inv_freq = ROPE_THETA ** (-jnp.arange(half, dtype=F32) * 2.0 / HEAD_DIM)
    invf = jnp.tile(inv_freq, LANES // half).reshape(1, LANES)
    pos_col = positions.reshape(n, 1).astype(F32)
    qa, ka, va, qs, ks, vs = _inproj(x2, ada, pos_col, invf, w_in_b, seq)

    head_of_block = np.asarray([c_ + per_group * g for g in range(SWA_KV_HEADS) for c_ in range(per_group)])
    sink_col = jnp.repeat(sinks.astype(F32)[head_of_block], WINDOW).reshape(SWA_HEADS * WINDOW, 1)
    oa = _swa(qa, ka, va, sink_col, batch, seq)
    ob = _sb(qs, ks, vs, batch, seq)

    w_out_b = jnp.concatenate([w_out[:SWA_WIDTH][perm], w_out[SWA_WIDTH:]], axis=0).astype(BF16)
    wr_t = w_router.T.astype(F32)
    wr_hi = wr_t.astype(BF16)
    wr_lo = (wr_t - wr_hi.astype(F32)).astype(BF16)
    x1, u2s, logits_t = _post(oa, ob, x2, ada, g_swa[perm].reshape(1, -1), g_sb.reshape(1, -1), w_out_b,
                              ln1_g.reshape(1, d), ln1_b.reshape(1, d), wr_hi, wr_lo, seq)

    eidx_t, gate_t, rank_t, cnt = _route(logits_t, e_bias)

    counts = cnt[:, 0].astype(I32)
    padded = (counts + EXPERT_BLOCK - 1) // EXPERT_BLOCK * EXPERT_BLOCK
    pend = jnp.cumsum(padded).astype(I32)
    pstart = pend - padded
    n_rows = n * TOP_K + N_EXPERTS * EXPERT_BLOCK
    n_blocks = n_rows // EXPERT_BLOCK
    block_row0 = jnp.arange(n_blocks, dtype=I32) * EXPERT_BLOCK
    blk_e = jnp.minimum(jnp.sum((pend[None, :] <= block_row0[:, None]).astype(I32), axis=1), N_EXPERTS - 1)
    nused = (pend[-1:] // EXPERT_BLOCK).astype(I32)
    slabs = d // LANES

    xs = _dispatch(pstart, pstart + counts, pend, nused, eidx_t, rank_t, u2s, n_rows, slabs)
    ys = _experts(blk_e, nused, xs, w1, w3, w2, slabs)
    out = _final(pstart, eidx_t, rank_t, gate_t.T, u2s, x1, ada, ws1.astype(BF16), ws3.astype(BF16),
                 ws2.astype(BF16), ln2_g.reshape(1, d), ln2_b.reshape(1, d), ys, seq, slabs)
    return out.reshape(batch, seq, d)


def kernel(x, c, positions, w_ada, b_ada, w_in, attn_sinks, g_swa, g_sb, w_out, ln1_g, ln1_b,
           w_router, e_bias, w1, w3, w2, ws1, ws3, ws2, ln2_g, ln2_b):
    assert w_ada.shape[0] == DEPTH
    for l in range(DEPTH):
        x = _layer(x, c, positions, w_ada[l], b_ada[l], w_in[l], attn_sinks[l], g_swa[l], g_sb[l],
                   w_out[l], ln1_g[l], ln1_b[l], w_router[l], e_bias[l], w1[l], w3[l], w2[l],
                   ws1[l], ws3[l], ws2[l], ln2_g[l], ln2_b[l])
    return x
```

```python
import functools

import numpy as np
import jax
import jax.numpy as jnp
from jax import lax
from jax.experimental import pallas as pl
from jax.experimental.pallas import tpu as pltpu

F32 = jnp.float32
BF16 = jnp.bfloat16
I32 = jnp.int32

HEAD_DIM = 64
SWA_HEADS = 8
SWA_KV_HEADS = 2
SB_HEADS = 8
SWA_WIDTH = SWA_HEADS * HEAD_DIM
SWA_KV_WIDTH = SWA_KV_HEADS * HEAD_DIM
SB_WIDTH = SB_HEADS * HEAD_DIM
WINDOW = 128
ROPE_THETA = 10000.0
N_EXPERTS = 256
TOP_K = 8
N_GROUPS = 8
TOPK_GROUPS = 4
GROUP_SIZE = N_EXPERTS // N_GROUPS
ROUTED_SCALE = 2.5
LN_EPS = 1e-5
DEPTH = 1
DEEPNORM_ALPHA = (2 * DEPTH) ** 0.25

LANES = 128
SB_BLOCK = 128
SB_TILE_BLOCKS = 3
SB_UNDERFLOW = 110.0
NEG_BIG = -1e30
VMEM_LIMIT = 56 * 1024 * 1024

TM_PROJ = 512
TQ_SWA = 512
T_MOE = 512
EXPERT_BLOCK = 512
ZERO_ROWS = 128
SEGMENT_CHUNK = 32


def _cparams(sem, vmem=VMEM_LIMIT):
    return pltpu.CompilerParams(dimension_semantics=sem, vmem_limit_bytes=vmem)


def _layer_norm(x):
    mu = jnp.mean(x, axis=-1, keepdims=True)
    xc = x - mu
    var = jnp.mean(xc * xc, axis=-1, keepdims=True)
    return xc * lax.rsqrt(var + LN_EPS)


def _silu(x):
    return x * jax.nn.sigmoid(x)


def _dot(a, b):
    return jnp.dot(a, b, preferred_element_type=F32)


def _dot_nt(a, b):
    return lax.dot_general(a, b, (((1,), (1,)), ((), ())), preferred_element_type=F32)


def _store_slab_rows(ref, x):
    t, d = x.shape
    slabs = d // LANES
    for s in range(slabs):
        ref[pl.ds(s, t, stride=slabs), :] = x[:, s * LANES:(s + 1) * LANES]


def _load_slab_rows(ref, slabs):
    t = ref.shape[0] // slabs
    return [ref[pl.ds(s, t, stride=slabs), :] for s in range(slabs)]


def _ada_kernel(c_ref, w_ref, b_ref, o_ref):
    sc = _silu(c_ref[...])
    o_ref[...] = _dot(sc.astype(BF16), w_ref[...].astype(BF16)) + b_ref[...]


def _ada(c, w_ada, b_ada):
    b, d = c.shape
    n_out = w_ada.shape[1]
    rows = 8
    c_pad = jnp.zeros((rows, d), F32).at[:b].set(c)
    out = pl.pallas_call(
        _ada_kernel,
        out_shape=jax.ShapeDtypeStruct((rows, n_out), F32),
        grid=(n_out // d,),
        in_specs=[
            pl.BlockSpec((rows, d), lambda j: (0, 0)),
            pl.BlockSpec((d, d), lambda j: (0, j)),
            pl.BlockSpec((1, d), lambda j: (0, j)),
        ],
        out_specs=pl.BlockSpec((rows, d), lambda j: (0, j)),
        compiler_params=_cparams(("arbitrary",)),
        name="ada",
    )(c_pad, w_ada, b_ada.reshape(1, n_out))
    return out[:b].reshape(b, n_out // d, d)


def _inproj_kernel(x_ref, ada_ref, pos_ref, invf_ref, w_ref,
                   qa_ref, ka_ref, va_ref, qs_ref, ks_ref, vs_ref):
    x = x_ref[...]
    shift = ada_ref[0, 0:1, :]
    scale = ada_ref[0, 1:2, :]
    u = _layer_norm(x) * (1.0 + scale) + shift
    h = _dot(u.astype(BF16), w_ref[...])

    ang = pos_ref[...] * invf_ref[...]
    cs = jnp.cos(ang)
    sn = jnp.sin(ang)
    lane = lax.broadcasted_iota(I32, (1, LANES), 1)
    first = (lane & (HEAD_DIM // 2)) == 0
    sn_signed = jnp.where(first, -sn, sn)

    def rope(hc):
        partner = jnp.where(first, pltpu.roll(hc, LANES - HEAD_DIM // 2, 1),
                            pltpu.roll(hc, HEAD_DIM // 2, 1))
        return hc * cs + partner * sn_signed

    q_scale = HEAD_DIM ** -0.5
    o = 0
    for c in range(SWA_WIDTH // LANES):
        qa_ref[:, c * LANES:(c + 1) * LANES] = (rope(h[:, o:o + LANES]) * q_scale).astype(BF16)
        o += LANES
    ka_ref[...] = rope(h[:, o:o + SWA_KV_WIDTH]).astype(BF16)
    o += SWA_KV_WIDTH
    va_ref[...] = h[:, o:o + SWA_KV_WIDTH].astype(BF16)
    o += SWA_KV_WIDTH
    qs_ref[...] = h[:, o:o + SB_WIDTH].astype(BF16)
    o += SB_WIDTH
    ks_ref[...] = h[:, o:o + SB_WIDTH].astype(BF16)
    o += SB_WIDTH
    vs_ref[...] = h[:, o:o + SB_WIDTH].astype(BF16)


def _inproj(x2, ada, pos_col, invf, w_in_b, seq):
    n, d = x2.shape
    tm = TM_PROJ
    steps_per_seq = seq // tm
    widths = (SWA_WIDTH, SWA_KV_WIDTH, SWA_KV_WIDTH, SB_WIDTH, SB_WIDTH, SB_WIDTH)
    return pl.pallas_call(
        _inproj_kernel,
        out_shape=[jax.ShapeDtypeStruct((n, w), BF16) for w in widths],
        grid=(n // tm,),
        in_specs=[
            pl.BlockSpec((tm, d), lambda i: (i, 0)),
            pl.BlockSpec((1,) + ada.shape[1:], lambda i: (i // steps_per_seq, 0, 0)),
            pl.BlockSpec((tm, 1), lambda i: (i, 0)),
            pl.BlockSpec((1, LANES), lambda i: (0, 0)),
            pl.BlockSpec(w_in_b.shape, lambda i: (0, 0)),
        ],
        out_specs=[pl.BlockSpec((tm, w), lambda i: (i, 0)) for w in widths],
        compiler_params=_cparams(("arbitrary",)),
        name="inproj",
    )(x2, ada, pos_col, invf, w_in_b)


def _swa_kernel(q_ref, kc_ref, kp_ref, vc_ref, vp_ref, bias0_ref, bias_ref, sink_ref,
                o_ref, kall, vall):
    tq = q_ref.shape[0]
    kall[0:WINDOW, :] = kp_ref[...]
    kall[WINDOW:, :] = kc_ref[...]
    vall[0:WINDOW, :] = vp_ref[...]
    vall[WINDOW:, :] = vc_ref[...]
    lane = lax.broadcasted_iota(I32, (1, LANES), 1)
    half0 = lane < HEAD_DIM
    half1 = lane >= HEAD_DIM
    sink = sink_ref[...]
    n_col = SWA_WIDTH // LANES
    for j in range(tq // WINDOW):
        r0 = j * WINDOW
        q = q_ref[r0:r0 + WINDOW, :]
        parts = []
        for half in (half0, half1):
            for c in range(n_col):
                parts.append(jnp.where(half, q[:, c * LANES:(c + 1) * LANES], jnp.zeros((), BF16)))
        qm = jnp.concatenate(parts, axis=0)
        kb = kall[r0:r0 + 2 * WINDOW, :]
        vb = vall[r0:r0 + 2 * WINDOW, :]
        s = _dot_nt(qm, kb)
        s = s + (bias0_ref[0] if j == 0 else bias_ref[...])
        m = jnp.maximum(jnp.max(s, axis=1, keepdims=True), sink)
        p = jnp.exp(s - m)
        den = jnp.sum(p, axis=1, keepdims=True) + jnp.exp(sink - m)
        o = _dot(p.astype(BF16), vb) / den
        for c in range(n_col):
            lo = o[c * WINDOW:(c + 1) * WINDOW]
            hi = o[(n_col + c) * WINDOW:(n_col + c + 1) * WINDOW]
            o_ref[r0:r0 + WINDOW, c * LANES:(c + 1) * LANES] = jnp.where(half0, lo, hi)


def _swa_bias():
    qi = np.arange(SWA_HEADS * WINDOW)[:, None] % WINDOW
    kj = np.arange(2 * WINDOW)[None, :]
    dist = qi + WINDOW - kj
    band = (dist >= 0) & (dist < WINDOW)
    bias = np.where(band, 0.0, NEG_BIG).astype(np.float32)
    first = np.where(band & (kj >= WINDOW), 0.0, NEG_BIG).astype(np.float32)
    return jnp.asarray(np.stack([bias, first])), jnp.asarray(bias)


def _swa(qa, ka, va, sink_col, batch, seq):
    n = qa.shape[0]
    tq = TQ_SWA
    nt = seq // tq
    blocks_per_tile = tq // WINDOW
    bias_pair, bias = _swa_bias()

    def cur(b, i):
        return (b * nt + i, 0)

    def prev(b, i):
        return (jnp.maximum((b * nt + i) * blocks_per_tile - 1, 0), 0)

    return pl.pallas_call(
        _swa_kernel,
        out_shape=jax.ShapeDtypeStruct((n, SWA_WIDTH), F32),
        grid=(batch, nt),
        in_specs=[
            pl.BlockSpec((tq, SWA_WIDTH), cur),
            pl.BlockSpec((tq, SWA_KV_WIDTH), cur),
            pl.BlockSpec((WINDOW, SWA_KV_WIDTH), prev),
            pl.BlockSpec((tq, SWA_KV_WIDTH), cur),
            pl.BlockSpec((WINDOW, SWA_KV_WIDTH), prev),
            pl.BlockSpec((1,) + bias.shape, lambda b, i: (jnp.where(i == 0, 1, 0), 0, 0)),
            pl.BlockSpec(bias.shape, lambda b, i: (0, 0)),
            pl.BlockSpec(sink_col.shape, lambda b, i: (0, 0)),
        ],
        out_specs=pl.BlockSpec((tq, SWA_WIDTH), cur),
        scratch_shapes=[pltpu.VMEM((tq + WINDOW, SWA_KV_WIDTH), BF16),
                        pltpu.VMEM((tq + WINDOW, SWA_KV_WIDTH), BF16)],
        compiler_params=_cparams(("arbitrary", "arbitrary")),
        name="swa",
    )(qa, ka, ka, va, va, bias_pair, bias, sink_col)


def _sb_kernel(q_ref, k_ref, v_ref, uu_ref, o_ref, qm, acc, run):
    i = pl.program_id(1)
    blk = SB_BLOCK
    n_sub = SB_TILE_BLOCKS
    tile = n_sub * blk
    pairs = q_ref.shape[1] // LANES
    lane = lax.broadcasted_iota(I32, (1, LANES), 1)
    half0 = lane < HEAD_DIM
    half1 = lane >= HEAD_DIM
    zero = jnp.zeros((), BF16)
    for p in range(pairs):
        q = q_ref[:, p * LANES:(p + 1) * LANES]
        qm[p] = jnp.concatenate([jnp.where(half0, q, zero), jnp.where(half1, q, zero)], axis=0)
    acc[...] = jnp.zeros_like(acc)
    run[...] = jnp.zeros_like(run)
    q_pos = i * blk + (lax.broadcasted_iota(I32, (2 * blk, blk), 0) & (blk - 1))
    col = lax.broadcasted_iota(I32, (2 * blk, blk), 1)

    def body(carry):
        end, _ = carry
        start = pl.multiple_of(jnp.maximum(end - tile, 0), blk)
        limit = jnp.minimum(end, q_pos) - start
        least = None
        for p in range(pairs):
            kt = k_ref[pl.ds(start, tile), p * LANES:(p + 1) * LANES]
            vt = v_ref[pl.ds(start, tile), p * LANES:(p + 1) * LANES]
            z = _dot_nt(qm[p], kt)
            mass_seen = run[p]
            ws = [None] * n_sub
            for b in reversed(range(n_sub)):
                zb = jnp.where(col < limit - b * blk, z[:, b * blk:(b + 1) * blk], NEG_BIG)
                mass = jnp.maximum(zb, 0.0) + jnp.log(1.0 + jnp.exp(-jnp.abs(zb)))
                mass_hi = mass.astype(BF16)
                mass_lo = (mass - mass_hi.astype(F32)).astype(BF16)
                cc = _dot(jnp.concatenate([mass_hi, mass_lo], axis=1), uu_ref[...])
                ws[b] = jnp.exp(zb - (cc[:, :blk] + mass_seen)).astype(BF16)
                mass_seen = mass_seen + cc[:, blk:]
            acc[p] += _dot(jnp.concatenate(ws, axis=1), vt)
            run[p] = mass_seen
            least = mass_seen if least is None else jnp.minimum(least, mass_seen)
        return start, jnp.min(least) > SB_UNDERFLOW

    lax.while_loop(lambda c: jnp.logical_and(c[0] > 0, jnp.logical_not(c[1])),
                   body, ((i + 1) * blk, jnp.bool_(False)))
    for p in range(pairs):
        a = acc[p]
        o_ref[:, p * LANES:(p + 1) * LANES] = jnp.where(half0, a[:blk], a[blk:])


def _sb_suffix_matrix():
    j = np.arange(2 * SB_BLOCK)[:, None] % SB_BLOCK
    s = np.arange(2 * SB_BLOCK)[None, :]
    m = np.where(s < SB_BLOCK, (j >= s), True)
    return jnp.asarray(m.astype(np.float32)).astype(BF16)


def _sb(qs, ks, vs, batch, seq):
    n = qs.shape[0]
    blk = SB_BLOCK
    nq = seq // blk
    pairs = SB_WIDTH // LANES
    uu = _sb_suffix_matrix()
    return pl.pallas_call(
        _sb_kernel,
        out_shape=jax.ShapeDtypeStruct((n, SB_WIDTH), F32),
        grid=(batch, nq),
        in_specs=[
            pl.BlockSpec((blk, SB_WIDTH), lambda b, i: (b * nq + i, 0)),
            pl.BlockSpec((seq, SB_WIDTH), lambda b, i: (b, 0)),
            pl.BlockSpec((seq, SB_WIDTH), lambda b, i: (b, 0)),
            pl.BlockSpec(uu.shape, lambda b, i: (0, 0)),
        ],
        out_specs=pl.BlockSpec((blk, SB_WIDTH), lambda b, i: (b * nq + i, 0)),
        scratch_shapes=[pltpu.VMEM((pairs, 2 * blk, LANES), BF16),
                        pltpu.VMEM((pairs, 2 * blk, LANES), F32),
                        pltpu.VMEM((pairs, 2 * blk, LANES), F32)],
        compiler_params=_cparams(("arbitrary", "arbitrary")),
        name="sb",
    )(qs, ks, vs, uu)


def _post_kernel(oa_ref, ob_ref, x_ref, ada_ref, ga_ref, gb_ref, wo_ref, g1_ref, b1_ref,
                 wrh_ref, wrl_ref, x1_ref, u2s_ref, lg_ref):
    def rms(o, g):
        return o * lax.rsqrt(jnp.mean(o * o, axis=-1, keepdims=True) + LN_EPS) * g

    na = rms(oa_ref[...], ga_ref[...]).astype(BF16)
    nb = rms(ob_ref[...], gb_ref[...]).astype(BF16)
    y = _dot(na, wo_ref[0:SWA_WIDTH, :]) + _dot(nb, wo_ref[SWA_WIDTH:, :])
    gate1 = ada_ref[0, 2:3, :]
    shift2 = ada_ref[0, 3:4, :]
    scale2 = ada_ref[0, 4:5, :]
    x1 = _layer_norm(DEEPNORM_ALPHA * x_ref[...] + (1.0 + gate1) * y) * g1_ref[...] + b1_ref[...]
    x1_ref[...] = x1
    u2 = _layer_norm(x1) * (1.0 + scale2) + shift2
    _store_slab_rows(u2s_ref, u2)
    u_hi = u2.astype(BF16)
    u_lo = (u2 - u_hi.astype(F32)).astype(BF16)
    wrh = wrh_ref[...]
    lg_ref[...] = _dot_nt(wrh, u_hi) + (_dot_nt(wrh, u_lo) + _dot_nt(wrl_ref[...], u_hi))


def _post(oa, ob, x2, ada, g_a, g_b, w_out_b, g1, b1, wr_hi, wr_lo, seq):
    n, d = x2.shape
    tm = TM_PROJ
    steps_per_seq = seq // tm
    e = wr_hi.shape[0]
    slabs = d // LANES
    const2 = lambda i: (0, 0)
    return pl.pallas_call(
        _post_kernel,
        out_shape=[jax.ShapeDtypeStruct((n, d), F32),
                   jax.ShapeDtypeStruct((n * slabs, LANES), F32),
                   jax.ShapeDtypeStruct((e, n), F32)],
        grid=(n // tm,),
        in_specs=[
            pl.BlockSpec((tm, SWA_WIDTH), lambda i: (i, 0)),
            pl.BlockSpec((tm, SB_WIDTH), lambda i: (i, 0)),
            pl.BlockSpec((tm, d), lambda i: (i, 0)),
            pl.BlockSpec((1,) + ada.shape[1:], lambda i: (i // steps_per_seq, 0, 0)),
            pl.BlockSpec(g_a.shape, const2),
            pl.BlockSpec(g_b.shape, const2),
            pl.BlockSpec(w_out_b.shape, const2),
            pl.BlockSpec(g1.shape, const2),
            pl.BlockSpec(b1.shape, const2),
            pl.BlockSpec(wr_hi.shape, const2),
            pl.BlockSpec(wr_lo.shape, const2),
        ],
        out_specs=[pl.BlockSpec((tm, d), lambda i: (i, 0)),
                   pl.BlockSpec((tm * slabs, LANES), lambda i: (i, 0)),
                   pl.BlockSpec((e, tm), lambda i: (0, i))],
        compiler_params=_cparams(("arbitrary",)),
        name="post",
    )(oa, ob, x2, ada, g_a, g_b, w_out_b, g1, b1, wr_hi, wr_lo)


def _route_kernel(lg_ref, bias_ref, su_ref, sl_ref, lpos_ref, gate_ref, seglo_ref, cnt_ref, seenb_ref,
                  seen_row):
    i = pl.program_id(0)

    @pl.when(i == 0)
    def _():
        seen_row[...] = jnp.zeros_like(seen_row)

    tt = lg_ref.shape[1]
    ninf = -jnp.inf
    scores = jax.nn.sigmoid(lg_ref[...])
    biased = scores + bias_ref[...]

    iog32 = lax.broadcasted_iota(I32, (GROUP_SIZE, tt), 0)
    groups = [biased[g * GROUP_SIZE:(g + 1) * GROUP_SIZE, :] for g in range(N_GROUPS)]
    gs_rows = []
    for blk in groups:
        m1 = jnp.max(blk, axis=0, keepdims=True)
        i1 = jnp.min(jnp.where(blk == m1, iog32, GROUP_SIZE), axis=0, keepdims=True)
        m2 = jnp.max(jnp.where(iog32 == i1, ninf, blk), axis=0, keepdims=True)
        gs_rows.append(m1 + m2)
    gs = jnp.concatenate(gs_rows, axis=0)

    iog = lax.broadcasted_iota(I32, gs.shape, 0)
    gsel = jnp.zeros(gs.shape, F32)
    cur = gs
    for _ in range(TOPK_GROUPS):
        m = jnp.max(cur, axis=0, keepdims=True)
        idx = jnp.min(jnp.where(cur == m, iog, N_GROUPS), axis=0, keepdims=True)
        hit = iog == idx
        gsel = jnp.where(hit, 1.0, gsel)
        cur = jnp.where(hit, ninf, cur)

    cand = jnp.concatenate(
        [jnp.where(gsel[g:g + 1, :] > 0.5, groups[g], ninf) for g in range(N_GROUPS)], axis=0)
    ioe = lax.broadcasted_iota(I32, cand.shape, 0)
    chosen = jnp.zeros(cand.shape, F32)
    idxs, gates = [], []
    for _ in range(TOP_K):
        m = jnp.max(cand, axis=0, keepdims=True)
        idx = jnp.min(jnp.where(cand == m, ioe, N_EXPERTS), axis=0, keepdims=True)
        hit = ioe == idx
        gates.append(jnp.sum(jnp.where(hit, scores, 0.0), axis=0, keepdims=True))
        cand = jnp.where(hit, ninf, cand)
        chosen = jnp.where(hit, 1.0, chosen)
        idxs.append(idx)
    gsum = gates[0]
    for g in gates[1:]:
        gsum = gsum + g
    gates = [g / gsum * ROUTED_SCALE for g in gates]

    def hi_lo(x):
        hi = x.astype(BF16)
        return hi, (x - hi.astype(F32)).astype(BF16)

    chosen_b = chosen.astype(BF16)
    rk = _dot(chosen_b, su_ref[...])
    c_hi, c_lo = hi_lo(rk[:, tt:])
    first_col = _dot(sl_ref[...], c_hi) + _dot(sl_ref[...], c_lo)
    slot_mat = rk[:, :tt] + jnp.concatenate([first_col] * (tt // LANES), axis=1)
    slots = [jnp.sum(jnp.where(ioe == idx, slot_mat, 0.0), axis=0, keepdims=True) for idx in idxs]
    lpos_ref[...] = jnp.concatenate(slots, axis=0).astype(I32)
    gate_ref[...] = jnp.concatenate(gates, axis=0)
    cnt_row = _dot_nt(jnp.ones((8, tt), BF16), chosen_b)
    r_hi, r_lo = hi_lo(cnt_row)
    before_row = seen_row[...]
    seen_row[...] = before_row + cnt_row
    seglo_ref[...] = (_dot_nt(r_hi, sl_ref[...]) + _dot_nt(r_lo, sl_ref[...])).astype(I32)
    cnt_ref[...] = cnt_row.astype(I32)
    seenb_ref[...] = before_row.astype(I32)


def _route_prefix_matrix(tt):
    a = np.arange(tt)[:, None] < np.arange(tt)[None, :]
    m = np.concatenate([a, np.ones((tt, LANES), bool)], axis=1)
    return jnp.asarray(m.astype(np.float32)).astype(BF16)


def _route(logits_t, e_bias):
    e, n = logits_t.shape
    tt = T_MOE
    bias_b = jnp.broadcast_to(e_bias.astype(F32)[:, None], (e, tt))
    su = _route_prefix_matrix(tt)
    sl = jnp.asarray((np.arange(e)[None, :] < np.arange(e)[:, None]).astype(np.float32)).astype(BF16)
    tok = lambda i: (0, i)
    per_tile = jax.ShapeDtypeStruct((n // tt * 8, e), I32)
    per_tile_spec = pl.BlockSpec((8, e), lambda i: (i, 0))
    return pl.pallas_call(
        _route_kernel,
        out_shape=[jax.ShapeDtypeStruct((TOP_K, n), I32),
                   jax.ShapeDtypeStruct((TOP_K, n), F32),
                   per_tile, per_tile, per_tile],
        grid=(n // tt,),
        in_specs=[pl.BlockSpec((e, tt), tok),
                  pl.BlockSpec((e, tt), lambda i: (0, 0)),
                  pl.BlockSpec(su.shape, lambda i: (0, 0)),
                  pl.BlockSpec(sl.shape, lambda i: (0, 0))],
        out_specs=[pl.BlockSpec((TOP_K, tt), tok), pl.BlockSpec((TOP_K, tt), tok),
                   per_tile_spec, per_tile_spec, per_tile_spec],
        scratch_shapes=[pltpu.VMEM((8, e), F32)],
        compiler_params=_cparams(("arbitrary",)),
        name="route",
    )(logits_t, bias_b, su, sl)


def _slab(ref, row, slabs):
    return ref.at[pl.ds(pl.multiple_of(row * slabs, slabs), slabs)]


def _start_segment(e, seglo_ref, cnt_ref, seenb_ref, pstart_ref, make_copy):
    cnt = cnt_ref[0, e]
    local = seglo_ref[0, e]
    glob = pstart_ref[e] + seenb_ref[0, e]
    n_chunks = cnt // SEGMENT_CHUNK

    def chunk(j, c):
        make_copy(local + j * SEGMENT_CHUNK, glob + j * SEGMENT_CHUNK, SEGMENT_CHUNK).start()
        return c

    lax.fori_loop(0, n_chunks, chunk, 0)
    done = n_chunks * SEGMENT_CHUNK
    rows = SEGMENT_CHUNK // 2
    while rows >= 1:
        take = (cnt & rows) != 0

        @pl.when(take)
        def _(done=done, rows=rows):
            make_copy(local + done, glob + done, rows).start()

        done = done + jnp.where(take, rows, 0)
        rows //= 2


def _dispatch_kernel(pstart_ref, pvalid_ref, pend_ref, nused_ref, lpos_ref, seglo_ref, cnt_ref, seenb_ref,
                     u_ref, xs_ref, obuf, zbuf, sems, sem, *, slabs):
    i = pl.program_id(0)
    last = pl.num_programs(0) - 1
    td = u_ref.shape[0] // slabs
    zrows = zbuf.shape[0] // slabs
    chunks_per_block = EXPERT_BLOCK // zrows
    n_chunks = xs_ref.shape[0] // slabs // zrows

    @pl.when(i == 0)
    def _():
        zbuf[...] = jnp.zeros_like(zbuf)

        def zero_chunk(row0):
            return pltpu.make_async_copy(
                zbuf, xs_ref.at[pl.ds(pl.multiple_of(row0 * slabs, zrows * slabs), zrows * slabs)], sem)

        def start_unused(b, c):
            zero_chunk(b * zrows).start()
            return c

        def wait_unused(b, c):
            zero_chunk(b * zrows).wait()
            return c

        first_unused = nused_ref[0] * chunks_per_block
        lax.fori_loop(first_unused, n_chunks, start_unused, 0)
        lax.fori_loop(first_unused, n_chunks, wait_unused, 0)

        def tails(e, c, *, wait):
            for g in range(chunks_per_block):
                row0 = pend_ref[e] - zrows * (g + 1)

                @pl.when(row0 + zrows > pvalid_ref[e])
                def _():
                    if wait:
                        zero_chunk(row0).wait()
                    else:
                        zero_chunk(row0).start()
            return c

        lax.fori_loop(0, N_EXPERTS, functools.partial(tails, wait=False), 0)
        lax.fori_loop(0, N_EXPERTS, functools.partial(tails, wait=True), 0)

    slot = i % 2
    buf = obuf.at[slot]
    out_sem = sems.at[slot]

    def drained(b, s):
        return pltpu.make_async_copy(obuf.at[b], xs_ref.at[pl.ds(0, obuf.shape[1])], sems.at[s])

    @pl.when(i >= 2)
    def _():
        drained(slot, slot).wait()

    def place(t, c):
        row = _slab(u_ref, t, slabs)[...]
        for k in range(TOP_K):
            _slab(buf, lpos_ref[k, t], slabs)[...] = row
        return c

    lax.fori_loop(0, td, place, 0)

    def to_table(local, glob, rows):
        return pltpu.make_async_copy(
            buf.at[pl.ds(pl.multiple_of(local * slabs, slabs), rows * slabs)],
            xs_ref.at[pl.ds(pl.multiple_of(glob * slabs, slabs), rows * slabs)], out_sem)

    def send(e, c):
        _start_segment(e, seglo_ref, cnt_ref, seenb_ref, pstart_ref, to_table)
        return c

    lax.fori_loop(0, N_EXPERTS, send, 0)

    @pl.when(i == last)
    def _():
        drained(slot, slot).wait()

        @pl.when(last >= 1)
        def _():
            drained(1 - slot, 1 - slot).wait()


def _dispatch(pstart, pvalid, pend, nused, lpos_t, seglo, cnt, seen_b, u2s, n_rows, slabs):
    n = u2s.shape[0] // slabs
    td = T_MOE
    smem_tok = pl.BlockSpec((TOP_K, td), lambda i, *_: (0, i), memory_space=pltpu.SMEM)
    smem_tile = pl.BlockSpec((8, N_EXPERTS), lambda i, *_: (i, 0), memory_space=pltpu.SMEM)
    return pl.pallas_call(
        functools.partial(_dispatch_kernel, slabs=slabs),
        out_shape=jax.ShapeDtypeStruct((n_rows * slabs, LANES), F32),
        grid_spec=pltpu.PrefetchScalarGridSpec(
            num_scalar_prefetch=4,
            grid=(n // td,),
            in_specs=[smem_tok, smem_tile, smem_tile, smem_tile,
                      pl.BlockSpec((td * slabs, LANES), lambda i, *_: (i, 0))],
            out_specs=pl.BlockSpec(memory_space=pl.ANY),
            scratch_shapes=[pltpu.VMEM((2, TOP_K * td * slabs, LANES), F32),
                            pltpu.VMEM((ZERO_ROWS * slabs, LANES), F32),
                            pltpu.SemaphoreType.DMA((2,)),
                            pltpu.SemaphoreType.DMA],
        ),
        compiler_params=_cparams(("arbitrary",)),
        name="dispatch",
    )(pstart, pvalid, pend, nused, lpos_t, seglo, cnt, seen_b, u2s)


def _expert_kernel(blk_e_ref, nused_ref, x_ref, w1_ref, w3_ref, w2_ref, y_ref, w1b, w3b, w2b, *, slabs):
    i = pl.program_id(0)

    @pl.when(i < nused_ref[0])
    def _():
        e = blk_e_ref[i]
        prev = blk_e_ref[jnp.maximum(i - 1, 0)]

        @pl.when(jnp.logical_or(i == 0, e != prev))
        def _():
            w1b[...] = w1_ref[0].astype(BF16)
            w3b[...] = w3_ref[0].astype(BF16)
            w2b[...] = w2_ref[0].astype(BF16)

        x = jnp.concatenate([c.astype(BF16) for c in _load_slab_rows(x_ref, slabs)], axis=1)
        h1 = _dot(x, w1b[...])
        h3 = _dot(x, w3b[...])
        a = (_silu(h1) * h3).astype(BF16)
        _store_slab_rows(y_ref, _dot(a, w2b[...]))

    @pl.when(i >= nused_ref[0])
    def _():
        y_ref[...] = jnp.zeros_like(y_ref)


def _experts(blk_e, nused, xs, w1, w3, w2, slabs):
    n_rows = xs.shape[0] // slabs
    e, d, f = w1.shape
    nb = n_rows // EXPERT_BLOCK

    def row_blk(i, blk_e_ref, nused_ref):
        return (jnp.minimum(i, nused_ref[0] - 1), 0)

    def w_blk(i, blk_e_ref, nused_ref):
        return (blk_e_ref[jnp.minimum(i, nused_ref[0] - 1)], 0, 0)

    return pl.pallas_call(
        functools.partial(_expert_kernel, slabs=slabs),
        out_shape=jax.ShapeDtypeStruct((n_rows * slabs, LANES), F32),
        grid_spec=pltpu.PrefetchScalarGridSpec(
            num_scalar_prefetch=2,
            grid=(nb,),
            in_specs=[pl.BlockSpec((EXPERT_BLOCK * slabs, LANES), row_blk),
                      pl.BlockSpec((1, d, f), w_blk),
                      pl.BlockSpec((1, d, f), w_blk),
                      pl.BlockSpec((1, f, d), w_blk)],
            out_specs=pl.BlockSpec((EXPERT_BLOCK * slabs, LANES), lambda i, *_: (i, 0)),
            scratch_shapes=[pltpu.VMEM((d, f), BF16), pltpu.VMEM((d, f), BF16),
                            pltpu.VMEM((f, d), BF16)],
        ),
        compiler_params=_cparams(("arbitrary",)),
        name="experts",
    )(blk_e, nused, xs, w1, w3, w2)


def _final_kernel(pstart_ref, lpos_ref, gate_ref, seglo_ref, cnt_ref, seenb_ref, seglo_nx, cnt_nx, seenb_nx,
                  u_ref, x1_ref, ada_ref, ws1_ref, ws3_ref, ws2_ref, g2_ref, b2_ref, ys_ref, o_ref,
                  lbuf, rbuf, sem, *, slabs):
    i = pl.program_id(0)
    last = pl.num_programs(0) - 1
    tf = u_ref.shape[0] // slabs

    def from_table(local, glob, rows):
        return pltpu.make_async_copy(
            ys_ref.at[pl.ds(pl.multiple_of(glob * slabs, slabs), rows * slabs)],
            lbuf.at[pl.ds(pl.multiple_of(local * slabs, slabs), rows * slabs)], sem)

    def fetch_tile(seglo_r, cnt_r, seenb_r):
        def fetch(e, c):
            _start_segment(e, seglo_r, cnt_r, seenb_r, pstart_ref, from_table)
            return c

        lax.fori_loop(0, N_EXPERTS, fetch, 0)

    @pl.when(i == 0)
    def _():
        fetch_tile(seglo_ref, cnt_ref, seenb_ref)

    pltpu.make_async_copy(ys_ref.at[pl.ds(0, lbuf.shape[0])], lbuf, sem).wait()

    def combine(t, c):
        acc = None
        for k in range(TOP_K):
            row = _slab(lbuf, lpos_ref[k, t], slabs)[...] * gate_ref[k, t]
            acc = row if acc is None else acc + row
        _slab(rbuf, t, slabs)[...] = acc
        return c

    lax.fori_loop(0, tf, combine, 0)

    @pl.when(i < last)
    def _():
        fetch_tile(seglo_nx, cnt_nx, seenb_nx)

    u = jnp.concatenate([c.astype(BF16) for c in _load_slab_rows(u_ref, slabs)], axis=1)
    shared = _dot((_silu(_dot(u, ws1_ref[...])) * _dot(u, ws3_ref[...])).astype(BF16), ws2_ref[...])
    routed = jnp.concatenate(_load_slab_rows(rbuf, slabs), axis=1)
    gate2 = ada_ref[0, 5:6, :]
    y = shared + routed
    o_ref[...] = _layer_norm(DEEPNORM_ALPHA * x1_ref[...] + (1.0 + gate2) * y) * g2_ref[...] + b2_ref[...]


def _final(pstart, lpos_t, gate_t, seglo, cnt, seen_b, u2s, x1, ada, ws1b, ws3b, ws2b, g2, b2, ys,
           seq, slabs):
    n, d = x1.shape
    tf = T_MOE
    n_tiles = n // tf
    steps_per_seq = seq // tf
    smem_tok = pl.BlockSpec((TOP_K, tf), lambda i, *_: (0, i), memory_space=pltpu.SMEM)
    smem_tile = pl.BlockSpec((8, N_EXPERTS), lambda i, *_: (i, 0), memory_space=pltpu.SMEM)
    smem_next = pl.BlockSpec((8, N_EXPERTS), lambda i, *_: (jnp.minimum(i + 1, n_tiles - 1), 0),
                             memory_space=pltpu.SMEM)
    const2 = lambda i, *_: (0, 0)
    return pl.pallas_call(
        functools.partial(_final_kernel, slabs=slabs),
        out_shape=jax.ShapeDtypeStruct((n, d), F32),
        grid_spec=pltpu.PrefetchScalarGridSpec(
            num_scalar_prefetch=1,
            grid=(n_tiles,),
            in_specs=[smem_tok, smem_tok, smem_tile, smem_tile, smem_tile, smem_next, smem_next, smem_next,
                      pl.BlockSpec((tf * slabs, LANES), lambda i, *_: (i, 0)),
                      pl.BlockSpec((tf, d), lambda i, *_: (i, 0)),
                      pl.BlockSpec((1,) + ada.shape[1:], lambda i, *_: (i // steps_per_seq, 0, 0)),
                      pl.BlockSpec(ws1b.shape, const2),
                      pl.BlockSpec(ws3b.shape, const2),
                      pl.BlockSpec(ws2b.shape, const2),
                      pl.BlockSpec(g2.shape, const2),
                      pl.BlockSpec(b2.shape, const2),
                      pl.BlockSpec(memory_space=pl.ANY)],
            out_specs=pl.BlockSpec((tf, d), lambda i, *_: (i, 0)),
            scratch_shapes=[pltpu.VMEM((TOP_K * tf * slabs, LANES), F32),
                            pltpu.VMEM((tf * slabs, LANES), F32),
                            pltpu.SemaphoreType.DMA],
        ),
        compiler_params=_cparams(("arbitrary",)),
        name="final",
    )(pstart, lpos_t, gate_t, seglo, cnt, seen_b, seglo, cnt, seen_b, u2s, x1, ada, ws1b, ws3b, ws2b,
      g2, b2, ys)


def _swa_head_permutation():
    per_group = SWA_HEADS // SWA_KV_HEADS
    cols = []
    for c in range(per_group):
        for g in range(SWA_KV_HEADS):
            h = c + per_group * g
            cols.extend(range(h * HEAD_DIM, (h + 1) * HEAD_DIM))
    return np.asarray(cols)


def _layer(x, c, positions, w_ada, b_ada, w_in, sinks, g_swa, g_sb, w_out, ln1_g, ln1_b,
           w_router, e_bias, w1, w3, w2, ws1, ws3, ws2, ln2_g, ln2_b):
    batch, seq, d = x.shape
    n = batch * seq
    x2 = x.reshape(n, d)
    perm = _swa_head_permutation()
    per_group = SWA_HEADS // SWA_KV_HEADS

    ada = _ada(c, w_ada, b_ada)

    o_q = SWA_WIDTH + 2 * SWA_KV_WIDTH
    w_in_b = jnp.concatenate(
        [w_in[:, :SWA_WIDTH][:, perm], w_in[:, SWA_WIDTH:o_q],
         w_in[:, o_q:o_q + SB_WIDTH] * (HEAD_DIM ** -0.5), w_in[:, o_q + SB_WIDTH:]], axis=1).astype(BF16)
    half = HEAD_DIM // 2
    inv_freq = ROPE_THETA ** (-jnp.arange(half, dtype=F32) * 2.0 / HEAD_DIM)
    invf = jnp.tile(inv_freq, LANES // half).reshape(1, LANES)
    pos_col = positions.reshape(n, 1).astype(F32)
    qa, ka, va, qs, ks, vs = _inproj(x2, ada, pos_col, invf, w_in_b, seq)

    head_of_block = np.asarray([c_ + per_group * g for g in range(SWA_KV_HEADS) for c_ in range(per_group)])
    sink_col = jnp.repeat(sinks.astype(F32)[head_of_block], WINDOW).reshape(SWA_HEADS * WINDOW, 1)
    oa = _swa(qa, ka, va, sink_col, batch, seq)
    ob = _sb(qs, ks, vs, batch, seq)

    w_out_b = jnp.concatenate([w_out[:SWA_WIDTH][perm], w_out[SWA_WIDTH:]], axis=0).astype(BF16)
    wr_t = w_router.T.astype(F32)
    wr_hi = wr_t.astype(BF16)
    wr_lo = (wr_t - wr_hi.astype(F32)).astype(BF16)
    x1, u2s, logits_t = _post(oa, ob, x2, ada, g_swa[perm].reshape(1, -1), g_sb.reshape(1, -1), w_out_b,
                              ln1_g.reshape(1, d), ln1_b.reshape(1, d), wr_hi, wr_lo, seq)

    lpos_t, gate_t, seglo, cnt, seen_b = _route(logits_t, e_bias)

    counts = seen_b[-1] + cnt[-1]
    padded = (counts + EXPERT_BLOCK - 1) // EXPERT_BLOCK * EXPERT_BLOCK
    pend = jnp.cumsum(padded).astype(I32)
    pstart = pend - padded
    n_rows = n * TOP_K + N_EXPERTS * EXPERT_BLOCK
    n_blocks = n_rows // EXPERT_BLOCK
    block_row0 = jnp.arange(n_blocks, dtype=I32) * EXPERT_BLOCK
    blk_e = jnp.minimum(jnp.sum((pend[None, :] <= block_row0[:, None]).astype(I32), axis=1), N_EXPERTS - 1)
    nused = (pend[-1:] // EXPERT_BLOCK).astype(I32)
    slabs = d // LANES

    xs = _dispatch(pstart, pstart + counts, pend, nused, lpos_t, seglo, cnt, seen_b, u2s, n_rows, slabs)
    ys = _experts(blk_e, nused, xs, w1, w3, w2, slabs)
    out = _final(pstart, lpos_t, gate_t, seglo, cnt, seen_b, u2s, x1, ada, ws1.astype(BF16),
                 ws3.astype(BF16), ws2.astype(BF16), ln2_g.reshape(1, d), ln2_b.reshape(1, d), ys, seq, slabs)
    return out.reshape(batch, seq, d)


def kernel(x, c, positions, w_ada, b_ada, w_in, attn_sinks, g_swa, g_sb, w_out, ln1_g, ln1_b,
           w_router, e_bias, w1, w3, w2, ws1, ws3, ws2, ln2_g, ln2_b):
    assert w_ada.shape[0] == DEPTH
    for l in range(DEPTH):
        x = _layer(x, c, positions, w_ada[l], b_ada[l], w_in[l], attn_sinks[l], g_swa[l], g_sb[l],
                   w_out[l], ln1_g[l], ln1_b[l], w_router[l], e_bias[l], w1[l], w3[l], w2[l],
                   ws1[l], ws3[l], ws2[l], ln2_g[l], ln2_b[l])
    return x
```

```python
import functools

import numpy as np
import jax
import jax.numpy as jnp
from jax import lax
from jax.experimental import pallas as pl
from jax.experimental.pallas import tpu as pltpu

F32 = jnp.float32
BF16 = jnp.bfloat16
I32 = jnp.int32

HEAD_DIM = 64
SWA_HEADS = 8
SWA_KV_HEADS = 2
SB_HEADS = 8
SWA_WIDTH = SWA_HEADS * HEAD_DIM
SWA_KV_WIDTH = SWA_KV_HEADS * HEAD_DIM
SB_WIDTH = SB_HEADS * HEAD_DIM
WINDOW = 128
ROPE_THETA = 10000.0
N_EXPERTS = 256
TOP_K = 8
N_GROUPS = 8
TOPK_GROUPS = 4
GROUP_SIZE = N_EXPERTS // N_GROUPS
ROUTED_SCALE = 2.5
LN_EPS = 1e-5
DEPTH = 1
DEEPNORM_ALPHA = (2 * DEPTH) ** 0.25

LANES = 128
SB_BLOCK = 128
SB_TILE_BLOCKS = 3
SB_UNDERFLOW = 110.0
NEG_BIG = -1e30
VMEM_LIMIT = 56 * 1024 * 1024

TM_PROJ = 512
TQ_SWA = 512
T_MOE = 512
EXPERT_BLOCK = 256
ZERO_ROWS = 128
SEGMENT_CHUNK = 8


def _cparams(sem, vmem=VMEM_LIMIT):
    return pltpu.CompilerParams(dimension_semantics=sem, vmem_limit_bytes=vmem)


def _layer_norm(x):
    mu = jnp.mean(x, axis=-1, keepdims=True)
    xc = x - mu
    var = jnp.mean(xc * xc, axis=-1, keepdims=True)
    return xc * lax.rsqrt(var + LN_EPS)


def _silu(x):
    return x * jax.nn.sigmoid(x)


def _dot(a, b):
    return jnp.dot(a, b, preferred_element_type=F32)


def _dot_nt(a, b):
    return lax.dot_general(a, b, (((1,), (1,)), ((), ())), preferred_element_type=F32)


def _store_slab_rows(ref, x):
    t, d = x.shape
    slabs = d // LANES
    for s in range(slabs):
        ref[pl.ds(s, t, stride=slabs), :] = x[:, s * LANES:(s + 1) * LANES]


def _load_slab_rows(ref, slabs):
    t = ref.shape[0] // slabs
    return [ref[pl.ds(s, t, stride=slabs), :] for s in range(slabs)]


def _ada_kernel(c_ref, w_ref, b_ref, o_ref):
    sc = _silu(c_ref[...])
    o_ref[...] = _dot(sc.astype(BF16), w_ref[...].astype(BF16)) + b_ref[...]


def _ada(c, w_ada, b_ada):
    b, d = c.shape
    n_out = w_ada.shape[1]
    rows = 8
    c_pad = jnp.zeros((rows, d), F32).at[:b].set(c)
    out = pl.pallas_call(
        _ada_kernel,
        out_shape=jax.ShapeDtypeStruct((rows, n_out), F32),
        grid=(n_out // d,),
        in_specs=[
            pl.BlockSpec((rows, d), lambda j: (0, 0)),
            pl.BlockSpec((d, d), lambda j: (0, j)),
            pl.BlockSpec((1, d), lambda j: (0, j)),
        ],
        out_specs=pl.BlockSpec((rows, d), lambda j: (0, j)),
        compiler_params=_cparams(("arbitrary",)),
        name="ada",
    )(c_pad, w_ada, b_ada.reshape(1, n_out))
    return out[:b].reshape(b, n_out // d, d)


def _inproj_kernel(x_ref, ada_ref, pos_ref, invf_ref, w_ref,
                   qa_ref, ka_ref, va_ref, qs_ref, ks_ref, vs_ref):
    x = x_ref[...]
    shift = ada_ref[0, 0:1, :]
    scale = ada_ref[0, 1:2, :]
    u = _layer_norm(x) * (1.0 + scale) + shift
    h = _dot(u.astype(BF16), w_ref[...])

    ang = pos_ref[...] * invf_ref[...]
    cs = jnp.cos(ang)
    sn = jnp.sin(ang)
    lane = lax.broadcasted_iota(I32, (1, LANES), 1)
    first = (lane & (HEAD_DIM // 2)) == 0
    sn_signed = jnp.where(first, -sn, sn)

    def rope(hc):
        partner = jnp.where(first, pltpu.roll(hc, LANES - HEAD_DIM // 2, 1),
                            pltpu.roll(hc, HEAD_DIM // 2, 1))
        return hc * cs + partner * sn_signed

    q_scale = HEAD_DIM ** -0.5
    o = 0
    for c in range(SWA_WIDTH // LANES):
        qa_ref[:, c * LANES:(c + 1) * LANES] = (rope(h[:, o:o + LANES]) * q_scale).astype(BF16)
        o += LANES
    ka_ref[...] = rope(h[:, o:o + SWA_KV_WIDTH]).astype(BF16)
    o += SWA_KV_WIDTH
    va_ref[...] = h[:, o:o + SWA_KV_WIDTH].astype(BF16)
    o += SWA_KV_WIDTH
    qs_ref[...] = h[:, o:o + SB_WIDTH].astype(BF16)
    o += SB_WIDTH
    ks_ref[...] = h[:, o:o + SB_WIDTH].astype(BF16)
    o += SB_WIDTH
    vs_ref[...] = h[:, o:o + SB_WIDTH].astype(BF16)


def _inproj(x2, ada, pos_col, invf, w_in_b, seq):
    n, d = x2.shape
    tm = TM_PROJ
    steps_per_seq = seq // tm
    widths = (SWA_WIDTH, SWA_KV_WIDTH, SWA_KV_WIDTH, SB_WIDTH, SB_WIDTH, SB_WIDTH)
    return pl.pallas_call(
        _inproj_kernel,
        out_shape=[jax.ShapeDtypeStruct((n, w), BF16) for w in widths],
        grid=(n // tm,),
        in_specs=[
            pl.BlockSpec((tm, d), lambda i: (i, 0)),
            pl.BlockSpec((1,) + ada.shape[1:], lambda i: (i // steps_per_seq, 0, 0)),
            pl.BlockSpec((tm, 1), lambda i: (i, 0)),
            pl.BlockSpec((1, LANES), lambda i: (0, 0)),
            pl.BlockSpec(w_in_b.shape, lambda i: (0, 0)),
        ],
        out_specs=[pl.BlockSpec((tm, w), lambda i: (i, 0)) for w in widths],
        compiler_params=_cparams(("arbitrary",)),
        name="inproj",
    )(x2, ada, pos_col, invf, w_in_b)


def _swa_kernel(q_ref, kc_ref, kp_ref, vc_ref, vp_ref, bias0_ref, bias_ref, sink_ref,
                o_ref, kall, vall):
    tq = q_ref.shape[0]
    kall[0:WINDOW, :] = kp_ref[...]
    kall[WINDOW:, :] = kc_ref[...]
    vall[0:WINDOW, :] = vp_ref[...]
    vall[WINDOW:, :] = vc_ref[...]
    lane = lax.broadcasted_iota(I32, (1, LANES), 1)
    half0 = lane < HEAD_DIM
    half1 = lane >= HEAD_DIM
    sink = sink_ref[...]
    n_col = SWA_WIDTH // LANES
    for j in range(tq // WINDOW):
        r0 = j * WINDOW
        q = q_ref[r0:r0 + WINDOW, :]
        parts = []
        for half in (half0, half1):
            for c in range(n_col):
                parts.append(jnp.where(half, q[:, c * LANES:(c + 1) * LANES], jnp.zeros((), BF16)))
        qm = jnp.concatenate(parts, axis=0)
        kb = kall[r0:r0 + 2 * WINDOW, :]
        vb = vall[r0:r0 + 2 * WINDOW, :]
        s = _dot_nt(qm, kb)
        s = s + (bias0_ref[0] if j == 0 else bias_ref[...])
        m = jnp.maximum(jnp.max(s, axis=1, keepdims=True), sink)
        p = jnp.exp(s - m)
        den = jnp.sum(p, axis=1, keepdims=True) + jnp.exp(sink - m)
        o = _dot(p.astype(BF16), vb) / den
        for c in range(n_col):
            lo = o[c * WINDOW:(c + 1) * WINDOW]
            hi = o[(n_col + c) * WINDOW:(n_col + c + 1) * WINDOW]
            o_ref[r0:r0 + WINDOW, c * LANES:(c + 1) * LANES] = jnp.where(half0, lo, hi)


def _swa_bias():
    qi = np.arange(SWA_HEADS * WINDOW)[:, None] % WINDOW
    kj = np.arange(2 * WINDOW)[None, :]
    dist = qi + WINDOW - kj
    band = (dist >= 0) & (dist < WINDOW)
    bias = np.where(band, 0.0, NEG_BIG).astype(np.float32)
    first = np.where(band & (kj >= WINDOW), 0.0, NEG_BIG).astype(np.float32)
    return jnp.asarray(np.stack([bias, first])), jnp.asarray(bias)


def _swa(qa, ka, va, sink_col, batch, seq):
    n = qa.shape[0]
    tq = TQ_SWA
    nt = seq // tq
    blocks_per_tile = tq // WINDOW
    bias_pair, bias = _swa_bias()

    def cur(b, i):
        return (b * nt + i, 0)

    def prev(b, i):
        return (jnp.maximum((b * nt + i) * blocks_per_tile - 1, 0), 0)

    return pl.pallas_call(
        _swa_kernel,
        out_shape=jax.ShapeDtypeStruct((n, SWA_WIDTH), F32),
        grid=(batch, nt),
        in_specs=[
            pl.BlockSpec((tq, SWA_WIDTH), cur),
            pl.BlockSpec((tq, SWA_KV_WIDTH), cur),
            pl.BlockSpec((WINDOW, SWA_KV_WIDTH), prev),
            pl.BlockSpec((tq, SWA_KV_WIDTH), cur),
            pl.BlockSpec((WINDOW, SWA_KV_WIDTH), prev),
            pl.BlockSpec((1,) + bias.shape, lambda b, i: (jnp.where(i == 0, 1, 0), 0, 0)),
            pl.BlockSpec(bias.shape, lambda b, i: (0, 0)),
            pl.BlockSpec(sink_col.shape, lambda b, i: (0, 0)),
        ],
        out_specs=pl.BlockSpec((tq, SWA_WIDTH), cur),
        scratch_shapes=[pltpu.VMEM((tq + WINDOW, SWA_KV_WIDTH), BF16),
                        pltpu.VMEM((tq + WINDOW, SWA_KV_WIDTH), BF16)],
        compiler_params=_cparams(("arbitrary", "arbitrary")),
        name="swa",
    )(qa, ka, ka, va, va, bias_pair, bias, sink_col)


def _sb_kernel(q_ref, k_ref, v_ref, uu_ref, o_ref, qm, acc, run):
    i = pl.program_id(1)
    blk = SB_BLOCK
    n_sub = SB_TILE_BLOCKS
    tile = n_sub * blk
    pairs = q_ref.shape[1] // LANES
    lane = lax.broadcasted_iota(I32, (1, LANES), 1)
    half0 = lane < HEAD_DIM
    half1 = lane >= HEAD_DIM
    zero = jnp.zeros((), BF16)
    for p in range(pairs):
        q = q_ref[:, p * LANES:(p + 1) * LANES]
        qm[p] = jnp.concatenate([jnp.where(half0, q, zero), jnp.where(half1, q, zero)], axis=0)
    acc[...] = jnp.zeros_like(acc)
    run[...] = jnp.zeros_like(run)
    q_pos = i * blk + (lax.broadcasted_iota(I32, (2 * blk, blk), 0) & (blk - 1))
    col = lax.broadcasted_iota(I32, (2 * blk, blk), 1)

    def body(carry):
        end, _ = carry
        start = pl.multiple_of(jnp.maximum(end - tile, 0), blk)
        limit = jnp.minimum(end, q_pos) - start
        least = None
        for p in range(pairs):
            kt = k_ref[pl.ds(start, tile), p * LANES:(p + 1) * LANES]
            vt = v_ref[pl.ds(start, tile), p * LANES:(p + 1) * LANES]
            z = _dot_nt(qm[p], kt)
            mass_seen = run[p]
            ws = [None] * n_sub
            for b in reversed(range(n_sub)):
                zb = jnp.where(col < limit - b * blk, z[:, b * blk:(b + 1) * blk], NEG_BIG)
                mass = jnp.maximum(zb, 0.0) + jnp.log(1.0 + jnp.exp(-jnp.abs(zb)))
                mass_hi = mass.astype(BF16)
                mass_lo = (mass - mass_hi.astype(F32)).astype(BF16)
                cc = _dot(jnp.concatenate([mass_hi, mass_lo], axis=1), uu_ref[...])
                ws[b] = jnp.exp(zb - (cc[:, :blk] + mass_seen)).astype(BF16)
                mass_seen = mass_seen + cc[:, blk:]
            acc[p] += _dot(jnp.concatenate(ws, axis=1), vt)
            run[p] = mass_seen
            least = mass_seen if least is None else jnp.minimum(least, mass_seen)
        return start, jnp.min(least) > SB_UNDERFLOW

    lax.while_loop(lambda c: jnp.logical_and(c[0] > 0, jnp.logical_not(c[1])),
                   body, ((i + 1) * blk, jnp.bool_(False)))
    for p in range(pairs):
        a = acc[p]
        o_ref[:, p * LANES:(p + 1) * LANES] = jnp.where(half0, a[:blk], a[blk:])


def _sb_suffix_matrix():
    j = np.arange(2 * SB_BLOCK)[:, None] % SB_BLOCK
    s = np.arange(2 * SB_BLOCK)[None, :]
    m = np.where(s < SB_BLOCK, (j >= s), True)
    return jnp.asarray(m.astype(np.float32)).astype(BF16)


def _sb(qs, ks, vs, batch, seq):
    n = qs.shape[0]
    blk = SB_BLOCK
    nq = seq // blk
    pairs = SB_WIDTH // LANES
    uu = _sb_suffix_matrix()
    return pl.pallas_call(
        _sb_kernel,
        out_shape=jax.ShapeDtypeStruct((n, SB_WIDTH), F32),
        grid=(batch, nq),
        in_specs=[
            pl.BlockSpec((blk, SB_WIDTH), lambda b, i: (b * nq + i, 0)),
            pl.BlockSpec((seq, SB_WIDTH), lambda b, i: (b, 0)),
            pl.BlockSpec((seq, SB_WIDTH), lambda b, i: (b, 0)),
            pl.BlockSpec(uu.shape, lambda b, i: (0, 0)),
        ],
        out_specs=pl.BlockSpec((blk, SB_WIDTH), lambda b, i: (b * nq + i, 0)),
        scratch_shapes=[pltpu.VMEM((pairs, 2 * blk, LANES), BF16),
                        pltpu.VMEM((pairs, 2 * blk, LANES), F32),
                        pltpu.VMEM((pairs, 2 * blk, LANES), F32)],
        compiler_params=_cparams(("arbitrary", "arbitrary")),
        name="sb",
    )(qs, ks, vs, uu)


def _post_kernel(oa_ref, ob_ref, x_ref, ada_ref, ga_ref, gb_ref, wo_ref, g1_ref, b1_ref,
                 wrh_ref, wrl_ref, x1_ref, u2s_ref, lg_ref):
    def rms(o, g):
        return o * lax.rsqrt(jnp.mean(o * o, axis=-1, keepdims=True) + LN_EPS) * g

    na = rms(oa_ref[...], ga_ref[...]).astype(BF16)
    nb = rms(ob_ref[...], gb_ref[...]).astype(BF16)
    y = _dot(na, wo_ref[0:SWA_WIDTH, :]) + _dot(nb, wo_ref[SWA_WIDTH:, :])
    gate1 = ada_ref[0, 2:3, :]
    shift2 = ada_ref[0, 3:4, :]
    scale2 = ada_ref[0, 4:5, :]
    x1 = _layer_norm(DEEPNORM_ALPHA * x_ref[...] + (1.0 + gate1) * y) * g1_ref[...] + b1_ref[...]
    x1_ref[...] = x1
    u2 = _layer_norm(x1) * (1.0 + scale2) + shift2
    _store_slab_rows(u2s_ref, u2)
    u_hi = u2.astype(BF16)
    u_lo = (u2 - u_hi.astype(F32)).astype(BF16)
    wrh = wrh_ref[...]
    lg_ref[...] = _dot_nt(wrh, u_hi) + (_dot_nt(wrh, u_lo) + _dot_nt(wrl_ref[...], u_hi))


def _post(oa, ob, x2, ada, g_a, g_b, w_out_b, g1, b1, wr_hi, wr_lo, seq):
    n, d = x2.shape
    tm = TM_PROJ
    steps_per_seq = seq // tm
    e = wr_hi.shape[0]
    slabs = d // LANES
    const2 = lambda i: (0, 0)
    return pl.pallas_call(
        _post_kernel,
        out_shape=[jax.ShapeDtypeStruct((n, d), F32),
                   jax.ShapeDtypeStruct((n * slabs, LANES), F32),
                   jax.ShapeDtypeStruct((e, n), F32)],
        grid=(n // tm,),
        in_specs=[
            pl.BlockSpec((tm, SWA_WIDTH), lambda i: (i, 0)),
            pl.BlockSpec((tm, SB_WIDTH), lambda i: (i, 0)),
            pl.BlockSpec((tm, d), lambda i: (i, 0)),
            pl.BlockSpec((1,) + ada.shape[1:], lambda i: (i // steps_per_seq, 0, 0)),
            pl.BlockSpec(g_a.shape, const2),
            pl.BlockSpec(g_b.shape, const2),
            pl.BlockSpec(w_out_b.shape, const2),
            pl.BlockSpec(g1.shape, const2),
            pl.BlockSpec(b1.shape, const2),
            pl.BlockSpec(wr_hi.shape, const2),
            pl.BlockSpec(wr_lo.shape, const2),
        ],
        out_specs=[pl.BlockSpec((tm, d), lambda i: (i, 0)),
                   pl.BlockSpec((tm * slabs, LANES), lambda i: (i, 0)),
                   pl.BlockSpec((e, tm), lambda i: (0, i))],
        compiler_params=_cparams(("arbitrary",)),
        name="post",
    )(oa, ob, x2, ada, g_a, g_b, w_out_b, g1, b1, wr_hi, wr_lo)


def _route_kernel(lg_ref, bias_ref, su_ref, sl_ref, lpos_ref, gate_ref, seglo_ref, cnt_ref, seenb_ref,
                  seen_row):
    i = pl.program_id(0)

    @pl.when(i == 0)
    def _():
        seen_row[...] = jnp.zeros_like(seen_row)

    tt = lg_ref.shape[1]
    ninf = -jnp.inf
    scores = jax.nn.sigmoid(lg_ref[...])
    biased = scores + bias_ref[...]

    iog32 = lax.broadcasted_iota(I32, (GROUP_SIZE, tt), 0)
    groups = [biased[g * GROUP_SIZE:(g + 1) * GROUP_SIZE, :] for g in range(N_GROUPS)]
    gs_rows = []
    for blk in groups:
        m1 = jnp.max(blk, axis=0, keepdims=True)
        i1 = jnp.min(jnp.where(blk == m1, iog32, GROUP_SIZE), axis=0, keepdims=True)
        m2 = jnp.max(jnp.where(iog32 == i1, ninf, blk), axis=0, keepdims=True)
        gs_rows.append(m1 + m2)
    gs = jnp.concatenate(gs_rows, axis=0)

    iog = lax.broadcasted_iota(I32, gs.shape, 0)
    gsel = jnp.zeros(gs.shape, F32)
    cur = gs
    for _ in range(TOPK_GROUPS):
        m = jnp.max(cur, axis=0, keepdims=True)
        idx = jnp.min(jnp.where(cur == m, iog, N_GROUPS), axis=0, keepdims=True)
        hit = iog == idx
        gsel = jnp.where(hit, 1.0, gsel)
        cur = jnp.where(hit, ninf, cur)

    cand = jnp.concatenate(
        [jnp.where(gsel[g:g + 1, :] > 0.5, groups[g], ninf) for g in range(N_GROUPS)], axis=0)
    ioe = lax.broadcasted_iota(I32, cand.shape, 0)
    chosen = jnp.zeros(cand.shape, F32)
    idxs, gates = [], []
    for _ in range(TOP_K):
        m = jnp.max(cand, axis=0, keepdims=True)
        idx = jnp.min(jnp.where(cand == m, ioe, N_EXPERTS), axis=0, keepdims=True)
        hit = ioe == idx
        gates.append(jnp.sum(jnp.where(hit, scores, 0.0), axis=0, keepdims=True))
        cand = jnp.where(hit, ninf, cand)
        chosen = jnp.where(hit, 1.0, chosen)
        idxs.append(idx)
    gsum = gates[0]
    for g in gates[1:]:
        gsum = gsum + g
    gates = [g / gsum * ROUTED_SCALE for g in gates]

    def hi_lo(x):
        hi = x.astype(BF16)
        return hi, (x - hi.astype(F32)).astype(BF16)

    chosen_b = chosen.astype(BF16)
    rk = _dot(chosen_b, su_ref[...])
    c_hi, c_lo = hi_lo(rk[:, tt:])
    first_col = _dot(sl_ref[...], c_hi) + _dot(sl_ref[...], c_lo)
    slot_mat = rk[:, :tt] + jnp.concatenate([first_col] * (tt // LANES), axis=1)
    slots = [jnp.sum(jnp.where(ioe == idx, slot_mat, 0.0), axis=0, keepdims=True) for idx in idxs]
    lpos_ref[...] = jnp.concatenate(slots, axis=0).astype(I32)
    gate_ref[...] = jnp.concatenate(gates, axis=0)
    cnt_row = _dot_nt(jnp.ones((8, tt), BF16), chosen_b)
    r_hi, r_lo = hi_lo(cnt_row)
    before_row = seen_row[...]
    seen_row[...] = before_row + cnt_row
    seglo_ref[...] = (_dot_nt(r_hi, sl_ref[...]) + _dot_nt(r_lo, sl_ref[...])).astype(I32)
    cnt_ref[...] = cnt_row.astype(I32)
    seenb_ref[...] = before_row.astype(I32)


def _route_prefix_matrix(tt):
    a = np.arange(tt)[:, None] < np.arange(tt)[None, :]
    m = np.concatenate([a, np.ones((tt, LANES), bool)], axis=1)
    return jnp.asarray(m.astype(np.float32)).astype(BF16)


def _route(logits_t, e_bias):
    e, n = logits_t.shape
    tt = T_MOE
    bias_b = jnp.broadcast_to(e_bias.astype(F32)[:, None], (e, tt))
    su = _route_prefix_matrix(tt)
    sl = jnp.asarray((np.arange(e)[None, :] < np.arange(e)[:, None]).astype(np.float32)).astype(BF16)
    tok = lambda i: (0, i)
    per_tile = jax.ShapeDtypeStruct((n // tt * 8, e), I32)
    per_tile_spec = pl.BlockSpec((8, e), lambda i: (i, 0))
    return pl.pallas_call(
        _route_kernel,
        out_shape=[jax.ShapeDtypeStruct((TOP_K, n), I32),
                   jax.ShapeDtypeStruct((TOP_K, n), F32),
                   per_tile, per_tile, per_tile],
        grid=(n // tt,),
        in_specs=[pl.BlockSpec((e, tt), tok),
                  pl.BlockSpec((e, tt), lambda i: (0, 0)),
                  pl.BlockSpec(su.shape, lambda i: (0, 0)),
                  pl.BlockSpec(sl.shape, lambda i: (0, 0))],
        out_specs=[pl.BlockSpec((TOP_K, tt), tok), pl.BlockSpec((TOP_K, tt), tok),
                   per_tile_spec, per_tile_spec, per_tile_spec],
        scratch_shapes=[pltpu.VMEM((8, e), F32)],
        compiler_params=_cparams(("arbitrary",)),
        name="route",
    )(logits_t, bias_b, su, sl)


def _slab(ref, row, slabs):
    return ref.at[pl.ds(pl.multiple_of(row * slabs, slabs), slabs)]


def _start_segment(e, seglo_ref, cnt_ref, seenb_ref, pstart_ref, make_copy):
    cnt = cnt_ref[0, e]
    local = seglo_ref[0, e]
    glob = pstart_ref[e] + seenb_ref[0, e]
    n_chunks = cnt // SEGMENT_CHUNK

    def chunk(j, c):
        make_copy(local + j * SEGMENT_CHUNK, glob + j * SEGMENT_CHUNK, SEGMENT_CHUNK).start()
        return c

    lax.fori_loop(0, n_chunks, chunk, 0)
    done = n_chunks * SEGMENT_CHUNK
    rows = SEGMENT_CHUNK // 2
    while rows >= 1:
        take = (cnt & rows) != 0

        @pl.when(take)
        def _(done=done, rows=rows):
            make_copy(local + done, glob + done, rows).start()

        done = done + jnp.where(take, rows, 0)
        rows //= 2


def _dispatch_kernel(pstart_ref, pvalid_ref, pend_ref, nused_ref, lpos_ref, seglo_ref, cnt_ref, seenb_ref,
                     u_ref, xs_ref, obuf, zbuf, sems, sem, *, slabs):
    i = pl.program_id(0)
    last = pl.num_programs(0) - 1
    td = u_ref.shape[0] // slabs
    zrows = zbuf.shape[0] // slabs
    chunks_per_block = EXPERT_BLOCK // zrows
    n_chunks = xs_ref.shape[0] // slabs // zrows

    @pl.when(i == 0)
    def _():
        zbuf[...] = jnp.zeros_like(zbuf)

        def zero_chunk(row0):
            return pltpu.make_async_copy(
                zbuf, xs_ref.at[pl.ds(pl.multiple_of(row0 * slabs, zrows * slabs), zrows * slabs)], sem)

        def start_unused(b, c):
            zero_chunk(b * zrows).start()
            return c

        def wait_unused(b, c):
            zero_chunk(b * zrows).wait()
            return c

        first_unused = nused_ref[0] * chunks_per_block
        lax.fori_loop(first_unused, n_chunks, start_unused, 0)
        lax.fori_loop(first_unused, n_chunks, wait_unused, 0)

        def tails(e, c, *, wait):
            for g in range(chunks_per_block):
                row0 = pend_ref[e] - zrows * (g + 1)

                @pl.when(row0 + zrows > pvalid_ref[e])
                def _():
                    if wait:
                        zero_chunk(row0).wait()
                    else:
                        zero_chunk(row0).start()
            return c

        lax.fori_loop(0, N_EXPERTS, functools.partial(tails, wait=False), 0)
        lax.fori_loop(0, N_EXPERTS, functools.partial(tails, wait=True), 0)

    slot = i % 2
    buf = obuf.at[slot]
    out_sem = sems.at[slot]

    def drained(b, s):
        return pltpu.make_async_copy(obuf.at[b], xs_ref.at[pl.ds(0, obuf.shape[1])], sems.at[s])

    @pl.when(i >= 2)
    def _():
        drained(slot, slot).wait()

    def place(t, c):
        row = _slab(u_ref, t, slabs)[...]
        for k in range(TOP_K):
            _slab(buf, lpos_ref[k, t], slabs)[...] = row
        return c

    lax.fori_loop(0, td, place, 0)

    def to_table(local, glob, rows):
        return pltpu.make_async_copy(
            buf.at[pl.ds(pl.multiple_of(local * slabs, slabs), rows * slabs)],
            xs_ref.at[pl.ds(pl.multiple_of(glob * slabs, slabs), rows * slabs)], out_sem)

    def send(e, c):
        _start_segment(e, seglo_ref, cnt_ref, seenb_ref, pstart_ref, to_table)
        return c

    lax.fori_loop(0, N_EXPERTS, send, 0)

    @pl.when(i == last)
    def _():
        drained(slot, slot).wait()

        @pl.when(last >= 1)
        def _():
            drained(1 - slot, 1 - slot).wait()


def _dispatch(pstart, pvalid, pend, nused, lpos_t, seglo, cnt, seen_b, u2s, n_rows, slabs):
    n = u2s.shape[0] // slabs
    td = T_MOE
    smem_tok = pl.BlockSpec((TOP_K, td), lambda i, *_: (0, i), memory_space=pltpu.SMEM)
    smem_tile = pl.BlockSpec((8, N_EXPERTS), lambda i, *_: (i, 0), memory_space=pltpu.SMEM)
    return pl.pallas_call(
        functools.partial(_dispatch_kernel, slabs=slabs),
        out_shape=jax.ShapeDtypeStruct((n_rows * slabs, LANES), F32),
        grid_spec=pltpu.PrefetchScalarGridSpec(
            num_scalar_prefetch=4,
            grid=(n // td,),
            in_specs=[smem_tok, smem_tile, smem_tile, smem_tile,
                      pl.BlockSpec((td * slabs, LANES), lambda i, *_: (i, 0))],
            out_specs=pl.BlockSpec(memory_space=pl.ANY),
            scratch_shapes=[pltpu.VMEM((2, TOP_K * td * slabs, LANES), F32),
                            pltpu.VMEM((ZERO_ROWS * slabs, LANES), F32),
                            pltpu.SemaphoreType.DMA((2,)),
                            pltpu.SemaphoreType.DMA],
        ),
        compiler_params=_cparams(("arbitrary",)),
        name="dispatch",
    )(pstart, pvalid, pend, nused, lpos_t, seglo, cnt, seen_b, u2s)


def _expert_kernel(blk_e_ref, nused_ref, x_ref, w1_ref, w3_ref, w2_ref, y_ref, w1b, w3b, w2b, *, slabs):
    i = pl.program_id(0)

    @pl.when(i < nused_ref[0])
    def _():
        e = blk_e_ref[i]
        prev = blk_e_ref[jnp.maximum(i - 1, 0)]

        @pl.when(jnp.logical_or(i == 0, e != prev))
        def _():
            w1b[...] = w1_ref[0].astype(BF16)
            w3b[...] = w3_ref[0].astype(BF16)
            w2b[...] = w2_ref[0].astype(BF16)

        x = jnp.concatenate([c.astype(BF16) for c in _load_slab_rows(x_ref, slabs)], axis=1)
        h1 = _dot(x, w1b[...])
        h3 = _dot(x, w3b[...])
        a = (_silu(h1) * h3).astype(BF16)
        _store_slab_rows(y_ref, _dot(a, w2b[...]))

    @pl.when(i >= nused_ref[0])
    def _():
        y_ref[...] = jnp.zeros_like(y_ref)


def _experts(blk_e, nused, xs, w1, w3, w2, slabs):
    n_rows = xs.shape[0] // slabs
    e, d, f = w1.shape
    nb = n_rows // EXPERT_BLOCK

    def row_blk(i, blk_e_ref, nused_ref):
        return (jnp.minimum(i, nused_ref[0] - 1), 0)

    def w_blk(i, blk_e_ref, nused_ref):
        return (blk_e_ref[jnp.minimum(i, nused_ref[0] - 1)], 0, 0)

    return pl.pallas_call(
        functools.partial(_expert_kernel, slabs=slabs),
        out_shape=jax.ShapeDtypeStruct((n_rows * slabs, LANES), F32),
        grid_spec=pltpu.PrefetchScalarGridSpec(
            num_scalar_prefetch=2,
            grid=(nb,),
            in_specs=[pl.BlockSpec((EXPERT_BLOCK * slabs, LANES), row_blk),
                      pl.BlockSpec((1, d, f), w_blk),
                      pl.BlockSpec((1, d, f), w_blk),
                      pl.BlockSpec((1, f, d), w_blk)],
            out_specs=pl.BlockSpec((EXPERT_BLOCK * slabs, LANES), lambda i, *_: (i, 0)),
            scratch_shapes=[pltpu.VMEM((d, f), BF16), pltpu.VMEM((d, f), BF16),
                            pltpu.VMEM((f, d), BF16)],
        ),
        compiler_params=_cparams(("arbitrary",)),
        name="experts",
    )(blk_e, nused, xs, w1, w3, w2)


def _final_kernel(pstart_ref, lpos_ref, gate_ref, seglo_ref, cnt_ref, seenb_ref, seglo_nx, cnt_nx, seenb_nx,
                  u_ref, x1_ref, ada_ref, ws1_ref, ws3_ref, ws2_ref, g2_ref, b2_ref, ys_ref, o_ref,
                  lbuf, rbuf, sem, *, slabs):
    i = pl.program_id(0)
    last = pl.num_programs(0) - 1
    tf = u_ref.shape[0] // slabs

    def from_table(local, glob, rows):
        return pltpu.make_async_copy(
            ys_ref.at[pl.ds(pl.multiple_of(glob * slabs, slabs), rows * slabs)],
            lbuf.at[pl.ds(pl.multiple_of(local * slabs, slabs), rows * slabs)], sem)

    def fetch_tile(seglo_r, cnt_r, seenb_r):
        def fetch(e, c):
            _start_segment(e, seglo_r, cnt_r, seenb_r, pstart_ref, from_table)
            return c

        lax.fori_loop(0, N_EXPERTS, fetch, 0)

    @pl.when(i == 0)
    def _():
        fetch_tile(seglo_ref, cnt_ref, seenb_ref)

    pltpu.make_async_copy(ys_ref.at[pl.ds(0, lbuf.shape[0])], lbuf, sem).wait()

    def combine(t, c):
        acc = None
        for k in range(TOP_K):
            row = _slab(lbuf, lpos_ref[k, t], slabs)[...] * gate_ref[k, t]
            acc = row if acc is None else acc + row
        _slab(rbuf, t, slabs)[...] = acc
        return c

    lax.fori_loop(0, tf, combine, 0)

    @pl.when(i < last)
    def _():
        fetch_tile(seglo_nx, cnt_nx, seenb_nx)

    u = jnp.concatenate([c.astype(BF16) for c in _load_slab_rows(u_ref, slabs)], axis=1)
    shared = _dot((_silu(_dot(u, ws1_ref[...])) * _dot(u, ws3_ref[...])).astype(BF16), ws2_ref[...])
    routed = jnp.concatenate(_load_slab_rows(rbuf, slabs), axis=1)
    gate2 = ada_ref[0, 5:6, :]
    y = shared + routed
    o_ref[...] = _layer_norm(DEEPNORM_ALPHA * x1_ref[...] + (1.0 + gate2) * y) * g2_ref[...] + b2_ref[...]


def _final(pstart, lpos_t, gate_t, seglo, cnt, seen_b, u2s, x1, ada, ws1b, ws3b, ws2b, g2, b2, ys,
           seq, slabs):
    n, d = x1.shape
    tf = T_MOE
    n_tiles = n // tf
    steps_per_seq = seq // tf
    smem_tok = pl.BlockSpec((TOP_K, tf), lambda i, *_: (0, i), memory_space=pltpu.SMEM)
    smem_tile = pl.BlockSpec((8, N_EXPERTS), lambda i, *_: (i, 0), memory_space=pltpu.SMEM)
    smem_next = pl.BlockSpec((8, N_EXPERTS), lambda i, *_: (jnp.minimum(i + 1, n_tiles - 1), 0),
                             memory_space=pltpu.SMEM)
    const2 = lambda i, *_: (0, 0)
    return pl.pallas_call(
        functools.partial(_final_kernel, slabs=slabs),
        out_shape=jax.ShapeDtypeStruct((n, d), F32),
        grid_spec=pltpu.PrefetchScalarGridSpec(
            num_scalar_prefetch=1,
            grid=(n_tiles,),
            in_specs=[smem_tok, smem_tok, smem_tile, smem_tile, smem_tile, smem_next, smem_next, smem_next,
                      pl.BlockSpec((tf * slabs, LANES), lambda i, *_: (i, 0)),
                      pl.BlockSpec((tf, d), lambda i, *_: (i, 0)),
                      pl.BlockSpec((1,) + ada.shape[1:], lambda i, *_: (i // steps_per_seq, 0, 0)),
                      pl.BlockSpec(ws1b.shape, const2),
                      pl.BlockSpec(ws3b.shape, const2),
                      pl.BlockSpec(ws2b.shape, const2),
                      pl.BlockSpec(g2.shape, const2),
                      pl.BlockSpec(b2.shape, const2),
                      pl.BlockSpec(memory_space=pl.ANY)],
            out_specs=pl.BlockSpec((tf, d), lambda i, *_: (i, 0)),
            scratch_shapes=[pltpu.VMEM((TOP_K * tf * slabs, LANES), F32),
                            pltpu.VMEM((tf * slabs, LANES), F32),
                            pltpu.SemaphoreType.DMA],
        ),
        compiler_params=_cparams(("arbitrary",)),
        name="final",
    )(pstart, lpos_t, gate_t, seglo, cnt, seen_b, seglo, cnt, seen_b, u2s, x1, ada, ws1b, ws3b, ws2b,
      g2, b2, ys)


def _swa_head_permutation():
    per_group = SWA_HEADS // SWA_KV_HEADS
    cols = []
    for c in range(per_group):
        for g in range(SWA_KV_HEADS):
            h = c + per_group * g
            cols.extend(range(h * HEAD_DIM, (h + 1) * HEAD_DIM))
    return np.asarray(cols)


def _layer(x, c, positions, w_ada, b_ada, w_in, sinks, g_swa, g_sb, w_out, ln1_g, ln1_b,
           w_router, e_bias, w1, w3, w2, ws1, ws3, ws2, ln2_g, ln2_b):
    batch, seq, d = x.shape
    n = batch * seq
    x2 = x.reshape(n, d)
    perm = _swa_head_permutation()
    per_group = SWA_HEADS // SWA_KV_HEADS

    ada = _ada(c, w_ada, b_ada)

    o_q = SWA_WIDTH + 2 * SWA_KV_WIDTH
    w_in_b = jnp.concatenate(
        [w_in[:, :SWA_WIDTH][:, perm], w_in[:, SWA_WIDTH:o_q],
         w_in[:, o_q:o_q + SB_WIDTH] * (HEAD_DIM ** -0.5), w_in[:, o_q + SB_WIDTH:]], axis=1).astype(BF16)
    half = HEAD_DIM // 2
    inv_freq = ROPE_THETA ** (-jnp.arange(half, dtype=F32) * 2.0 / HEAD_DIM)
    invf = jnp.tile(inv_freq, LANES // half).reshape(1, LANES)
    pos_col = positions.reshape(n, 1).astype(F32)
    qa, ka, va, qs, ks, vs = _inproj(x2, ada, pos_col, invf, w_in_b, seq)

    head_of_block = np.asarray([c_ + per_group * g for g in range(SWA_KV_HEADS) for c_ in range(per_group)])
    sink_col = jnp.repeat(sinks.astype(F32)[head_of_block], WINDOW).reshape(SWA_HEADS * WINDOW, 1)
    oa = _swa(qa, ka, va, sink_col, batch, seq)
    ob = _sb(qs, ks, vs, batch, seq)

    w_out_b = jnp.concatenate([w_out[:SWA_WIDTH][perm], w_out[SWA_WIDTH:]], axis=0).astype(BF16)
    wr_t = w_router.T.astype(F32)
    wr_hi = wr_t.astype(BF16)
    wr_lo = (wr_t - wr_hi.astype(F32)).astype(BF16)
    x1, u2s, logits_t = _post(oa, ob, x2, ada, g_swa[perm].reshape(1, -1), g_sb.reshape(1, -1), w_out_b,
                              ln1_g.reshape(1, d), ln1_b.reshape(1, d), wr_hi, wr_lo, seq)

    lpos_t, gate_t, seglo, cnt, seen_b = _route(logits_t, e_bias)

    counts = seen_b[-1] + cnt[-1]
    padded = (counts + EXPERT_BLOCK - 1) // EXPERT_BLOCK * EXPERT_BLOCK
    pend = jnp.cumsum(padded).astype(I32)
    pstart = pend - padded
    n_rows = n * TOP_K + N_EXPERTS * EXPERT_BLOCK
    n_blocks = n_rows // EXPERT_BLOCK
    block_row0 = jnp.arange(n_blocks, dtype=I32) * EXPERT_BLOCK
    blk_e = jnp.minimum(jnp.sum((pend[None, :] <= block_row0[:, None]).astype(I32), axis=1), N_EXPERTS - 1)
    nused = (pend[-1:] // EXPERT_BLOCK).astype(I32)
    slabs = d // LANES

    xs = _dispatch(pstart, pstart + counts, pend, nused, lpos_t, seglo, cnt, seen_b, u2s, n_rows, slabs)
    ys = _experts(blk_e, nused, xs, w1, w3, w2, slabs)
    out = _final(pstart, lpos_t, gate_t, seglo, cnt, seen_b, u2s, x1, ada, ws1.astype(BF16),
                 ws3.astype(BF16), ws2.astype(BF16), ln2_g.reshape(1, d), ln2_b.reshape(1, d), ys, seq, slabs)
    return out.reshape(batch, seq, d)


def kernel(x, c, positions, w_ada, b_ada, w_in, attn_sinks, g_swa, g_sb, w_out, ln1_g, ln1_b,
           w_router, e_bias, w1, w3, w2, ws1, ws3, ws2, ln2_g, ln2_b):
    assert w_ada.shape[0] == DEPTH
    for l in range(DEPTH):
        x = _layer(x, c, positions, w_ada[l], b_ada[l], w_in[l], attn_sinks[l], g_swa[l], g_sb[l],
                   w_out[l], ln1_g[l], ln1_b[l], w_router[l], e_bias[l], w1[l], w3[l], w2[l],
                   ws1[l], ws3[l], ws2[l], ln2_g[l], ln2_b[l])
    return x
```

```python
import functools

import numpy as np
import jax
import jax.numpy as jnp
from jax import lax
from jax.experimental import pallas as pl
from jax.experimental.pallas import tpu as pltpu

F32 = jnp.float32
BF16 = jnp.bfloat16
I32 = jnp.int32

HEAD_DIM = 64
SWA_HEADS = 8
SWA_KV_HEADS = 2
SB_HEADS = 8
SWA_WIDTH = SWA_HEADS * HEAD_DIM
SWA_KV_WIDTH = SWA_KV_HEADS * HEAD_DIM
SB_WIDTH = SB_HEADS * HEAD_DIM
WINDOW = 128
ROPE_THETA = 10000.0
N_EXPERTS = 256
TOP_K = 8
N_GROUPS = 8
TOPK_GROUPS = 4
GROUP_SIZE = N_EXPERTS // N_GROUPS
ROUTED_SCALE = 2.5
LN_EPS = 1e-5
DEPTH = 1
DEEPNORM_ALPHA = (2 * DEPTH) ** 0.25

LANES = 128
SB_BLOCK = 128
SB_TILE_BLOCKS = 3
SB_UNDERFLOW = 110.0
NEG_BIG = -1e30
VMEM_LIMIT = 56 * 1024 * 1024

TM_PROJ = 512
TQ_SWA = 512
T_MOE = 512
EXPERT_BLOCK = 512
ZERO_ROWS = 128
SEGMENT_CHUNK = 8


def _cparams(sem, vmem=VMEM_LIMIT):
    return pltpu.CompilerParams(dimension_semantics=sem, vmem_limit_bytes=vmem)


def _layer_norm(x):
    mu = jnp.mean(x, axis=-1, keepdims=True)
    xc = x - mu
    var = jnp.mean(xc * xc, axis=-1, keepdims=True)
    return xc * lax.rsqrt(var + LN_EPS)


def _silu(x):
    return x * jax.nn.sigmoid(x)


def _dot(a, b):
    return jnp.dot(a, b, preferred_element_type=F32)


def _dot_nt(a, b):
    return lax.dot_general(a, b, (((1,), (1,)), ((), ())), preferred_element_type=F32)


def _store_slab_rows(ref, x):
    t, d = x.shape
    slabs = d // LANES
    for s in range(slabs):
        ref[pl.ds(s, t, stride=slabs), :] = x[:, s * LANES:(s + 1) * LANES]


def _load_slab_rows(ref, slabs):
    t = ref.shape[0] // slabs
    return [ref[pl.ds(s, t, stride=slabs), :] for s in range(slabs)]


def _ada_kernel(c_ref, w_ref, b_ref, o_ref):
    sc = _silu(c_ref[...])
    o_ref[...] = _dot(sc.astype(BF16), w_ref[...].astype(BF16)) + b_ref[...]


def _ada(c, w_ada, b_ada):
    b, d = c.shape
    n_out = w_ada.shape[1]
    rows = 8
    c_pad = jnp.zeros((rows, d), F32).at[:b].set(c)
    out = pl.pallas_call(
        _ada_kernel,
        out_shape=jax.ShapeDtypeStruct((rows, n_out), F32),
        grid=(n_out // d,),
        in_specs=[
            pl.BlockSpec((rows, d), lambda j: (0, 0)),
            pl.BlockSpec((d, d), lambda j: (0, j)),
            pl.BlockSpec((1, d), lambda j: (0, j)),
        ],
        out_specs=pl.BlockSpec((rows, d), lambda j: (0, j)),
        compiler_params=_cparams(("arbitrary",)),
        name="ada",
    )(c_pad, w_ada, b_ada.reshape(1, n_out))
    return out[:b].reshape(b, n_out // d, d)


def _inproj_kernel(x_ref, ada_ref, pos_ref, invf_ref, w_ref,
                   qa_ref, ka_ref, va_ref, qs_ref, ks_ref, vs_ref):
    x = x_ref[...]
    shift = ada_ref[0, 0:1, :]
    scale = ada_ref[0, 1:2, :]
    u = _layer_norm(x) * (1.0 + scale) + shift
    h = _dot(u.astype(BF16), w_ref[...])

    ang = pos_ref[...] * invf_ref[...]
    cs = jnp.cos(ang)
    sn = jnp.sin(ang)
    lane = lax.broadcasted_iota(I32, (1, LANES), 1)
    first = (lane & (HEAD_DIM // 2)) == 0
    sn_signed = jnp.where(first, -sn, sn)

    def rope(hc):
        partner = jnp.where(first, pltpu.roll(hc, LANES - HEAD_DIM // 2, 1),
                            pltpu.roll(hc, HEAD_DIM // 2, 1))
        return hc * cs + partner * sn_signed

    q_scale = HEAD_DIM ** -0.5
    o = 0
    for c in range(SWA_WIDTH // LANES):
        qa_ref[:, c * LANES:(c + 1) * LANES] = (rope(h[:, o:o + LANES]) * q_scale).astype(BF16)
        o += LANES
    ka_ref[...] = rope(h[:, o:o + SWA_KV_WIDTH]).astype(BF16)
    o += SWA_KV_WIDTH
    va_ref[...] = h[:, o:o + SWA_KV_WIDTH].astype(BF16)
    o += SWA_KV_WIDTH
    qs_ref[...] = h[:, o:o + SB_WIDTH].astype(BF16)
    o += SB_WIDTH
    ks_ref[...] = h[:, o:o + SB_WIDTH].astype(BF16)
    o += SB_WIDTH
    vs_ref[...] = h[:, o:o + SB_WIDTH].astype(BF16)


def _inproj(x2, ada, pos_col, invf, w_in_b, seq):
    n, d = x2.shape
    tm = TM_PROJ
    steps_per_seq = seq // tm
    widths = (SWA_WIDTH, SWA_KV_WIDTH, SWA_KV_WIDTH, SB_WIDTH, SB_WIDTH, SB_WIDTH)
    return pl.pallas_call(
        _inproj_kernel,
        out_shape=[jax.ShapeDtypeStruct((n, w), BF16) for w in widths],
        grid=(n // tm,),
        in_specs=[
            pl.BlockSpec((tm, d), lambda i: (i, 0)),
            pl.BlockSpec((1,) + ada.shape[1:], lambda i: (i // steps_per_seq, 0, 0)),
            pl.BlockSpec((tm, 1), lambda i: (i, 0)),
            pl.BlockSpec((1, LANES), lambda i: (0, 0)),
            pl.BlockSpec(w_in_b.shape, lambda i: (0, 0)),
        ],
        out_specs=[pl.BlockSpec((tm, w), lambda i: (i, 0)) for w in widths],
        compiler_params=_cparams(("arbitrary",)),
        name="inproj",
    )(x2, ada, pos_col, invf, w_in_b)


def _swa_kernel(q_ref, kc_ref, kp_ref, vc_ref, vp_ref, bias0_ref, bias_ref, sink_ref,
                o_ref, kall, vall):
    tq = q_ref.shape[0]
    kall[0:WINDOW, :] = kp_ref[...]
    kall[WINDOW:, :] = kc_ref[...]
    vall[0:WINDOW, :] = vp_ref[...]
    vall[WINDOW:, :] = vc_ref[...]
    lane = lax.broadcasted_iota(I32, (1, LANES), 1)
    half0 = lane < HEAD_DIM
    half1 = lane >= HEAD_DIM
    sink = sink_ref[...]
    n_col = SWA_WIDTH // LANES
    for j in range(tq // WINDOW):
        r0 = j * WINDOW
        q = q_ref[r0:r0 + WINDOW, :]
        parts = []
        for half in (half0, half1):
            for c in range(n_col):
                parts.append(jnp.where(half, q[:, c * LANES:(c + 1) * LANES], jnp.zeros((), BF16)))
        qm = jnp.concatenate(parts, axis=0)
        kb = kall[r0:r0 + 2 * WINDOW, :]
        vb = vall[r0:r0 + 2 * WINDOW, :]
        s = _dot_nt(qm, kb)
        s = s + (bias0_ref[0] if j == 0 else bias_ref[...])
        m = jnp.maximum(jnp.max(s, axis=1, keepdims=True), sink)
        p = jnp.exp(s - m)
        den = jnp.sum(p, axis=1, keepdims=True) + jnp.exp(sink - m)
        o = _dot(p.astype(BF16), vb) / den
        for c in range(n_col):
            lo = o[c * WINDOW:(c + 1) * WINDOW]
            hi = o[(n_col + c) * WINDOW:(n_col + c + 1) * WINDOW]
            o_ref[r0:r0 + WINDOW, c * LANES:(c + 1) * LANES] = jnp.where(half0, lo, hi)


def _swa_bias():
    qi = np.arange(SWA_HEADS * WINDOW)[:, None] % WINDOW
    kj = np.arange(2 * WINDOW)[None, :]
    dist = qi + WINDOW - kj
    band = (dist >= 0) & (dist < WINDOW)
    bias = np.where(band, 0.0, NEG_BIG).astype(np.float32)
    first = np.where(band & (kj >= WINDOW), 0.0, NEG_BIG).astype(np.float32)
    return jnp.asarray(np.stack([bias, first])), jnp.asarray(bias)


def _swa(qa, ka, va, sink_col, batch, seq):
    n = qa.shape[0]
    tq = TQ_SWA
    nt = seq // tq
    blocks_per_tile = tq // WINDOW
    bias_pair, bias = _swa_bias()

    def cur(b, i):
        return (b * nt + i, 0)

    def prev(b, i):
        return (jnp.maximum((b * nt + i) * blocks_per_tile - 1, 0), 0)

    return pl.pallas_call(
        _swa_kernel,
        out_shape=jax.ShapeDtypeStruct((n, SWA_WIDTH), F32),
        grid=(batch, nt),
        in_specs=[
            pl.BlockSpec((tq, SWA_WIDTH), cur),
            pl.BlockSpec((tq, SWA_KV_WIDTH), cur),
            pl.BlockSpec((WINDOW, SWA_KV_WIDTH), prev),
            pl.BlockSpec((tq, SWA_KV_WIDTH), cur),
            pl.BlockSpec((WINDOW, SWA_KV_WIDTH), prev),
            pl.BlockSpec((1,) + bias.shape, lambda b, i: (jnp.where(i == 0, 1, 0), 0, 0)),
            pl.BlockSpec(bias.shape, lambda b, i: (0, 0)),
            pl.BlockSpec(sink_col.shape, lambda b, i: (0, 0)),
        ],
        out_specs=pl.BlockSpec((tq, SWA_WIDTH), cur),
        scratch_shapes=[pltpu.VMEM((tq + WINDOW, SWA_KV_WIDTH), BF16),
                        pltpu.VMEM((tq + WINDOW, SWA_KV_WIDTH), BF16)],
        compiler_params=_cparams(("arbitrary", "arbitrary")),
        name="swa",
    )(qa, ka, ka, va, va, bias_pair, bias, sink_col)


def _sb_kernel(q_ref, k_ref, v_ref, uu_ref, o_ref, qm, acc, run):
    i = pl.program_id(1)
    blk = SB_BLOCK
    n_sub = SB_TILE_BLOCKS
    tile = n_sub * blk
    pairs = q_ref.shape[1] // LANES
    lane = lax.broadcasted_iota(I32, (1, LANES), 1)
    half0 = lane < HEAD_DIM
    half1 = lane >= HEAD_DIM
    zero = jnp.zeros((), BF16)
    for p in range(pairs):
        q = q_ref[:, p * LANES:(p + 1) * LANES]
        qm[p] = jnp.concatenate([jnp.where(half0, q, zero), jnp.where(half1, q, zero)], axis=0)
    acc[...] = jnp.zeros_like(acc)
    run[...] = jnp.zeros_like(run)
    q_pos = i * blk + (lax.broadcasted_iota(I32, (2 * blk, blk), 0) & (blk - 1))
    col = lax.broadcasted_iota(I32, (2 * blk, blk), 1)

    def body(carry):
        end, _ = carry
        start = pl.multiple_of(jnp.maximum(end - tile, 0), blk)
        limit = jnp.minimum(end, q_pos) - start
        least = None
        for p in range(pairs):
            kt = k_ref[pl.ds(start, tile), p * LANES:(p + 1) * LANES]
            vt = v_ref[pl.ds(start, tile), p * LANES:(p + 1) * LANES]
            z = _dot_nt(qm[p], kt)
            mass_seen = run[p]
            ws = [None] * n_sub
            for b in reversed(range(n_sub)):
                zb = jnp.where(col < limit - b * blk, z[:, b * blk:(b + 1) * blk], NEG_BIG)
                mass = jnp.maximum(zb, 0.0) + jnp.log(1.0 + jnp.exp(-jnp.abs(zb)))
                mass_hi = mass.astype(BF16)
                mass_lo = (mass - mass_hi.astype(F32)).astype(BF16)
                cc = _dot(jnp.concatenate([mass_hi, mass_lo], axis=1), uu_ref[...])
                ws[b] = jnp.exp(zb - (cc[:, :blk] + mass_seen)).astype(BF16)
                mass_seen = mass_seen + cc[:, blk:]
            acc[p] += _dot(jnp.concatenate(ws, axis=1), vt)
            run[p] = mass_seen
            least = mass_seen if least is None else jnp.minimum(least, mass_seen)
        return start, jnp.min(least) > SB_UNDERFLOW

    lax.while_loop(lambda c: jnp.logical_and(c[0] > 0, jnp.logical_not(c[1])),
                   body, ((i + 1) * blk, jnp.bool_(False)))
    for p in range(pairs):
        a = acc[p]
        o_ref[:, p * LANES:(p + 1) * LANES] = jnp.where(half0, a[:blk], a[blk:])


def _sb_suffix_matrix():
    j = np.arange(2 * SB_BLOCK)[:, None] % SB_BLOCK
    s = np.arange(2 * SB_BLOCK)[None, :]
    m = np.where(s < SB_BLOCK, (j >= s), True)
    return jnp.asarray(m.astype(np.float32)).astype(BF16)


def _sb(qs, ks, vs, batch, seq):
    n = qs.shape[0]
    blk = SB_BLOCK
    nq = seq // blk
    pairs = SB_WIDTH // LANES
    uu = _sb_suffix_matrix()
    return pl.pallas_call(
        _sb_kernel,
        out_shape=jax.ShapeDtypeStruct((n, SB_WIDTH), F32),
        grid=(batch, nq),
        in_specs=[
            pl.BlockSpec((blk, SB_WIDTH), lambda b, i: (b * nq + i, 0)),
            pl.BlockSpec((seq, SB_WIDTH), lambda b, i: (b, 0)),
            pl.BlockSpec((seq, SB_WIDTH), lambda b, i: (b, 0)),
            pl.BlockSpec(uu.shape, lambda b, i: (0, 0)),
        ],
        out_specs=pl.BlockSpec((blk, SB_WIDTH), lambda b, i: (b * nq + i, 0)),
        scratch_shapes=[pltpu.VMEM((pairs, 2 * blk, LANES), BF16),
                        pltpu.VMEM((pairs, 2 * blk, LANES), F32),
                        pltpu.VMEM((pairs, 2 * blk, LANES), F32)],
        compiler_params=_cparams(("arbitrary", "arbitrary")),
        name="sb",
    )(qs, ks, vs, uu)


def _post_kernel(oa_ref, ob_ref, x_ref, ada_ref, ga_ref, gb_ref, wo_ref, g1_ref, b1_ref,
                 wrh_ref, wrl_ref, x1_ref, u2s_ref, lg_ref):
    def rms(o, g):
        return o * lax.rsqrt(jnp.mean(o * o, axis=-1, keepdims=True) + LN_EPS) * g

    na = rms(oa_ref[...], ga_ref[...]).astype(BF16)
    nb = rms(ob_ref[...], gb_ref[...]).astype(BF16)
    y = _dot(na, wo_ref[0:SWA_WIDTH, :]) + _dot(nb, wo_ref[SWA_WIDTH:, :])
    gate1 = ada_ref[0, 2:3, :]
    shift2 = ada_ref[0, 3:4, :]
    scale2 = ada_ref[0, 4:5, :]
    x1 = _layer_norm(DEEPNORM_ALPHA * x_ref[...] + (1.0 + gate1) * y) * g1_ref[...] + b1_ref[...]
    x1_ref[...] = x1
    u2 = _layer_norm(x1) * (1.0 + scale2) + shift2
    _store_slab_rows(u2s_ref, u2)
    u_hi = u2.astype(BF16)
    u_lo = (u2 - u_hi.astype(F32)).astype(BF16)
    wrh = wrh_ref[...]
    lg_ref[...] = _dot_nt(wrh, u_hi) + (_dot_nt(wrh, u_lo) + _dot_nt(wrl_ref[...], u_hi))


def _post(oa, ob, x2, ada, g_a, g_b, w_out_b, g1, b1, wr_hi, wr_lo, seq):
    n, d = x2.shape
    tm = TM_PROJ
    steps_per_seq = seq // tm
    e = wr_hi.shape[0]
    slabs = d // LANES
    const2 = lambda i: (0, 0)
    return pl.pallas_call(
        _post_kernel,
        out_shape=[jax.ShapeDtypeStruct((n, d), F32),
                   jax.ShapeDtypeStruct((n * slabs, LANES), F32),
                   jax.ShapeDtypeStruct((e, n), F32)],
        grid=(n // tm,),
        in_specs=[
            pl.BlockSpec((tm, SWA_WIDTH), lambda i: (i, 0)),
            pl.BlockSpec((tm, SB_WIDTH), lambda i: (i, 0)),
            pl.BlockSpec((tm, d), lambda i: (i, 0)),
            pl.BlockSpec((1,) + ada.shape[1:], lambda i: (i // steps_per_seq, 0, 0)),
            pl.BlockSpec(g_a.shape, const2),
            pl.BlockSpec(g_b.shape, const2),
            pl.BlockSpec(w_out_b.shape, const2),
            pl.BlockSpec(g1.shape, const2),
            pl.BlockSpec(b1.shape, const2),
            pl.BlockSpec(wr_hi.shape, const2),
            pl.BlockSpec(wr_lo.shape, const2),
        ],
        out_specs=[pl.BlockSpec((tm, d), lambda i: (i, 0)),
                   pl.BlockSpec((tm * slabs, LANES), lambda i: (i, 0)),
                   pl.BlockSpec((e, tm), lambda i: (0, i))],
        compiler_params=_cparams(("arbitrary",)),
        name="post",
    )(oa, ob, x2, ada, g_a, g_b, w_out_b, g1, b1, wr_hi, wr_lo)


def _route_kernel(lg_ref, bias_ref, su_ref, sl_ref, lpos_ref, gate_ref, seglo_ref, cnt_ref, seenb_ref,
                  seen_row):
    i = pl.program_id(0)

    @pl.when(i == 0)
    def _():
        seen_row[...] = jnp.zeros_like(seen_row)

    tt = lg_ref.shape[1]
    ninf = -jnp.inf
    scores = jax.nn.sigmoid(lg_ref[...])
    biased = scores + bias_ref[...]

    iog32 = lax.broadcasted_iota(I32, (GROUP_SIZE, tt), 0)
    groups = [biased[g * GROUP_SIZE:(g + 1) * GROUP_SIZE, :] for g in range(N_GROUPS)]
    gs_rows = []
    for blk in groups:
        m1 = jnp.max(blk, axis=0, keepdims=True)
        i1 = jnp.min(jnp.where(blk == m1, iog32, GROUP_SIZE), axis=0, keepdims=True)
        m2 = jnp.max(jnp.where(iog32 == i1, ninf, blk), axis=0, keepdims=True)
        gs_rows.append(m1 + m2)
    gs = jnp.concatenate(gs_rows, axis=0)

    iog = lax.broadcasted_iota(I32, gs.shape, 0)
    gsel = jnp.zeros(gs.shape, F32)
    cur = gs
    for _ in range(TOPK_GROUPS):
        m = jnp.max(cur, axis=0, keepdims=True)
        idx = jnp.min(jnp.where(cur == m, iog, N_GROUPS), axis=0, keepdims=True)
        hit = iog == idx
        gsel = jnp.where(hit, 1.0, gsel)
        cur = jnp.where(hit, ninf, cur)

    cand = jnp.concatenate(
        [jnp.where(gsel[g:g + 1, :] > 0.5, groups[g], ninf) for g in range(N_GROUPS)], axis=0)
    ioe = lax.broadcasted_iota(I32, cand.shape, 0)
    chosen = jnp.zeros(cand.shape, F32)
    idxs, gates = [], []
    for _ in range(TOP_K):
        m = jnp.max(cand, axis=0, keepdims=True)
        idx = jnp.min(jnp.where(cand == m, ioe, N_EXPERTS), axis=0, keepdims=True)
        hit = ioe == idx
        gates.append(jnp.sum(jnp.where(hit, scores, 0.0), axis=0, keepdims=True))
        cand = jnp.where(hit, ninf, cand)
        chosen = jnp.where(hit, 1.0, chosen)
        idxs.append(idx)
    gsum = gates[0]
    for g in gates[1:]:
        gsum = gsum + g
    gates = [g / gsum * ROUTED_SCALE for g in gates]

    def hi_lo(x):
        hi = x.astype(BF16)
        return hi, (x - hi.astype(F32)).astype(BF16)

    chosen_b = chosen.astype(BF16)
    rk = _dot(chosen_b, su_ref[...])
    c_hi, c_lo = hi_lo(rk[:, tt:])
    first_col = _dot(sl_ref[...], c_hi) + _dot(sl_ref[...], c_lo)
    slot_mat = rk[:, :tt] + jnp.concatenate([first_col] * (tt // LANES), axis=1)
    slots = [jnp.sum(jnp.where(ioe == idx, slot_mat, 0.0), axis=0, keepdims=True) for idx in idxs]
    lpos_ref[...] = jnp.concatenate(slots, axis=0).astype(I32)
    gate_ref[...] = jnp.concatenate(gates, axis=0)
    cnt_row = _dot_nt(jnp.ones((8, tt), BF16), chosen_b)
    r_hi, r_lo = hi_lo(cnt_row)
    before_row = seen_row[...]
    seen_row[...] = before_row + cnt_row
    seglo_ref[...] = (_dot_nt(r_hi, sl_ref[...]) + _dot_nt(r_lo, sl_ref[...])).astype(I32)
    cnt_ref[...] = cnt_row.astype(I32)
    seenb_ref[...] = before_row.astype(I32)


def _route_prefix_matrix(tt):
    a = np.arange(tt)[:, None] < np.arange(tt)[None, :]
    m = np.concatenate([a, np.ones((tt, LANES), bool)], axis=1)
    return jnp.asarray(m.astype(np.float32)).astype(BF16)


def _route(logits_t, e_bias):
    e, n = logits_t.shape
    tt = T_MOE
    bias_b = jnp.broadcast_to(e_bias.astype(F32)[:, None], (e, tt))
    su = _route_prefix_matrix(tt)
    sl = jnp.asarray((np.arange(e)[None, :] < np.arange(e)[:, None]).astype(np.float32)).astype(BF16)
    tok = lambda i: (0, i)
    per_tile = jax.ShapeDtypeStruct((n // tt * 8, e), I32)
    per_tile_spec = pl.BlockSpec((8, e), lambda i: (i, 0))
    return pl.pallas_call(
        _route_kernel,
        out_shape=[jax.ShapeDtypeStruct((TOP_K, n), I32),
                   jax.ShapeDtypeStruct((TOP_K, n), F32),
                   per_tile, per_tile, per_tile],
        grid=(n // tt,),
        in_specs=[pl.BlockSpec((e, tt), tok),
                  pl.BlockSpec((e, tt), lambda i: (0, 0)),
                  pl.BlockSpec(su.shape, lambda i: (0, 0)),
                  pl.BlockSpec(sl.shape, lambda i: (0, 0))],
        out_specs=[pl.BlockSpec((TOP_K, tt), tok), pl.BlockSpec((TOP_K, tt), tok),
                   per_tile_spec, per_tile_spec, per_tile_spec],
        scratch_shapes=[pltpu.VMEM((8, e), F32)],
        compiler_params=_cparams(("arbitrary",)),
        name="route",
    )(logits_t, bias_b, su, sl)


def _slab(ref, row, slabs):
    return ref.at[pl.ds(pl.multiple_of(row * slabs, slabs), slabs)]


def _start_segment(e, seglo_ref, cnt_ref, seenb_ref, pstart_ref, make_copy):
    cnt = cnt_ref[0, e]
    local = seglo_ref[0, e]
    glob = pstart_ref[e] + seenb_ref[0, e]
    n_chunks = cnt // SEGMENT_CHUNK

    def chunk(j, c):
        make_copy(local + j * SEGMENT_CHUNK, glob + j * SEGMENT_CHUNK, SEGMENT_CHUNK).start()
        return c

    lax.fori_loop(0, n_chunks, chunk, 0)
    done = n_chunks * SEGMENT_CHUNK
    rows = SEGMENT_CHUNK // 2
    while rows >= 1:
        take = (cnt & rows) != 0

        @pl.when(take)
        def _(done=done, rows=rows):
            make_copy(local + done, glob + done, rows).start()

        done = done + jnp.where(take, rows, 0)
        rows //= 2


def _dispatch_kernel(pstart_ref, pvalid_ref, pend_ref, nused_ref, lpos_ref, seglo_ref, cnt_ref, seenb_ref,
                     u_ref, xs_ref, obuf, zbuf, sems, sem, *, slabs):
    i = pl.program_id(0)
    last = pl.num_programs(0) - 1
    td = u_ref.shape[0] // slabs
    zrows = zbuf.shape[0] // slabs
    chunks_per_block = EXPERT_BLOCK // zrows
    n_chunks = xs_ref.shape[0] // slabs // zrows

    @pl.when(i == 0)
    def _():
        zbuf[...] = jnp.zeros_like(zbuf)

        def zero_chunk(row0):
            return pltpu.make_async_copy(
                zbuf, xs_ref.at[pl.ds(pl.multiple_of(row0 * slabs, zrows * slabs), zrows * slabs)], sem)

        def start_unused(b, c):
            zero_chunk(b * zrows).start()
            return c

        def wait_unused(b, c):
            zero_chunk(b * zrows).wait()
            return c

        first_unused = nused_ref[0] * chunks_per_block
        lax.fori_loop(first_unused, n_chunks, start_unused, 0)
        lax.fori_loop(first_unused, n_chunks, wait_unused, 0)

        def tails(e, c, *, wait):
            for g in range(chunks_per_block):
                row0 = pend_ref[e] - zrows * (g + 1)

                @pl.when(row0 + zrows > pvalid_ref[e])
                def _():
                    if wait:
                        zero_chunk(row0).wait()
                    else:
                        zero_chunk(row0).start()
            return c

        lax.fori_loop(0, N_EXPERTS, functools.partial(tails, wait=False), 0)
        lax.fori_loop(0, N_EXPERTS, functools.partial(tails, wait=True), 0)

    slot = i % 2
    buf = obuf.at[slot]
    out_sem = sems.at[slot]

    def drained(b, s):
        return pltpu.make_async_copy(obuf.at[b], xs_ref.at[pl.ds(0, obuf.shape[1])], sems.at[s])

    @pl.when(i >= 2)
    def _():
        drained(slot, slot).wait()

    def place(t, c):
        row = _slab(u_ref, t, slabs)[...]
        for k in range(TOP_K):
            _slab(buf, lpos_ref[k, t], slabs)[...] = row
        return c

    lax.fori_loop(0, td, place, 0)

    def to_table(local, glob, rows):
        return pltpu.make_async_copy(
            buf.at[pl.ds(pl.multiple_of(local * slabs, slabs), rows * slabs)],
            xs_ref.at[pl.ds(pl.multiple_of(glob * slabs, slabs), rows * slabs)], out_sem)

    def send(e, c):
        _start_segment(e, seglo_ref, cnt_ref, seenb_ref, pstart_ref, to_table)
        return c

    lax.fori_loop(0, N_EXPERTS, send, 0)

    @pl.when(i == last)
    def _():
        drained(slot, slot).wait()

        @pl.when(last >= 1)
        def _():
            drained(1 - slot, 1 - slot).wait()


def _dispatch(pstart, pvalid, pend, nused, lpos_t, seglo, cnt, seen_b, u2s, n_rows, slabs):
    n = u2s.shape[0] // slabs
    td = T_MOE
    smem_tok = pl.BlockSpec((TOP_K, td), lambda i, *_: (0, i), memory_space=pltpu.SMEM)
    smem_tile = pl.BlockSpec((8, N_EXPERTS), lambda i, *_: (i, 0), memory_space=pltpu.SMEM)
    return pl.pallas_call(
        functools.partial(_dispatch_kernel, slabs=slabs),
        out_shape=jax.ShapeDtypeStruct((n_rows * slabs, LANES), F32),
        grid_spec=pltpu.PrefetchScalarGridSpec(
            num_scalar_prefetch=4,
            grid=(n // td,),
            in_specs=[smem_tok, smem_tile, smem_tile, smem_tile,
                      pl.BlockSpec((td * slabs, LANES), lambda i, *_: (i, 0))],
            out_specs=pl.BlockSpec(memory_space=pl.ANY),
            scratch_shapes=[pltpu.VMEM((2, TOP_K * td * slabs, LANES), F32),
                            pltpu.VMEM((ZERO_ROWS * slabs, LANES), F32),
                            pltpu.SemaphoreType.DMA((2,)),
                            pltpu.SemaphoreType.DMA],
        ),
        compiler_params=_cparams(("arbitrary",)),
        name="dispatch",
    )(pstart, pvalid, pend, nused, lpos_t, seglo, cnt, seen_b, u2s)


def _expert_kernel(blk_e_ref, nused_ref, first_ref, slot_ref, next_e_ref, x_ref, w1_hbm, w3_hbm, w2_hbm,
                   y_ref, wf1, wf3, wf2, w1b, w3b, w2b, sems, *, slabs):
    i = pl.program_id(0)

    def weights(e, s):
        return [pltpu.make_async_copy(src.at[e], dst.at[s], sems.at[s])
                for src, dst in ((w1_hbm, wf1), (w3_hbm, wf3), (w2_hbm, wf2))]

    @pl.when(i < nused_ref[0])
    def _():
        @pl.when(first_ref[i] == 1)
        def _():
            e = blk_e_ref[i]
            s = slot_ref[i]

            @pl.when(i == 0)
            def _():
                for c in weights(e, s):
                    c.start()

            for c in weights(e, s):
                c.wait()
            w1b[...] = wf1[s].astype(BF16)
            w3b[...] = wf3[s].astype(BF16)
            w2b[...] = wf2[s].astype(BF16)

            @pl.when(next_e_ref[i] >= 0)
            def _():
                for c in weights(next_e_ref[i], 1 - s):
                    c.start()

        x = jnp.concatenate([c.astype(BF16) for c in _load_slab_rows(x_ref, slabs)], axis=1)
        h1 = _dot(x, w1b[...])
        h3 = _dot(x, w3b[...])
        a = (_silu(h1) * h3).astype(BF16)
        _store_slab_rows(y_ref, _dot(a, w2b[...]))

    @pl.when(i >= nused_ref[0])
    def _():
        y_ref[...] = jnp.zeros_like(y_ref)


def _experts(blk_e, nused, xs, w1, w3, w2, slabs):
    n_rows = xs.shape[0] // slabs
    e, d, f = w1.shape
    nb = n_rows // EXPERT_BLOCK

    blk = jnp.arange(nb, dtype=I32)
    first = ((blk == 0) | (blk_e != jnp.roll(blk_e, 1))) & (blk < nused[0])
    slot = (jnp.cumsum(first.astype(I32)) - 1) & 1
    opens_at = jnp.where(first, blk, nb)
    next_open = jnp.concatenate([lax.cummin(opens_at, reverse=True)[1:], jnp.full((1,), nb, I32)])
    next_e = jnp.where(next_open < nb, blk_e[jnp.minimum(next_open, nb - 1)], -1).astype(I32)

    def row_blk(i, blk_e_ref, nused_ref, *_):
        return (jnp.minimum(i, nused_ref[0] - 1), 0)

    any_space = pl.BlockSpec(memory_space=pl.ANY)
    return pl.pallas_call(
        functools.partial(_expert_kernel, slabs=slabs),
        out_shape=jax.ShapeDtypeStruct((n_rows * slabs, LANES), F32),
        grid_spec=pltpu.PrefetchScalarGridSpec(
            num_scalar_prefetch=5,
            grid=(nb,),
            in_specs=[pl.BlockSpec((EXPERT_BLOCK * slabs, LANES), row_blk),
                      any_space, any_space, any_space],
            out_specs=pl.BlockSpec((EXPERT_BLOCK * slabs, LANES), lambda i, *_: (i, 0)),
            scratch_shapes=[pltpu.VMEM((2, d, f), F32), pltpu.VMEM((2, d, f), F32),
                            pltpu.VMEM((2, f, d), F32),
                            pltpu.VMEM((d, f), BF16), pltpu.VMEM((d, f), BF16),
                            pltpu.VMEM((f, d), BF16),
                            pltpu.SemaphoreType.DMA((2,))],
        ),
        compiler_params=_cparams(("arbitrary",)),
        name="experts",
    )(blk_e, nused, first.astype(I32), slot.astype(I32), next_e, xs, w1, w3, w2)


def _final_kernel(pstart_ref, lpos_ref, gate_ref, seglo_ref, cnt_ref, seenb_ref, seglo_nx, cnt_nx, seenb_nx,
                  u_ref, x1_ref, ada_ref, ws1_ref, ws3_ref, ws2_ref, g2_ref, b2_ref, ys_ref, o_ref,
                  lbuf, rbuf, sem, *, slabs):
    i = pl.program_id(0)
    last = pl.num_programs(0) - 1
    tf = u_ref.shape[0] // slabs

    def from_table(local, glob, rows):
        return pltpu.make_async_copy(
            ys_ref.at[pl.ds(pl.multiple_of(glob * slabs, slabs), rows * slabs)],
            lbuf.at[pl.ds(pl.multiple_of(local * slabs, slabs), rows * slabs)], sem)

    def fetch_tile(seglo_r, cnt_r, seenb_r):
        def fetch(e, c):
            _start_segment(e, seglo_r, cnt_r, seenb_r, pstart_ref, from_table)
            return c

        lax.fori_loop(0, N_EXPERTS, fetch, 0)

    @pl.when(i == 0)
    def _():
        fetch_tile(seglo_ref, cnt_ref, seenb_ref)

    pltpu.make_async_copy(ys_ref.at[pl.ds(0, lbuf.shape[0])], lbuf, sem).wait()

    def combine(t, c):
        acc = None
        for k in range(TOP_K):
            row = _slab(lbuf, lpos_ref[k, t], slabs)[...] * gate_ref[k, t]
            acc = row if acc is None else acc + row
        _slab(rbuf, t, slabs)[...] = acc
        return c

    lax.fori_loop(0, tf, combine, 0)

    @pl.when(i < last)
    def _():
        fetch_tile(seglo_nx, cnt_nx, seenb_nx)

    u = jnp.concatenate([c.astype(BF16) for c in _load_slab_rows(u_ref, slabs)], axis=1)
    shared = _dot((_silu(_dot(u, ws1_ref[...])) * _dot(u, ws3_ref[...])).astype(BF16), ws2_ref[...])
    routed = jnp.concatenate(_load_slab_rows(rbuf, slabs), axis=1)
    gate2 = ada_ref[0, 5:6, :]
    y = shared + routed
    o_ref[...] = _layer_norm(DEEPNORM_ALPHA * x1_ref[...] + (1.0 + gate2) * y) * g2_ref[...] + b2_ref[...]


def _final(pstart, lpos_t, gate_t, seglo, cnt, seen_b, u2s, x1, ada, ws1b, ws3b, ws2b, g2, b2, ys,
           seq, slabs):
    n, d = x1.shape
    tf = T_MOE
    n_tiles = n // tf
    steps_per_seq = seq // tf
    smem_tok = pl.BlockSpec((TOP_K, tf), lambda i, *_: (0, i), memory_space=pltpu.SMEM)
    smem_tile = pl.BlockSpec((8, N_EXPERTS), lambda i, *_: (i, 0), memory_space=pltpu.SMEM)
    smem_next = pl.BlockSpec((8, N_EXPERTS), lambda i, *_: (jnp.minimum(i + 1, n_tiles - 1), 0),
                             memory_space=pltpu.SMEM)
    const2 = lambda i, *_: (0, 0)
    return pl.pallas_call(
        functools.partial(_final_kernel, slabs=slabs),
        out_shape=jax.ShapeDtypeStruct((n, d), F32),
        grid_spec=pltpu.PrefetchScalarGridSpec(
            num_scalar_prefetch=1,
            grid=(n_tiles,),
            in_specs=[smem_tok, smem_tok, smem_tile, smem_tile, smem_tile, smem_next, smem_next, smem_next,
                      pl.BlockSpec((tf * slabs, LANES), lambda i, *_: (i, 0)),
                      pl.BlockSpec((tf, d), lambda i, *_: (i, 0)),
                      pl.BlockSpec((1,) + ada.shape[1:], lambda i, *_: (i // steps_per_seq, 0, 0)),
                      pl.BlockSpec(ws1b.shape, const2),
                      pl.BlockSpec(ws3b.shape, const2),
                      pl.BlockSpec(ws2b.shape, const2),
                      pl.BlockSpec(g2.shape, const2),
                      pl.BlockSpec(b2.shape, const2),
                      pl.BlockSpec(memory_space=pl.ANY)],
            out_specs=pl.BlockSpec((tf, d), lambda i, *_: (i, 0)),
            scratch_shapes=[pltpu.VMEM((TOP_K * tf * slabs, LANES), F32),
                            pltpu.VMEM((tf * slabs, LANES), F32),
                            pltpu.SemaphoreType.DMA],
        ),
        compiler_params=_cparams(("arbitrary",)),
        name="final",
    )(pstart, lpos_t, gate_t, seglo, cnt, seen_b, seglo, cnt, seen_b, u2s, x1, ada, ws1b, ws3b, ws2b,
      g2, b2, ys)


def _swa_head_permutation():
    per_group = SWA_HEADS // SWA_KV_HEADS
    cols = []
    for c in range(per_group):
        for g in range(SWA_KV_HEADS):
            h = c + per_group * g
            cols.extend(range(h * HEAD_DIM, (h + 1) * HEAD_DIM))
    return np.asarray(cols)


def _layer(x, c, positions, w_ada, b_ada, w_in, sinks, g_swa, g_sb, w_out, ln1_g, ln1_b,
           w_router, e_bias, w1, w3, w2, ws1, ws3, ws2, ln2_g, ln2_b):
    batch, seq, d = x.shape
    n = batch * seq
    x2 = x.reshape(n, d)
    perm = _swa_head_permutation()
    per_group = SWA_HEADS // SWA_KV_HEADS

    ada = _ada(c, w_ada, b_ada)

    o_q = SWA_WIDTH + 2 * SWA_KV_WIDTH
    w_in_b = jnp.concatenate(
        [w_in[:, :SWA_WIDTH][:, perm], w_in[:, SWA_WIDTH:o_q],
         w_in[:, o_q:o_q + SB_WIDTH] * (HEAD_DIM ** -0.5), w_in[:, o_q + SB_WIDTH:]], axis=1).astype(BF16)
    half = HEAD_DIM // 2
    inv_freq = ROPE_THETA ** (-jnp.arange(half, dtype=F32) * 2.0 / HEAD_DIM)
    invf = jnp.tile(inv_freq, LANES // half).reshape(1, LANES)
    pos_col = positions.reshape(n, 1).astype(F32)
    qa, ka, va, qs, ks, vs = _inproj(x2, ada, pos_col, invf, w_in_b, seq)

    head_of_block = np.asarray([c_ + per_group * g for g in range(SWA_KV_HEADS) for c_ in range(per_group)])
    sink_col = jnp.repeat(sinks.astype(F32)[head_of_block], WINDOW).reshape(SWA_HEADS * WINDOW, 1)
    oa = _swa(qa, ka, va, sink_col, batch, seq)
    ob = _sb(qs, ks, vs, batch, seq)

    w_out_b = jnp.concatenate([w_out[:SWA_WIDTH][perm], w_out[SWA_WIDTH:]], axis=0).astype(BF16)
    wr_t = w_router.T.astype(F32)
    wr_hi = wr_t.astype(BF16)
    wr_lo = (wr_t - wr_hi.astype(F32)).astype(BF16)
    x1, u2s, logits_t = _post(oa, ob, x2, ada, g_swa[perm].reshape(1, -1), g_sb.reshape(1, -1), w_out_b,
                              ln1_g.reshape(1, d), ln1_b.reshape(1, d), wr_hi, wr_lo, seq)

    lpos_t, gate_t, seglo, cnt, seen_b = _route(logits_t, e_bias)

    counts = seen_b[-1] + cnt[-1]
    padded = (counts + EXPERT_BLOCK - 1) // EXPERT_BLOCK * EXPERT_BLOCK
    pend = jnp.cumsum(padded).astype(I32)
    pstart = pend - padded
    n_rows = n * TOP_K + N_EXPERTS * EXPERT_BLOCK
    n_blocks = n_rows // EXPERT_BLOCK
    block_row0 = jnp.arange(n_blocks, dtype=I32) * EXPERT_BLOCK
    blk_e = jnp.minimum(jnp.sum((pend[None, :] <= block_row0[:, None]).astype(I32), axis=1), N_EXPERTS - 1)
    nused = (pend[-1:] // EXPERT_BLOCK).astype(I32)
    slabs = d // LANES

    xs = _dispatch(pstart, pstart + counts, pend, nused, lpos_t, seglo, cnt, seen_b, u2s, n_rows, slabs)
    ys = _experts(blk_e, nused, xs, w1, w3, w2, slabs)
    out = _final(pstart, lpos_t, gate_t, seglo, cnt, seen_b, u2s, x1, ada, ws1.astype(BF16),
                 ws3.astype(BF16), ws2.astype(BF16), ln2_g.reshape(1, d), ln2_b.reshape(1, d), ys, seq, slabs)
    return out.reshape(batch, seq, d)


def kernel(x, c, positions, w_ada, b_ada, w_in, attn_sinks, g_swa, g_sb, w_out, ln1_g, ln1_b,
           w_router, e_bias, w1, w3, w2, ws1, ws3, ws2, ln2_g, ln2_b):
    assert w_ada.shape[0] == DEPTH
    for l in range(DEPTH):
        x = _layer(x, c, positions, w_ada[l], b_ada[l], w_in[l], attn_sinks[l], g_swa[l], g_sb[l],
                   w_out[l], ln1_g[l], ln1_b[l], w_router[l], e_bias[l], w1[l], w3[l], w2[l],
                   ws1[l], ws3[l], ws2[l], ln2_g[l], ln2_b[l])
    return x
```

```python
import functools

import numpy as np
import jax
import jax.numpy as jnp
from jax import lax
from jax.experimental import pallas as pl
from jax.experimental.pallas import tpu as pltpu

F32 = jnp.float32
BF16 = jnp.bfloat16
I32 = jnp.int32

HEAD_DIM = 64
SWA_HEADS = 8
SWA_KV_HEADS = 2
SB_HEADS = 8
SWA_WIDTH = SWA_HEADS * HEAD_DIM
SWA_KV_WIDTH = SWA_KV_HEADS * HEAD_DIM
SB_WIDTH = SB_HEADS * HEAD_DIM
WINDOW = 128
ROPE_THETA = 10000.0
N_EXPERTS = 256
TOP_K = 8
N_GROUPS = 8
TOPK_GROUPS = 4
GROUP_SIZE = N_EXPERTS // N_GROUPS
ROUTED_SCALE = 2.5
LN_EPS = 1e-5
DEPTH = 1
DEEPNORM_ALPHA = (2 * DEPTH) ** 0.25

LANES = 128
SB_BLOCK = 128
SB_TILE_BLOCKS = 3
SB_UNDERFLOW = 110.0
NEG_BIG = -1e30
VMEM_LIMIT = 56 * 1024 * 1024

TM_PROJ = 512
TQ_SWA = 512
T_MOE = 512
EXPERT_BLOCK = 512
ZERO_ROWS = 128
SEGMENT_CHUNK = 8
MOE_ROW_UNROLL = 4


def _cparams(sem, vmem=VMEM_LIMIT):
    return pltpu.CompilerParams(dimension_semantics=sem, vmem_limit_bytes=vmem)


def _layer_norm(x):
    mu = jnp.mean(x, axis=-1, keepdims=True)
    xc = x - mu
    var = jnp.mean(xc * xc, axis=-1, keepdims=True)
    return xc * lax.rsqrt(var + LN_EPS)


def _silu(x):
    return x * jax.nn.sigmoid(x)


def _dot(a, b):
    return jnp.dot(a, b, preferred_element_type=F32)


def _dot_nt(a, b):
    return lax.dot_general(a, b, (((1,), (1,)), ((), ())), preferred_element_type=F32)


def _store_slab_rows(ref, x):
    t, d = x.shape
    slabs = d // LANES
    for s in range(slabs):
        ref[pl.ds(s, t, stride=slabs), :] = x[:, s * LANES:(s + 1) * LANES]


def _load_slab_rows(ref, slabs):
    t = ref.shape[0] // slabs
    return [ref[pl.ds(s, t, stride=slabs), :] for s in range(slabs)]


def _ada_kernel(c_ref, w_ref, b_ref, o_ref):
    sc = _silu(c_ref[...])
    o_ref[...] = _dot(sc.astype(BF16), w_ref[...].astype(BF16)) + b_ref[...]


def _ada(c, w_ada, b_ada):
    b, d = c.shape
    n_out = w_ada.shape[1]
    rows = 8
    c_pad = jnp.zeros((rows, d), F32).at[:b].set(c)
    out = pl.pallas_call(
        _ada_kernel,
        out_shape=jax.ShapeDtypeStruct((rows, n_out), F32),
        grid=(n_out // d,),
        in_specs=[
            pl.BlockSpec((rows, d), lambda j: (0, 0)),
            pl.BlockSpec((d, d), lambda j: (0, j)),
            pl.BlockSpec((1, d), lambda j: (0, j)),
        ],
        out_specs=pl.BlockSpec((rows, d), lambda j: (0, j)),
        compiler_params=_cparams(("arbitrary",)),
        name="ada",
    )(c_pad, w_ada, b_ada.reshape(1, n_out))
    return out[:b].reshape(b, n_out // d, d)


def _inproj_kernel(x_ref, ada_ref, pos_ref, invf_ref, w_ref,
                   qa_ref, ka_ref, va_ref, qs_ref, ks_ref, vs_ref):
    x = x_ref[...]
    shift = ada_ref[0, 0:1, :]
    scale = ada_ref[0, 1:2, :]
    u = _layer_norm(x) * (1.0 + scale) + shift
    h = _dot(u.astype(BF16), w_ref[...])

    ang = pos_ref[...] * invf_ref[...]
    cs = jnp.cos(ang)
    sn = jnp.sin(ang)
    lane = lax.broadcasted_iota(I32, (1, LANES), 1)
    first = (lane & (HEAD_DIM // 2)) == 0
    sn_signed = jnp.where(first, -sn, sn)

    def rope(hc):
        partner = jnp.where(first, pltpu.roll(hc, LANES - HEAD_DIM // 2, 1),
                            pltpu.roll(hc, HEAD_DIM // 2, 1))
        return hc * cs + partner * sn_signed

    q_scale = HEAD_DIM ** -0.5
    o = 0
    for c in range(SWA_WIDTH // LANES):
        qa_ref[:, c * LANES:(c + 1) * LANES] = (rope(h[:, o:o + LANES]) * q_scale).astype(BF16)
        o += LANES
    ka_ref[...] = rope(h[:, o:o + SWA_KV_WIDTH]).astype(BF16)
    o += SWA_KV_WIDTH
    va_ref[...] = h[:, o:o + SWA_KV_WIDTH].astype(BF16)
    o += SWA_KV_WIDTH
    qs_ref[...] = h[:, o:o + SB_WIDTH].astype(BF16)
    o += SB_WIDTH
    ks_ref[...] = h[:, o:o + SB_WIDTH].astype(BF16)
    o += SB_WIDTH
    vs_ref[...] = h[:, o:o + SB_WIDTH].astype(BF16)


def _inproj(x2, ada, pos_col, invf, w_in_b, seq):
    n, d = x2.shape
    tm = TM_PROJ
    steps_per_seq = seq // tm
    widths = (SWA_WIDTH, SWA_KV_WIDTH, SWA_KV_WIDTH, SB_WIDTH, SB_WIDTH, SB_WIDTH)
    return pl.pallas_call(
        _inproj_kernel,
        out_shape=[jax.ShapeDtypeStruct((n, w), BF16) for w in widths],
        grid=(n // tm,),
        in_specs=[
            pl.BlockSpec((tm, d), lambda i: (i, 0)),
            pl.BlockSpec((1,) + ada.shape[1:], lambda i: (i // steps_per_seq, 0, 0)),
            pl.BlockSpec((tm, 1), lambda i: (i, 0)),
            pl.BlockSpec((1, LANES), lambda i: (0, 0)),
            pl.BlockSpec(w_in_b.shape, lambda i: (0, 0)),
        ],
        out_specs=[pl.BlockSpec((tm, w), lambda i: (i, 0)) for w in widths],
        compiler_params=_cparams(("arbitrary",)),
        name="inproj",
    )(x2, ada, pos_col, invf, w_in_b)


def _swa_kernel(q_ref, kc_ref, kp_ref, vc_ref, vp_ref, bias0_ref, bias_ref, sink_ref,
                o_ref, kall, vall):
    tq = q_ref.shape[0]
    kall[0:WINDOW, :] = kp_ref[...]
    kall[WINDOW:, :] = kc_ref[...]
    vall[0:WINDOW, :] = vp_ref[...]
    vall[WINDOW:, :] = vc_ref[...]
    lane = lax.broadcasted_iota(I32, (1, LANES), 1)
    half0 = lane < HEAD_DIM
    half1 = lane >= HEAD_DIM
    sink = sink_ref[...]
    n_col = SWA_WIDTH // LANES
    for j in range(tq // WINDOW):
        r0 = j * WINDOW
        q = q_ref[r0:r0 + WINDOW, :]
        parts = []
        for half in (half0, half1):
            for c in range(n_col):
                parts.append(jnp.where(half, q[:, c * LANES:(c + 1) * LANES], jnp.zeros((), BF16)))
        qm = jnp.concatenate(parts, axis=0)
        kb = kall[r0:r0 + 2 * WINDOW, :]
        vb = vall[r0:r0 + 2 * WINDOW, :]
        s = _dot_nt(qm, kb)
        s = s + (bias0_ref[0] if j == 0 else bias_ref[...])
        m = jnp.maximum(jnp.max(s, axis=1, keepdims=True), sink)
        p = jnp.exp(s - m)
        den = jnp.sum(p, axis=1, keepdims=True) + jnp.exp(sink - m)
        o = _dot(p.astype(BF16), vb) / den
        for c in range(n_col):
            lo = o[c * WINDOW:(c + 1) * WINDOW]
            hi = o[(n_col + c) * WINDOW:(n_col + c + 1) * WINDOW]
            o_ref[r0:r0 + WINDOW, c * LANES:(c + 1) * LANES] = jnp.where(half0, lo, hi)


def _swa_bias():
    qi = np.arange(SWA_HEADS * WINDOW)[:, None] % WINDOW
    kj = np.arange(2 * WINDOW)[None, :]
    dist = qi + WINDOW - kj
    band = (dist >= 0) & (dist < WINDOW)
    bias = np.where(band, 0.0, NEG_BIG).astype(np.float32)
    first = np.where(band & (kj >= WINDOW), 0.0, NEG_BIG).astype(np.float32)
    return jnp.asarray(np.stack([bias, first])), jnp.asarray(bias)


def _swa(qa, ka, va, sink_col, batch, seq):
    n = qa.shape[0]
    tq = TQ_SWA
    nt = seq // tq
    blocks_per_tile = tq // WINDOW
    bias_pair, bias = _swa_bias()

    def cur(b, i):
        return (b * nt + i, 0)

    def prev(b, i):
        return (jnp.maximum((b * nt + i) * blocks_per_tile - 1, 0), 0)

    return pl.pallas_call(
        _swa_kernel,
        out_shape=jax.ShapeDtypeStruct((n, SWA_WIDTH), F32),
        grid=(batch, nt),
        in_specs=[
            pl.BlockSpec((tq, SWA_WIDTH), cur),
            pl.BlockSpec((tq, SWA_KV_WIDTH), cur),
            pl.BlockSpec((WINDOW, SWA_KV_WIDTH), prev),
            pl.BlockSpec((tq, SWA_KV_WIDTH), cur),
            pl.BlockSpec((WINDOW, SWA_KV_WIDTH), prev),
            pl.BlockSpec((1,) + bias.shape, lambda b, i: (jnp.where(i == 0, 1, 0), 0, 0)),
            pl.BlockSpec(bias.shape, lambda b, i: (0, 0)),
            pl.BlockSpec(sink_col.shape, lambda b, i: (0, 0)),
        ],
        out_specs=pl.BlockSpec((tq, SWA_WIDTH), cur),
        scratch_shapes=[pltpu.VMEM((tq + WINDOW, SWA_KV_WIDTH), BF16),
                        pltpu.VMEM((tq + WINDOW, SWA_KV_WIDTH), BF16)],
        compiler_params=_cparams(("arbitrary", "arbitrary")),
        name="swa",
    )(qa, ka, ka, va, va, bias_pair, bias, sink_col)


def _sb_kernel(q_ref, k_ref, v_ref, uu_ref, o_ref, qm, acc, run):
    i = pl.program_id(1)
    blk = SB_BLOCK
    n_sub = SB_TILE_BLOCKS
    tile = n_sub * blk
    pairs = q_ref.shape[1] // LANES
    lane = lax.broadcasted_iota(I32, (1, LANES), 1)
    half0 = lane < HEAD_DIM
    half1 = lane >= HEAD_DIM
    zero = jnp.zeros((), BF16)
    for p in range(pairs):
        q = q_ref[:, p * LANES:(p + 1) * LANES]
        qm[p] = jnp.concatenate([jnp.where(half0, q, zero), jnp.where(half1, q, zero)], axis=0)
    acc[...] = jnp.zeros_like(acc)
    run[...] = jnp.zeros_like(run)
    q_pos = i * blk + (lax.broadcasted_iota(I32, (2 * blk, blk), 0) & (blk - 1))
    col = lax.broadcasted_iota(I32, (2 * blk, blk), 1)

    def body(carry):
        end, _ = carry
        start = pl.multiple_of(jnp.maximum(end - tile, 0), blk)
        limit = jnp.minimum(end, q_pos) - start
        least = None
        for p in range(pairs):
            kt = k_ref[pl.ds(start, tile), p * LANES:(p + 1) * LANES]
            vt = v_ref[pl.ds(start, tile), p * LANES:(p + 1) * LANES]
            z = _dot_nt(qm[p], kt)
            mass_seen = run[p]
            ws = [None] * n_sub
            for b in reversed(range(n_sub)):
                zb = jnp.where(col < limit - b * blk, z[:, b * blk:(b + 1) * blk], NEG_BIG)
                mass = jnp.maximum(zb, 0.0) + jnp.log(1.0 + jnp.exp(-jnp.abs(zb)))
                mass_hi = mass.astype(BF16)
                mass_lo = (mass - mass_hi.astype(F32)).astype(BF16)
                cc = _dot(jnp.concatenate([mass_hi, mass_lo], axis=1), uu_ref[...])
                ws[b] = jnp.exp(zb - (cc[:, :blk] + mass_seen)).astype(BF16)
                mass_seen = mass_seen + cc[:, blk:]
            acc[p] += _dot(jnp.concatenate(ws, axis=1), vt)
            run[p] = mass_seen
            least = mass_seen if least is None else jnp.minimum(least, mass_seen)
        return start, jnp.min(least) > SB_UNDERFLOW

    lax.while_loop(lambda c: jnp.logical_and(c[0] > 0, jnp.logical_not(c[1])),
                   body, ((i + 1) * blk, jnp.bool_(False)))
    for p in range(pairs):
        a = acc[p]
        o_ref[:, p * LANES:(p + 1) * LANES] = jnp.where(half0, a[:blk], a[blk:])


def _sb_suffix_matrix():
    j = np.arange(2 * SB_BLOCK)[:, None] % SB_BLOCK
    s = np.arange(2 * SB_BLOCK)[None, :]
    m = np.where(s < SB_BLOCK, (j >= s), True)
    return jnp.asarray(m.astype(np.float32)).astype(BF16)


def _sb(qs, ks, vs, batch, seq):
    n = qs.shape[0]
    blk = SB_BLOCK
    nq = seq // blk
    pairs = SB_WIDTH // LANES
    uu = _sb_suffix_matrix()
    return pl.pallas_call(
        _sb_kernel,
        out_shape=jax.ShapeDtypeStruct((n, SB_WIDTH), F32),
        grid=(batch, nq),
        in_specs=[
            pl.BlockSpec((blk, SB_WIDTH), lambda b, i: (b * nq + i, 0)),
            pl.BlockSpec((seq, SB_WIDTH), lambda b, i: (b, 0)),
            pl.BlockSpec((seq, SB_WIDTH), lambda b, i: (b, 0)),
            pl.BlockSpec(uu.shape, lambda b, i: (0, 0)),
        ],
        out_specs=pl.BlockSpec((blk, SB_WIDTH), lambda b, i: (b * nq + i, 0)),
        scratch_shapes=[pltpu.VMEM((pairs, 2 * blk, LANES), BF16),
                        pltpu.VMEM((pairs, 2 * blk, LANES), F32),
                        pltpu.VMEM((pairs, 2 * blk, LANES), F32)],
        compiler_params=_cparams(("arbitrary", "arbitrary")),
        name="sb",
    )(qs, ks, vs, uu)


def _post_kernel(oa_ref, ob_ref, x_ref, ada_ref, ga_ref, gb_ref, wo_ref, g1_ref, b1_ref,
                 wrh_ref, wrl_ref, x1_ref, u2s_ref, lg_ref):
    def rms(o, g):
        return o * lax.rsqrt(jnp.mean(o * o, axis=-1, keepdims=True) + LN_EPS) * g

    na = rms(oa_ref[...], ga_ref[...]).astype(BF16)
    nb = rms(ob_ref[...], gb_ref[...]).astype(BF16)
    y = _dot(na, wo_ref[0:SWA_WIDTH, :]) + _dot(nb, wo_ref[SWA_WIDTH:, :])
    gate1 = ada_ref[0, 2:3, :]
    shift2 = ada_ref[0, 3:4, :]
    scale2 = ada_ref[0, 4:5, :]
    x1 = _layer_norm(DEEPNORM_ALPHA * x_ref[...] + (1.0 + gate1) * y) * g1_ref[...] + b1_ref[...]
    x1_ref[...] = x1
    u2 = _layer_norm(x1) * (1.0 + scale2) + shift2
    _store_slab_rows(u2s_ref, u2)
    u_hi = u2.astype(BF16)
    u_lo = (u2 - u_hi.astype(F32)).astype(BF16)
    wrh = wrh_ref[...]
    lg_ref[...] = _dot_nt(wrh, u_hi) + (_dot_nt(wrh, u_lo) + _dot_nt(wrl_ref[...], u_hi))


def _post(oa, ob, x2, ada, g_a, g_b, w_out_b, g1, b1, wr_hi, wr_lo, seq):
    n, d = x2.shape
    tm = TM_PROJ
    steps_per_seq = seq // tm
    e = wr_hi.shape[0]
    slabs = d // LANES
    const2 = lambda i: (0, 0)
    return pl.pallas_call(
        _post_kernel,
        out_shape=[jax.ShapeDtypeStruct((n, d), F32),
                   jax.ShapeDtypeStruct((n * slabs, LANES), F32),
                   jax.ShapeDtypeStruct((e, n), F32)],
        grid=(n // tm,),
        in_specs=[
            pl.BlockSpec((tm, SWA_WIDTH), lambda i: (i, 0)),
            pl.BlockSpec((tm, SB_WIDTH), lambda i: (i, 0)),
            pl.BlockSpec((tm, d), lambda i: (i, 0)),
            pl.BlockSpec((1,) + ada.shape[1:], lambda i: (i // steps_per_seq, 0, 0)),
            pl.BlockSpec(g_a.shape, const2),
            pl.BlockSpec(g_b.shape, const2),
            pl.BlockSpec(w_out_b.shape, const2),
            pl.BlockSpec(g1.shape, const2),
            pl.BlockSpec(b1.shape, const2),
            pl.BlockSpec(wr_hi.shape, const2),
            pl.BlockSpec(wr_lo.shape, const2),
        ],
        out_specs=[pl.BlockSpec((tm, d), lambda i: (i, 0)),
                   pl.BlockSpec((tm * slabs, LANES), lambda i: (i, 0)),
                   pl.BlockSpec((e, tm), lambda i: (0, i))],
        compiler_params=_cparams(("arbitrary",)),
        name="post",
    )(oa, ob, x2, ada, g_a, g_b, w_out_b, g1, b1, wr_hi, wr_lo)


def _route_kernel(lg_ref, bias_ref, su_ref, sl_ref, lpos_ref, gate_ref, seglo_ref, cnt_ref, seenb_ref,
                  seen_row):
    i = pl.program_id(0)

    @pl.when(i == 0)
    def _():
        seen_row[...] = jnp.zeros_like(seen_row)

    tt = lg_ref.shape[1]
    ninf = -jnp.inf
    scores = jax.nn.sigmoid(lg_ref[...])
    biased = scores + bias_ref[...]

    iog32 = lax.broadcasted_iota(I32, (GROUP_SIZE, tt), 0)
    groups = [biased[g * GROUP_SIZE:(g + 1) * GROUP_SIZE, :] for g in range(N_GROUPS)]
    gs_rows = []
    for blk in groups:
        m1 = jnp.max(blk, axis=0, keepdims=True)
        i1 = jnp.min(jnp.where(blk == m1, iog32, GROUP_SIZE), axis=0, keepdims=True)
        m2 = jnp.max(jnp.where(iog32 == i1, ninf, blk), axis=0, keepdims=True)
        gs_rows.append(m1 + m2)
    gs = jnp.concatenate(gs_rows, axis=0)

    iog = lax.broadcasted_iota(I32, gs.shape, 0)
    gsel = jnp.zeros(gs.shape, F32)
    cur = gs
    for _ in range(TOPK_GROUPS):
        m = jnp.max(cur, axis=0, keepdims=True)
        idx = jnp.min(jnp.where(cur == m, iog, N_GROUPS), axis=0, keepdims=True)
        hit = iog == idx
        gsel = jnp.where(hit, 1.0, gsel)
        cur = jnp.where(hit, ninf, cur)

    cand = jnp.concatenate(
        [jnp.where(gsel[g:g + 1, :] > 0.5, groups[g], ninf) for g in range(N_GROUPS)], axis=0)
    ioe = lax.broadcasted_iota(I32, cand.shape, 0)
    chosen = jnp.zeros(cand.shape, F32)
    idxs, gates = [], []
    for _ in range(TOP_K):
        m = jnp.max(cand, axis=0, keepdims=True)
        idx = jnp.min(jnp.where(cand == m, ioe, N_EXPERTS), axis=0, keepdims=True)
        hit = ioe == idx
        gates.append(jnp.sum(jnp.where(hit, scores, 0.0), axis=0, keepdims=True))
        cand = jnp.where(hit, ninf, cand)
        chosen = jnp.where(hit, 1.0, chosen)
        idxs.append(idx)
    gsum = gates[0]
    for g in gates[1:]:
        gsum = gsum + g
    gates = [g / gsum * ROUTED_SCALE for g in gates]

    def hi_lo(x):
        hi = x.astype(BF16)
        return hi, (x - hi.astype(F32)).astype(BF16)

    chosen_b = chosen.astype(BF16)
    rk = _dot(chosen_b, su_ref[...])
    c_hi, c_lo = hi_lo(rk[:, tt:])
    first_col = _dot(sl_ref[...], c_hi) + _dot(sl_ref[...], c_lo)
    slot_mat = rk[:, :tt] + jnp.concatenate([first_col] * (tt // LANES), axis=1)
    slots = [jnp.sum(jnp.where(ioe == idx, slot_mat, 0.0), axis=0, keepdims=True) for idx in idxs]
    lpos_ref[...] = jnp.concatenate(slots, axis=0).astype(I32)
    gate_ref[...] = jnp.concatenate(gates, axis=0)
    cnt_row = _dot_nt(jnp.ones((8, tt), BF16), chosen_b)
    r_hi, r_lo = hi_lo(cnt_row)
    before_row = seen_row[...]
    seen_row[...] = before_row + cnt_row
    seglo_ref[...] = (_dot_nt(r_hi, sl_ref[...]) + _dot_nt(r_lo, sl_ref[...])).astype(I32)
    cnt_ref[...] = cnt_row.astype(I32)
    seenb_ref[...] = before_row.astype(I32)


def _route_prefix_matrix(tt):
    a = np.arange(tt)[:, None] < np.arange(tt)[None, :]
    m = np.concatenate([a, np.ones((tt, LANES), bool)], axis=1)
    return jnp.asarray(m.astype(np.float32)).astype(BF16)


def _route(logits_t, e_bias):
    e, n = logits_t.shape
    tt = T_MOE
    bias_b = jnp.broadcast_to(e_bias.astype(F32)[:, None], (e, tt))
    su = _route_prefix_matrix(tt)
    sl = jnp.asarray((np.arange(e)[None, :] < np.arange(e)[:, None]).astype(np.float32)).astype(BF16)
    tok = lambda i: (0, i)
    per_tile = jax.ShapeDtypeStruct((n // tt * 8, e), I32)
    per_tile_spec = pl.BlockSpec((8, e), lambda i: (i, 0))
    return pl.pallas_call(
        _route_kernel,
        out_shape=[jax.ShapeDtypeStruct((TOP_K, n), I32),
                   jax.ShapeDtypeStruct((TOP_K, n), F32),
                   per_tile, per_tile, per_tile],
        grid=(n // tt,),
        in_specs=[pl.BlockSpec((e, tt), tok),
                  pl.BlockSpec((e, tt), lambda i: (0, 0)),
                  pl.BlockSpec(su.shape, lambda i: (0, 0)),
                  pl.BlockSpec(sl.shape, lambda i: (0, 0))],
        out_specs=[pl.BlockSpec((TOP_K, tt), tok), pl.BlockSpec((TOP_K, tt), tok),
                   per_tile_spec, per_tile_spec, per_tile_spec],
        scratch_shapes=[pltpu.VMEM((8, e), F32)],
        compiler_params=_cparams(("arbitrary",)),
        name="route",
    )(logits_t, bias_b, su, sl)


def _slab(ref, row, slabs):
    return ref.at[pl.ds(pl.multiple_of(row * slabs, slabs), slabs)]


def _start_segment(e, seglo_ref, cnt_ref, seenb_ref, pstart_ref, make_copy):
    cnt = cnt_ref[0, e]
    local = seglo_ref[0, e]
    glob = pstart_ref[e] + seenb_ref[0, e]
    n_chunks = cnt // SEGMENT_CHUNK

    def chunk(j, c):
        make_copy(local + j * SEGMENT_CHUNK, glob + j * SEGMENT_CHUNK, SEGMENT_CHUNK).start()
        return c

    lax.fori_loop(0, n_chunks, chunk, 0)
    done = n_chunks * SEGMENT_CHUNK
    rows = SEGMENT_CHUNK // 2
    while rows >= 1:
        take = (cnt & rows) != 0

        @pl.when(take)
        def _(done=done, rows=rows):
            make_copy(local + done, glob + done, rows).start()

        done = done + jnp.where(take, rows, 0)
        rows //= 2


def _dispatch_kernel(pstart_ref, pvalid_ref, pend_ref, nused_ref, loff_ref, seglo_ref, cnt_ref, seenb_ref,
                     u_ref, xs_ref, obuf, zbuf, sems, sem, *, slabs):
    i = pl.program_id(0)
    last = pl.num_programs(0) - 1
    td = u_ref.shape[0] // slabs
    zrows = zbuf.shape[0] // slabs
    chunks_per_block = EXPERT_BLOCK // zrows
    n_chunks = xs_ref.shape[0] // slabs // zrows

    @pl.when(i == 0)
    def _():
        zbuf[...] = jnp.zeros_like(zbuf)

        def zero_chunk(row0):
            return pltpu.make_async_copy(
                zbuf, xs_ref.at[pl.ds(pl.multiple_of(row0 * slabs, zrows * slabs), zrows * slabs)], sem)

        def start_unused(b, c):
            zero_chunk(b * zrows).start()
            return c

        def wait_unused(b, c):
            zero_chunk(b * zrows).wait()
            return c

        first_unused = nused_ref[0] * chunks_per_block
        lax.fori_loop(first_unused, n_chunks, start_unused, 0)
        lax.fori_loop(first_unused, n_chunks, wait_unused, 0)

        def tails(e, c, *, wait):
            for g in range(chunks_per_block):
                row0 = pend_ref[e] - zrows * (g + 1)

                @pl.when(row0 + zrows > pvalid_ref[e])
                def _():
                    if wait:
                        zero_chunk(row0).wait()
                    else:
                        zero_chunk(row0).start()
            return c

        lax.fori_loop(0, N_EXPERTS, functools.partial(tails, wait=False), 0)
        lax.fori_loop(0, N_EXPERTS, functools.partial(tails, wait=True), 0)

    slot = i % 2
    buf = obuf.at[slot]
    out_sem = sems.at[slot]

    def drained(b, s):
        return pltpu.make_async_copy(obuf.at[b], xs_ref.at[pl.ds(0, obuf.shape[1])], sems.at[s])

    @pl.when(i >= 2)
    def _():
        drained(slot, slot).wait()

    def place(t, c):
        row = _slab(u_ref, t, slabs)[...]
        for k in range(TOP_K):
            buf[pl.ds(pl.multiple_of(loff_ref[t * TOP_K + k], slabs), slabs), :] = row
        return c

    lax.fori_loop(0, td, place, 0, unroll=MOE_ROW_UNROLL)

    def to_table(local, glob, rows):
        return pltpu.make_async_copy(
            buf.at[pl.ds(pl.multiple_of(local * slabs, slabs), rows * slabs)],
            xs_ref.at[pl.ds(pl.multiple_of(glob * slabs, slabs), rows * slabs)], out_sem)

    def send(e, c):
        _start_segment(e, seglo_ref, cnt_ref, seenb_ref, pstart_ref, to_table)
        return c

    lax.fori_loop(0, N_EXPERTS, send, 0)

    @pl.when(i == last)
    def _():
        drained(slot, slot).wait()

        @pl.when(last >= 1)
        def _():
            drained(1 - slot, 1 - slot).wait()


def _dispatch(pstart, pvalid, pend, nused, loff, seglo, cnt, seen_b, u2s, n_rows, slabs):
    n = u2s.shape[0] // slabs
    td = T_MOE
    smem_tok = pl.BlockSpec((TOP_K * td,), lambda i, *_: (i,), memory_space=pltpu.SMEM)
    smem_tile = pl.BlockSpec((8, N_EXPERTS), lambda i, *_: (i, 0), memory_space=pltpu.SMEM)
    return pl.pallas_call(
        functools.partial(_dispatch_kernel, slabs=slabs),
        out_shape=jax.ShapeDtypeStruct((n_rows * slabs, LANES), F32),
        grid_spec=pltpu.PrefetchScalarGridSpec(
            num_scalar_prefetch=4,
            grid=(n // td,),
            in_specs=[smem_tok, smem_tile, smem_tile, smem_tile,
                      pl.BlockSpec((td * slabs, LANES), lambda i, *_: (i, 0))],
            out_specs=pl.BlockSpec(memory_space=pl.ANY),
            scratch_shapes=[pltpu.VMEM((2, TOP_K * td * slabs, LANES), F32),
                            pltpu.VMEM((ZERO_ROWS * slabs, LANES), F32),
                            pltpu.SemaphoreType.DMA((2,)),
                            pltpu.SemaphoreType.DMA],
        ),
        compiler_params=_cparams(("arbitrary",)),
        name="dispatch",
    )(pstart, pvalid, pend, nused, loff, seglo, cnt, seen_b, u2s)


def _expert_kernel(blk_e_ref, nused_ref, first_ref, slot_ref, next_e_ref, x_ref, w1_hbm, w3_hbm, w2_hbm,
                   y_ref, wf1, wf3, wf2, w1b, w3b, w2b, sems, *, slabs):
    i = pl.program_id(0)

    def weights(e, s):
        return [pltpu.make_async_copy(src.at[e], dst.at[s], sems.at[s])
                for src, dst in ((w1_hbm, wf1), (w3_hbm, wf3), (w2_hbm, wf2))]

    @pl.when(i < nused_ref[0])
    def _():
        @pl.when(first_ref[i] == 1)
        def _():
            e = blk_e_ref[i]
            s = slot_ref[i]

            @pl.when(i == 0)
            def _():
                for c in weights(e, s):
                    c.start()

            for c in weights(e, s):
                c.wait()
            w1b[...] = wf1[s].astype(BF16)
            w3b[...] = wf3[s].astype(BF16)
            w2b[...] = wf2[s].astype(BF16)

            @pl.when(next_e_ref[i] >= 0)
            def _():
                for c in weights(next_e_ref[i], 1 - s):
                    c.start()

        x = jnp.concatenate([c.astype(BF16) for c in _load_slab_rows(x_ref, slabs)], axis=1)
        h1 = _dot(x, w1b[...])
        h3 = _dot(x, w3b[...])
        a = (_silu(h1) * h3).astype(BF16)
        _store_slab_rows(y_ref, _dot(a, w2b[...]))

    @pl.when(i >= nused_ref[0])
    def _():
        y_ref[...] = jnp.zeros_like(y_ref)


def _experts(blk_e, nused, xs, w1, w3, w2, slabs):
    n_rows = xs.shape[0] // slabs
    e, d, f = w1.shape
    nb = n_rows // EXPERT_BLOCK

    blk = jnp.arange(nb, dtype=I32)
    first = ((blk == 0) | (blk_e != jnp.roll(blk_e, 1))) & (blk < nused[0])
    slot = (jnp.cumsum(first.astype(I32)) - 1) & 1
    opens_at = jnp.where(first, blk, nb)
    next_open = jnp.concatenate([lax.cummin(opens_at, reverse=True)[1:], jnp.full((1,), nb, I32)])
    next_e = jnp.where(next_open < nb, blk_e[jnp.minimum(next_open, nb - 1)], -1).astype(I32)

    def row_blk(i, blk_e_ref, nused_ref, *_):
        return (jnp.minimum(i, nused_ref[0] - 1), 0)

    any_space = pl.BlockSpec(memory_space=pl.ANY)
    return pl.pallas_call(
        functools.partial(_expert_kernel, slabs=slabs),
        out_shape=jax.ShapeDtypeStruct((n_rows * slabs, LANES), F32),
        grid_spec=pltpu.PrefetchScalarGridSpec(
            num_scalar_prefetch=5,
            grid=(nb,),
            in_specs=[pl.BlockSpec((EXPERT_BLOCK * slabs, LANES), row_blk),
                      any_space, any_space, any_space],
            out_specs=pl.BlockSpec((EXPERT_BLOCK * slabs, LANES), lambda i, *_: (i, 0)),
            scratch_shapes=[pltpu.VMEM((2, d, f), F32), pltpu.VMEM((2, d, f), F32),
                            pltpu.VMEM((2, f, d), F32),
                            pltpu.VMEM((d, f), BF16), pltpu.VMEM((d, f), BF16),
                            pltpu.VMEM((f, d), BF16),
                            pltpu.SemaphoreType.DMA((2,))],
        ),
        compiler_params=_cparams(("arbitrary",)),
        name="experts",
    )(blk_e, nused, first.astype(I32), slot.astype(I32), next_e, xs, w1, w3, w2)


def _final_kernel(pstart_ref, loff_ref, gate_ref, seglo_ref, cnt_ref, seenb_ref, seglo_nx, cnt_nx, seenb_nx,
                  u_ref, x1_ref, ada_ref, ws1_ref, ws3_ref, ws2_ref, g2_ref, b2_ref, ys_ref, o_ref,
                  lbuf, rbuf, sem, *, slabs):
    i = pl.program_id(0)
    last = pl.num_programs(0) - 1
    tf = u_ref.shape[0] // slabs

    def from_table(local, glob, rows):
        return pltpu.make_async_copy(
            ys_ref.at[pl.ds(pl.multiple_of(glob * slabs, slabs), rows * slabs)],
            lbuf.at[pl.ds(pl.multiple_of(local * slabs, slabs), rows * slabs)], sem)

    def fetch_tile(seglo_r, cnt_r, seenb_r):
        def fetch(e, c):
            _start_segment(e, seglo_r, cnt_r, seenb_r, pstart_ref, from_table)
            return c

        lax.fori_loop(0, N_EXPERTS, fetch, 0)

    @pl.when(i == 0)
    def _():
        fetch_tile(seglo_ref, cnt_ref, seenb_ref)

    pltpu.make_async_copy(ys_ref.at[pl.ds(0, lbuf.shape[0])], lbuf, sem).wait()

    def combine(t, c):
        rows = [lbuf[pl.ds(pl.multiple_of(loff_ref[t * TOP_K + k], slabs), slabs), :] * gate_ref[t * TOP_K + k]
                for k in range(TOP_K)]
        while len(rows) > 1:
            rows = [a + b for a, b in zip(rows[0::2], rows[1::2])]
        _slab(rbuf, t, slabs)[...] = rows[0]
        return c

    lax.fori_loop(0, tf, combine, 0, unroll=MOE_ROW_UNROLL)

    @pl.when(i < last)
    def _():
        fetch_tile(seglo_nx, cnt_nx, seenb_nx)

    u = jnp.concatenate([c.astype(BF16) for c in _load_slab_rows(u_ref, slabs)], axis=1)
    shared = _dot((_silu(_dot(u, ws1_ref[...])) * _dot(u, ws3_ref[...])).astype(BF16), ws2_ref[...])
    routed = jnp.concatenate(_load_slab_rows(rbuf, slabs), axis=1)
    gate2 = ada_ref[0, 5:6, :]
    y = shared + routed
    o_ref[...] = _layer_norm(DEEPNORM_ALPHA * x1_ref[...] + (1.0 + gate2) * y) * g2_ref[...] + b2_ref[...]


def _final(pstart, loff, gate, seglo, cnt, seen_b, u2s, x1, ada, ws1b, ws3b, ws2b, g2, b2, ys,
           seq, slabs):
    n, d = x1.shape
    tf = T_MOE
    n_tiles = n // tf
    steps_per_seq = seq // tf
    smem_tok = pl.BlockSpec((TOP_K * tf,), lambda i, *_: (i,), memory_space=pltpu.SMEM)
    smem_tile = pl.BlockSpec((8, N_EXPERTS), lambda i, *_: (i, 0), memory_space=pltpu.SMEM)
    smem_next = pl.BlockSpec((8, N_EXPERTS), lambda i, *_: (jnp.minimum(i + 1, n_tiles - 1), 0),
                             memory_space=pltpu.SMEM)
    const2 = lambda i, *_: (0, 0)
    return pl.pallas_call(
        functools.partial(_final_kernel, slabs=slabs),
        out_shape=jax.ShapeDtypeStruct((n, d), F32),
        grid_spec=pltpu.PrefetchScalarGridSpec(
            num_scalar_prefetch=1,
            grid=(n_tiles,),
            in_specs=[smem_tok, smem_tok, smem_tile, smem_tile, smem_tile, smem_next, smem_next, smem_next,
                      pl.BlockSpec((tf * slabs, LANES), lambda i, *_: (i, 0)),
                      pl.BlockSpec((tf, d), lambda i, *_: (i, 0)),
                      pl.BlockSpec((1,) + ada.shape[1:], lambda i, *_: (i // steps_per_seq, 0, 0)),
                      pl.BlockSpec(ws1b.shape, const2),
                      pl.BlockSpec(ws3b.shape, const2),
                      pl.BlockSpec(ws2b.shape, const2),
                      pl.BlockSpec(g2.shape, const2),
                      pl.BlockSpec(b2.shape, const2),
                      pl.BlockSpec(memory_space=pl.ANY)],
            out_specs=pl.BlockSpec((tf, d), lambda i, *_: (i, 0)),
            scratch_shapes=[pltpu.VMEM((TOP_K * tf * slabs, LANES), F32),
                            pltpu.VMEM((tf * slabs, LANES), F32),
                            pltpu.SemaphoreType.DMA],
        ),
        compiler_params=_cparams(("arbitrary",)),
        name="final",
    )(pstart, loff, gate, seglo, cnt, seen_b, seglo, cnt, seen_b, u2s, x1, ada, ws1b, ws3b, ws2b,
      g2, b2, ys)


def _swa_head_permutation():
    per_group = SWA_HEADS // SWA_KV_HEADS
    cols = []
    for c in range(per_group):
        for g in range(SWA_KV_HEADS):
            h = c + per_group * g
            cols.extend(range(h * HEAD_DIM, (h + 1) * HEAD_DIM))
    return np.asarray(cols)


def _layer(x, c, positions, w_ada, b_ada, w_in, sinks, g_swa, g_sb, w_out, ln1_g, ln1_b,
           w_router, e_bias, w1, w3, w2, ws1, ws3, ws2, ln2_g, ln2_b):
    batch, seq, d = x.shape
    n = batch * seq
    x2 = x.reshape(n, d)
    perm = _swa_head_permutation()
    per_group = SWA_HEADS // SWA_KV_HEADS

    ada = _ada(c, w_ada, b_ada)

    o_q = SWA_WIDTH + 2 * SWA_KV_WIDTH
    w_in_b = jnp.concatenate(
        [w_in[:, :SWA_WIDTH][:, perm], w_in[:, SWA_WIDTH:o_q],
         w_in[:, o_q:o_q + SB_WIDTH] * (HEAD_DIM ** -0.5), w_in[:, o_q + SB_WIDTH:]], axis=1).astype(BF16)
    half = HEAD_DIM // 2
    inv_freq = ROPE_THETA ** (-jnp.arange(half, dtype=F32) * 2.0 / HEAD_DIM)
    invf = jnp.tile(inv_freq, LANES // half).reshape(1, LANES)
    pos_col = positions.reshape(n, 1).astype(F32)
    qa, ka, va, qs, ks, vs = _inproj(x2, ada, pos_col, invf, w_in_b, seq)

    head_of_block = np.asarray([c_ + per_group * g for g in range(SWA_KV_HEADS) for c_ in range(per_group)])
    sink_col = jnp.repeat(sinks.astype(F32)[head_of_block], WINDOW).reshape(SWA_HEADS * WINDOW, 1)
    oa = _swa(qa, ka, va, sink_col, batch, seq)
    ob = _sb(qs, ks, vs, batch, seq)

    w_out_b = jnp.concatenate([w_out[:SWA_WIDTH][perm], w_out[SWA_WIDTH:]], axis=0).astype(BF16)
    wr_t = w_router.T.astype(F32)
    wr_hi = wr_t.astype(BF16)
    wr_lo = (wr_t - wr_hi.astype(F32)).astype(BF16)
    x1, u2s, logits_t = _post(oa, ob, x2, ada, g_swa[perm].reshape(1, -1), g_sb.reshape(1, -1), w_out_b,
                              ln1_g.reshape(1, d), ln1_b.reshape(1, d), wr_hi, wr_lo, seq)

    lpos_t, gate_t, seglo, cnt, seen_b = _route(logits_t, e_bias)

    counts = seen_b[-1] + cnt[-1]
    padded = (counts + EXPERT_BLOCK - 1) // EXPERT_BLOCK * EXPERT_BLOCK
    pend = jnp.cumsum(padded).astype(I32)
    pstart = pend - padded
    n_rows = n * TOP_K + N_EXPERTS * EXPERT_BLOCK
    n_blocks = n_rows // EXPERT_BLOCK
    block_row0 = jnp.arange(n_blocks, dtype=I32) * EXPERT_BLOCK
    blk_e = jnp.minimum(jnp.sum((pend[None, :] <= block_row0[:, None]).astype(I32), axis=1), N_EXPERTS - 1)
    nused = (pend[-1:] // EXPERT_BLOCK).astype(I32)
    slabs = d // LANES
    loff = (lpos_t * slabs).T.reshape(-1)
    gate = gate_t.T.reshape(-1)

    xs = _dispatch(pstart, pstart + counts, pend, nused, loff, seglo, cnt, seen_b, u2s, n_rows, slabs)
    ys = _experts(blk_e, nused, xs, w1, w3, w2, slabs)
    out = _final(pstart, loff, gate, seglo, cnt, seen_b, u2s, x1, ada, ws1.astype(BF16),
                 ws3.astype(BF16), ws2.astype(BF16), ln2_g.reshape(1, d), ln2_b.reshape(1, d), ys, seq, slabs)
    return out.reshape(batch, seq, d)


def kernel(x, c, positions, w_ada, b_ada, w_in, attn_sinks, g_swa, g_sb, w_out, ln1_g, ln1_b,
           w_router, e_bias, w1, w3, w2, ws1, ws3, ws2, ln2_g, ln2_b):
    assert w_ada.shape[0] == DEPTH
    for l in range(DEPTH):
        x = _layer(x, c, positions, w_ada[l], b_ada[l], w_in[l], attn_sinks[l], g_swa[l], g_sb[l],
                   w_out[l], ln1_g[l], ln1_b[l], w_router[l], e_bias[l], w1[l], w3[l], w2[l],
                   ws1[l], ws3[l], ws2[l], ln2_g[l], ln2_b[l])
    return x
```

```python
import functools

import numpy as np
import jax
import jax.numpy as jnp
from jax import lax
from jax.experimental import pallas as pl
from jax.experimental.pallas import tpu as pltpu

F32 = jnp.float32
BF16 = jnp.bfloat16
I32 = jnp.int32

HEAD_DIM = 64
SWA_HEADS = 8
SWA_KV_HEADS = 2
SB_HEADS = 8
SWA_WIDTH = SWA_HEADS * HEAD_DIM
SWA_KV_WIDTH = SWA_KV_HEADS * HEAD_DIM
SB_WIDTH = SB_HEADS * HEAD_DIM
WINDOW = 128
ROPE_THETA = 10000.0
N_EXPERTS = 256
TOP_K = 8
N_GROUPS = 8
TOPK_GROUPS = 4
GROUP_SIZE = N_EXPERTS // N_GROUPS
ROUTED_SCALE = 2.5
LN_EPS = 1e-5
DEPTH = 1
DEEPNORM_ALPHA = (2 * DEPTH) ** 0.25

LANES = 128
SB_BLOCK = 128
SB_TILE_BLOCKS = 3
SB_UNDERFLOW = 110.0
NEG_BIG = -1e30
VMEM_LIMIT = 56 * 1024 * 1024

TM_PROJ = 512
TQ_SWA = 512
T_MOE = 512
EXPERT_BLOCK = 512
ZERO_ROWS = 128
SEGMENT_CHUNK = 8
MOE_ROW_UNROLL = 4


def _cparams(sem, vmem=VMEM_LIMIT):
    return pltpu.CompilerParams(dimension_semantics=sem, vmem_limit_bytes=vmem)


def _layer_norm(x):
    mu = jnp.mean(x, axis=-1, keepdims=True)
    xc = x - mu
    var = jnp.mean(xc * xc, axis=-1, keepdims=True)
    return xc * lax.rsqrt(var + LN_EPS)


def _silu(x):
    return x * jax.nn.sigmoid(x)


def _dot(a, b):
    return jnp.dot(a, b, preferred_element_type=F32)


def _dot_nt(a, b):
    return lax.dot_general(a, b, (((1,), (1,)), ((), ())), preferred_element_type=F32)


def _store_slab_rows(ref, x):
    t, d = x.shape
    slabs = d // LANES
    for s in range(slabs):
        ref[pl.ds(s, t, stride=slabs), :] = x[:, s * LANES:(s + 1) * LANES]


def _load_slab_rows(ref, slabs):
    t = ref.shape[0] // slabs
    return [ref[pl.ds(s, t, stride=slabs), :] for s in range(slabs)]


def _ada_kernel(c_ref, w_ref, b_ref, o_ref):
    sc = _silu(c_ref[...])
    o_ref[...] = _dot(sc.astype(BF16), w_ref[...].astype(BF16)) + b_ref[...]


def _ada(c, w_ada, b_ada):
    b, d = c.shape
    n_out = w_ada.shape[1]
    rows = 8
    c_pad = jnp.zeros((rows, d), F32).at[:b].set(c)
    out = pl.pallas_call(
        _ada_kernel,
        out_shape=jax.ShapeDtypeStruct((rows, n_out), F32),
        grid=(n_out // d,),
        in_specs=[
            pl.BlockSpec((rows, d), lambda j: (0, 0)),
            pl.BlockSpec((d, d), lambda j: (0, j)),
            pl.BlockSpec((1, d), lambda j: (0, j)),
        ],
        out_specs=pl.BlockSpec((rows, d), lambda j: (0, j)),
        compiler_params=_cparams(("arbitrary",)),
        name="ada",
    )(c_pad, w_ada, b_ada.reshape(1, n_out))
    return out[:b].reshape(b, n_out // d, d)


def _inproj_kernel(x_ref, ada_ref, pos_ref, invf_ref, w_ref,
                   qa_ref, ka_ref, va_ref, qs_ref, ks_ref, vs_ref):
    x = x_ref[...]
    shift = ada_ref[0, 0:1, :]
    scale = ada_ref[0, 1:2, :]
    u = _layer_norm(x) * (1.0 + scale) + shift
    h = _dot(u.astype(BF16), w_ref[...])

    ang = pos_ref[...] * invf_ref[...]
    cs = jnp.cos(ang)
    sn = jnp.sin(ang)
    lane = lax.broadcasted_iota(I32, (1, LANES), 1)
    first = (lane & (HEAD_DIM // 2)) == 0
    sn_signed = jnp.where(first, -sn, sn)

    def rope(hc):
        partner = jnp.where(first, pltpu.roll(hc, LANES - HEAD_DIM // 2, 1),
                            pltpu.roll(hc, HEAD_DIM // 2, 1))
        return hc * cs + partner * sn_signed

    q_scale = HEAD_DIM ** -0.5
    o = 0
    for c in range(SWA_WIDTH // LANES):
        qa_ref[:, c * LANES:(c + 1) * LANES] = (rope(h[:, o:o + LANES]) * q_scale).astype(BF16)
        o += LANES
    ka_ref[...] = rope(h[:, o:o + SWA_KV_WIDTH]).astype(BF16)
    o += SWA_KV_WIDTH
    va_ref[...] = h[:, o:o + SWA_KV_WIDTH].astype(BF16)
    o += SWA_KV_WIDTH
    qs_ref[...] = h[:, o:o + SB_WIDTH].astype(BF16)
    o += SB_WIDTH
    ks_ref[...] = h[:, o:o + SB_WIDTH].astype(BF16)
    o += SB_WIDTH
    vs_ref[...] = h[:, o:o + SB_WIDTH].astype(BF16)


def _inproj(x2, ada, pos_col, invf, w_in_b, seq):
    n, d = x2.shape
    tm = TM_PROJ
    steps_per_seq = seq // tm
    widths = (SWA_WIDTH, SWA_KV_WIDTH, SWA_KV_WIDTH, SB_WIDTH, SB_WIDTH, SB_WIDTH)
    return pl.pallas_call(
        _inproj_kernel,
        out_shape=[jax.ShapeDtypeStruct((n, w), BF16) for w in widths],
        grid=(n // tm,),
        in_specs=[
            pl.BlockSpec((tm, d), lambda i: (i, 0)),
            pl.BlockSpec((1,) + ada.shape[1:], lambda i: (i // steps_per_seq, 0, 0)),
            pl.BlockSpec((tm, 1), lambda i: (i, 0)),
            pl.BlockSpec((1, LANES), lambda i: (0, 0)),
            pl.BlockSpec(w_in_b.shape, lambda i: (0, 0)),
        ],
        out_specs=[pl.BlockSpec((tm, w), lambda i: (i, 0)) for w in widths],
        compiler_params=_cparams(("arbitrary",)),
        name="inproj",
    )(x2, ada, pos_col, invf, w_in_b)


def _swa_kernel(q_ref, kc_ref, kp_ref, vc_ref, vp_ref, bias0_ref, bias_ref, sink_ref,
                o_ref, kall, vall):
    tq = q_ref.shape[0]
    kall[0:WINDOW, :] = kp_ref[...]
    kall[WINDOW:, :] = kc_ref[...]
    vall[0:WINDOW, :] = vp_ref[...]
    vall[WINDOW:, :] = vc_ref[...]
    lane = lax.broadcasted_iota(I32, (1, LANES), 1)
    half0 = lane < HEAD_DIM
    half1 = lane >= HEAD_DIM
    sink = sink_ref[...]
    n_col = SWA_WIDTH // LANES
    for j in range(tq // WINDOW):
        r0 = j * WINDOW
        q = q_ref[r0:r0 + WINDOW, :]
        parts = []
        for half in (half0, half1):
            for c in range(n_col):
                parts.append(jnp.where(half, q[:, c * LANES:(c + 1) * LANES], jnp.zeros((), BF16)))
        qm = jnp.concatenate(parts, axis=0)
        kb = kall[r0:r0 + 2 * WINDOW, :]
        vb = vall[r0:r0 + 2 * WINDOW, :]
        s = _dot_nt(qm, kb)
        s = s + (bias0_ref[0] if j == 0 else bias_ref[...])
        m = jnp.maximum(jnp.max(s, axis=1, keepdims=True), sink)
        p = jnp.exp(s - m)
        den = jnp.sum(p, axis=1, keepdims=True) + jnp.exp(sink - m)
        o = _dot(p.astype(BF16), vb) / den
        for c in range(n_col):
            lo = o[c * WINDOW:(c + 1) * WINDOW]
            hi = o[(n_col + c) * WINDOW:(n_col + c + 1) * WINDOW]
            o_ref[r0:r0 + WINDOW, c * LANES:(c + 1) * LANES] = jnp.where(half0, lo, hi)


def _swa_bias():
    qi = np.arange(SWA_HEADS * WINDOW)[:, None] % WINDOW
    kj = np.arange(2 * WINDOW)[None, :]
    dist = qi + WINDOW - kj
    band = (dist >= 0) & (dist < WINDOW)
    bias = np.where(band, 0.0, NEG_BIG).astype(np.float32)
    first = np.where(band & (kj >= WINDOW), 0.0, NEG_BIG).astype(np.float32)
    return jnp.asarray(np.stack([bias, first])), jnp.asarray(bias)


def _swa(qa, ka, va, sink_col, batch, seq):
    n = qa.shape[0]
    tq = TQ_SWA
    nt = seq // tq
    blocks_per_tile = tq // WINDOW
    bias_pair, bias = _swa_bias()

    def cur(b, i):
        return (b * nt + i, 0)

    def prev(b, i):
        return (jnp.maximum((b * nt + i) * blocks_per_tile - 1, 0), 0)

    return pl.pallas_call(
        _swa_kernel,
        out_shape=jax.ShapeDtypeStruct((n, SWA_WIDTH), F32),
        grid=(batch, nt),
        in_specs=[
            pl.BlockSpec((tq, SWA_WIDTH), cur),
            pl.BlockSpec((tq, SWA_KV_WIDTH), cur),
            pl.BlockSpec((WINDOW, SWA_KV_WIDTH), prev),
            pl.BlockSpec((tq, SWA_KV_WIDTH), cur),
            pl.BlockSpec((WINDOW, SWA_KV_WIDTH), prev),
            pl.BlockSpec((1,) + bias.shape, lambda b, i: (jnp.where(i == 0, 1, 0), 0, 0)),
            pl.BlockSpec(bias.shape, lambda b, i: (0, 0)),
            pl.BlockSpec(sink_col.shape, lambda b, i: (0, 0)),
        ],
        out_specs=pl.BlockSpec((tq, SWA_WIDTH), cur),
        scratch_shapes=[pltpu.VMEM((tq + WINDOW, SWA_KV_WIDTH), BF16),
                        pltpu.VMEM((tq + WINDOW, SWA_KV_WIDTH), BF16)],
        compiler_params=_cparams(("arbitrary", "arbitrary")),
        name="swa",
    )(qa, ka, ka, va, va, bias_pair, bias, sink_col)


def _sb_kernel(q_ref, k_ref, v_ref, uu_ref, o_ref, qm, acc, run):
    i = pl.program_id(1)
    blk = SB_BLOCK
    n_sub = SB_TILE_BLOCKS
    tile = n_sub * blk
    pairs = q_ref.shape[1] // LANES
    lane = lax.broadcasted_iota(I32, (1, LANES), 1)
    half0 = lane < HEAD_DIM
    half1 = lane >= HEAD_DIM
    zero = jnp.zeros((), BF16)
    for p in range(pairs):
        q = q_ref[:, p * LANES:(p + 1) * LANES]
        qm[p] = jnp.concatenate([jnp.where(half0, q, zero), jnp.where(half1, q, zero)], axis=0)
    acc[...] = jnp.zeros_like(acc)
    run[...] = jnp.zeros_like(run)
    q_pos = i * blk + (lax.broadcasted_iota(I32, (2 * blk, blk), 0) & (blk - 1))
    col = lax.broadcasted_iota(I32, (2 * blk, blk), 1)

    def body(carry):
        end, _ = carry
        start = pl.multiple_of(jnp.maximum(end - tile, 0), blk)
        limit = jnp.minimum(end, q_pos) - start
        least = None
        for p in range(pairs):
            kt = k_ref[pl.ds(start, tile), p * LANES:(p + 1) * LANES]
            vt = v_ref[pl.ds(start, tile), p * LANES:(p + 1) * LANES]
            z = _dot_nt(qm[p], kt)
            mass_seen = run[p]
            ws = [None] * n_sub
            for b in reversed(range(n_sub)):
                zb = jnp.where(col < limit - b * blk, z[:, b * blk:(b + 1) * blk], NEG_BIG)
                mass = jnp.maximum(zb, 0.0) + jnp.log(1.0 + jnp.exp(-jnp.abs(zb)))
                mass_hi = mass.astype(BF16)
                mass_lo = (mass - mass_hi.astype(F32)).astype(BF16)
                cc = _dot(jnp.concatenate([mass_hi, mass_lo], axis=1), uu_ref[...])
                ws[b] = jnp.exp(zb - (cc[:, :blk] + mass_seen)).astype(BF16)
                mass_seen = mass_seen + cc[:, blk:]
            acc[p] += _dot(jnp.concatenate(ws, axis=1), vt)
            run[p] = mass_seen
            least = mass_seen if least is None else jnp.minimum(least, mass_seen)
        return start, jnp.min(least) > SB_UNDERFLOW

    lax.while_loop(lambda c: jnp.logical_and(c[0] > 0, jnp.logical_not(c[1])),
                   body, ((i + 1) * blk, jnp.bool_(False)))
    for p in range(pairs):
        a = acc[p]
        o_ref[:, p * LANES:(p + 1) * LANES] = jnp.where(half0, a[:blk], a[blk:])


def _sb_suffix_matrix():
    j = np.arange(2 * SB_BLOCK)[:, None] % SB_BLOCK
    s = np.arange(2 * SB_BLOCK)[None, :]
    m = np.where(s < SB_BLOCK, (j >= s), True)
    return jnp.asarray(m.astype(np.float32)).astype(BF16)


def _sb(qs, ks, vs, batch, seq):
    n = qs.shape[0]
    blk = SB_BLOCK
    nq = seq // blk
    pairs = SB_WIDTH // LANES
    uu = _sb_suffix_matrix()
    return pl.pallas_call(
        _sb_kernel,
        out_shape=jax.ShapeDtypeStruct((n, SB_WIDTH), F32),
        grid=(batch, nq),
        in_specs=[
            pl.BlockSpec((blk, SB_WIDTH), lambda b, i: (b * nq + i, 0)),
            pl.BlockSpec((seq, SB_WIDTH), lambda b, i: (b, 0)),
            pl.BlockSpec((seq, SB_WIDTH), lambda b, i: (b, 0)),
            pl.BlockSpec(uu.shape, lambda b, i: (0, 0)),
        ],
        out_specs=pl.BlockSpec((blk, SB_WIDTH), lambda b, i: (b * nq + i, 0)),
        scratch_shapes=[pltpu.VMEM((pairs, 2 * blk, LANES), BF16),
                        pltpu.VMEM((pairs, 2 * blk, LANES), F32),
                        pltpu.VMEM((pairs, 2 * blk, LANES), F32)],
        compiler_params=_cparams(("arbitrary", "arbitrary")),
        name="sb",
    )(qs, ks, vs, uu)


def _post_kernel(oa_ref, ob_ref, x_ref, ada_ref, ga_ref, gb_ref, wo_ref, g1_ref, b1_ref,
                 wrh_ref, wrl_ref, x1_ref, u2s_ref, lg_ref):
    def rms(o, g):
        return o * lax.rsqrt(jnp.mean(o * o, axis=-1, keepdims=True) + LN_EPS) * g

    na = rms(oa_ref[...], ga_ref[...]).astype(BF16)
    nb = rms(ob_ref[...], gb_ref[...]).astype(BF16)
    y = _dot(na, wo_ref[0:SWA_WIDTH, :]) + _dot(nb, wo_ref[SWA_WIDTH:, :])
    gate1 = ada_ref[0, 2:3, :]
    shift2 = ada_ref[0, 3:4, :]
    scale2 = ada_ref[0, 4:5, :]
    x1 = _layer_norm(DEEPNORM_ALPHA * x_ref[...] + (1.0 + gate1) * y) * g1_ref[...] + b1_ref[...]
    x1_ref[...] = x1
    u2 = _layer_norm(x1) * (1.0 + scale2) + shift2
    _store_slab_rows(u2s_ref, u2)
    u_hi = u2.astype(BF16)
    u_lo = (u2 - u_hi.astype(F32)).astype(BF16)
    wrh = wrh_ref[...]
    lg_ref[...] = _dot_nt(wrh, u_hi) + (_dot_nt(wrh, u_lo) + _dot_nt(wrl_ref[...], u_hi))


def _post(oa, ob, x2, ada, g_a, g_b, w_out_b, g1, b1, wr_hi, wr_lo, seq):
    n, d = x2.shape
    tm = TM_PROJ
    steps_per_seq = seq // tm
    e = wr_hi.shape[0]
    slabs = d // LANES
    const2 = lambda i: (0, 0)
    return pl.pallas_call(
        _post_kernel,
        out_shape=[jax.ShapeDtypeStruct((n, d), F32),
                   jax.ShapeDtypeStruct((n * slabs, LANES), F32),
                   jax.ShapeDtypeStruct((e, n), F32)],
        grid=(n // tm,),
        in_specs=[
            pl.BlockSpec((tm, SWA_WIDTH), lambda i: (i, 0)),
            pl.BlockSpec((tm, SB_WIDTH), lambda i: (i, 0)),
            pl.BlockSpec((tm, d), lambda i: (i, 0)),
            pl.BlockSpec((1,) + ada.shape[1:], lambda i: (i // steps_per_seq, 0, 0)),
            pl.BlockSpec(g_a.shape, const2),
            pl.BlockSpec(g_b.shape, const2),
            pl.BlockSpec(w_out_b.shape, const2),
            pl.BlockSpec(g1.shape, const2),
            pl.BlockSpec(b1.shape, const2),
            pl.BlockSpec(wr_hi.shape, const2),
            pl.BlockSpec(wr_lo.shape, const2),
        ],
        out_specs=[pl.BlockSpec((tm, d), lambda i: (i, 0)),
                   pl.BlockSpec((tm * slabs, LANES), lambda i: (i, 0)),
                   pl.BlockSpec((e, tm), lambda i: (0, i))],
        compiler_params=_cparams(("arbitrary",)),
        name="post",
    )(oa, ob, x2, ada, g_a, g_b, w_out_b, g1, b1, wr_hi, wr_lo)


def _route_kernel(lg_ref, bias_ref, su_ref, sl_ref, lpos_ref, gate_ref, seglo_ref, cnt_ref, seenb_ref,
                  seen_row):
    i = pl.program_id(0)

    @pl.when(i == 0)
    def _():
        seen_row[...] = jnp.zeros_like(seen_row)

    tt = lg_ref.shape[1]
    ninf = -jnp.inf
    scores = jax.nn.sigmoid(lg_ref[...])
    biased = scores + bias_ref[...]

    iog32 = lax.broadcasted_iota(I32, (GROUP_SIZE, tt), 0)
    groups = [biased[g * GROUP_SIZE:(g + 1) * GROUP_SIZE, :] for g in range(N_GROUPS)]
    gs_rows = []
    for blk in groups:
        m1 = jnp.max(blk, axis=0, keepdims=True)
        i1 = jnp.min(jnp.where(blk == m1, iog32, GROUP_SIZE), axis=0, keepdims=True)
        m2 = jnp.max(jnp.where(iog32 == i1, ninf, blk), axis=0, keepdims=True)
        gs_rows.append(m1 + m2)
    gs = jnp.concatenate(gs_rows, axis=0)

    iog = lax.broadcasted_iota(I32, gs.shape, 0)
    gsel = jnp.zeros(gs.shape, F32)
    cur = gs
    for _ in range(TOPK_GROUPS):
        m = jnp.max(cur, axis=0, keepdims=True)
        idx = jnp.min(jnp.where(cur == m, iog, N_GROUPS), axis=0, keepdims=True)
        hit = iog == idx
        gsel = jnp.where(hit, 1.0, gsel)
        cur = jnp.where(hit, ninf, cur)

    cand = jnp.concatenate(
        [jnp.where(gsel[g:g + 1, :] > 0.5, groups[g], ninf) for g in range(N_GROUPS)], axis=0)
    ioe = lax.broadcasted_iota(I32, cand.shape, 0)
    chosen = jnp.zeros(cand.shape, F32)
    idxs, gates = [], []
    for _ in range(TOP_K):
        m = jnp.max(cand, axis=0, keepdims=True)
        idx = jnp.min(jnp.where(cand == m, ioe, N_EXPERTS), axis=0, keepdims=True)
        hit = ioe == idx
        gates.append(jnp.sum(jnp.where(hit, scores, 0.0), axis=0, keepdims=True))
        cand = jnp.where(hit, ninf, cand)
        chosen = jnp.where(hit, 1.0, chosen)
        idxs.append(idx)
    gsum = gates[0]
    for g in gates[1:]:
        gsum = gsum + g
    gates = [g / gsum * ROUTED_SCALE for g in gates]

    def hi_lo(x):
        hi = x.astype(BF16)
        return hi, (x - hi.astype(F32)).astype(BF16)

    chosen_b = chosen.astype(BF16)
    rk = _dot(chosen_b, su_ref[...])
    c_hi, c_lo = hi_lo(rk[:, tt:])
    first_col = _dot(sl_ref[...], c_hi) + _dot(sl_ref[...], c_lo)
    slot_mat = rk[:, :tt] + jnp.concatenate([first_col] * (tt // LANES), axis=1)
    slots = [jnp.sum(jnp.where(ioe == idx, slot_mat, 0.0), axis=0, keepdims=True) for idx in idxs]
    lpos_ref[...] = jnp.concatenate(slots, axis=0).astype(I32)
    gate_ref[...] = jnp.concatenate(gates, axis=0)
    cnt_row = _dot_nt(jnp.ones((8, tt), BF16), chosen_b)
    r_hi, r_lo = hi_lo(cnt_row)
    before_row = seen_row[...]
    seen_row[...] = before_row + cnt_row
    seglo_ref[...] = (_dot_nt(r_hi, sl_ref[...]) + _dot_nt(r_lo, sl_ref[...])).astype(I32)
    cnt_ref[...] = cnt_row.astype(I32)
    seenb_ref[...] = before_row.astype(I32)


def _route_prefix_matrix(tt):
    a = np.arange(tt)[:, None] < np.arange(tt)[None, :]
    m = np.concatenate([a, np.ones((tt, LANES), bool)], axis=1)
    return jnp.asarray(m.astype(np.float32)).astype(BF16)


def _route(logits_t, e_bias):
    e, n = logits_t.shape
    tt = T_MOE
    bias_b = jnp.broadcast_to(e_bias.astype(F32)[:, None], (e, tt))
    su = _route_prefix_matrix(tt)
    sl = jnp.asarray((np.arange(e)[None, :] < np.arange(e)[:, None]).astype(np.float32)).astype(BF16)
    tok = lambda i: (0, i)
    per_tile = jax.ShapeDtypeStruct((n // tt * 8, e), I32)
    per_tile_spec = pl.BlockSpec((8, e), lambda i: (i, 0))
    return pl.pallas_call(
        _route_kernel,
        out_shape=[jax.ShapeDtypeStruct((TOP_K, n), I32),
                   jax.ShapeDtypeStruct((TOP_K, n), F32),
                   per_tile, per_tile, per_tile],
        grid=(n // tt,),
        in_specs=[pl.BlockSpec((e, tt), tok),
                  pl.BlockSpec((e, tt), lambda i: (0, 0)),
                  pl.BlockSpec(su.shape, lambda i: (0, 0)),
                  pl.BlockSpec(sl.shape, lambda i: (0, 0))],
        out_specs=[pl.BlockSpec((TOP_K, tt), tok), pl.BlockSpec((TOP_K, tt), tok),
                   per_tile_spec, per_tile_spec, per_tile_spec],
        scratch_shapes=[pltpu.VMEM((8, e), F32)],
        compiler_params=_cparams(("arbitrary",)),
        name="route",
    )(logits_t, bias_b, su, sl)


def _slab(ref, row, slabs):
    return ref.at[pl.ds(pl.multiple_of(row * slabs, slabs), slabs)]


def _start_segment(e, local_ref, table_ref, cnt_ref, make_copy, slabs):
    cnt = cnt_ref[e]
    local = local_ref[e]
    table = table_ref[e]
    n_chunks = lax.shift_right_logical(cnt, SEGMENT_CHUNK.bit_length() - 1)
    step = SEGMENT_CHUNK * slabs

    def chunk(j, c):
        make_copy(local + j * step, table + j * step, SEGMENT_CHUNK).start()
        return c

    lax.fori_loop(0, n_chunks, chunk, 0)
    done = n_chunks * step
    rows = SEGMENT_CHUNK // 2
    while rows >= 1:
        take = (cnt & rows) != 0

        @pl.when(take)
        def _(done=done, rows=rows):
            make_copy(local + done, table + done, rows).start()

        done = done + jnp.where(take, rows * slabs, 0)
        rows //= 2


def _dispatch_kernel(pvalid_ref, pend_ref, nused_ref, loff_ref, seg_local_ref, seg_table_ref, seg_cnt_ref,
                     u_ref, xs_ref, obuf, zbuf, sems, sem, *, slabs):
    i = pl.program_id(0)
    last = pl.num_programs(0) - 1
    td = u_ref.shape[0] // slabs
    zrows = zbuf.shape[0] // slabs
    chunks_per_block = EXPERT_BLOCK // zrows
    n_chunks = xs_ref.shape[0] // slabs // zrows

    @pl.when(i == 0)
    def _():
        zbuf[...] = jnp.zeros_like(zbuf)

        def zero_chunk(row0):
            return pltpu.make_async_copy(
                zbuf, xs_ref.at[pl.ds(pl.multiple_of(row0 * slabs, zrows * slabs), zrows * slabs)], sem)

        def start_unused(b, c):
            zero_chunk(b * zrows).start()
            return c

        def wait_unused(b, c):
            zero_chunk(b * zrows).wait()
            return c

        first_unused = nused_ref[0] * chunks_per_block
        lax.fori_loop(first_unused, n_chunks, start_unused, 0)
        lax.fori_loop(first_unused, n_chunks, wait_unused, 0)

        def tails(e, c, *, wait):
            for g in range(chunks_per_block):
                row0 = pend_ref[e] - zrows * (g + 1)

                @pl.when(row0 + zrows > pvalid_ref[e])
                def _():
                    if wait:
                        zero_chunk(row0).wait()
                    else:
                        zero_chunk(row0).start()
            return c

        lax.fori_loop(0, N_EXPERTS, functools.partial(tails, wait=False), 0)
        lax.fori_loop(0, N_EXPERTS, functools.partial(tails, wait=True), 0)

    slot = i % 2
    buf = obuf.at[slot]
    out_sem = sems.at[slot]

    def drained(b, s):
        return pltpu.make_async_copy(obuf.at[b], xs_ref.at[pl.ds(0, obuf.shape[1])], sems.at[s])

    @pl.when(i >= 2)
    def _():
        drained(slot, slot).wait()

    def place(t, c):
        row = _slab(u_ref, t, slabs)[...]
        for k in range(TOP_K):
            buf[pl.ds(pl.multiple_of(loff_ref[t * TOP_K + k], slabs), slabs), :] = row
        return c

    lax.fori_loop(0, td, place, 0, unroll=MOE_ROW_UNROLL)

    def to_table(local, table, rows):
        return pltpu.make_async_copy(
            buf.at[pl.ds(pl.multiple_of(local, slabs), rows * slabs)],
            xs_ref.at[pl.ds(pl.multiple_of(table, slabs), rows * slabs)], out_sem)

    def send(e, c):
        _start_segment(e, seg_local_ref, seg_table_ref, seg_cnt_ref, to_table, slabs)
        return c

    lax.fori_loop(0, N_EXPERTS, send, 0)

    @pl.when(i == last)
    def _():
        drained(slot, slot).wait()

        @pl.when(last >= 1)
        def _():
            drained(1 - slot, 1 - slot).wait()


def _dispatch(pvalid, pend, nused, loff, seg_local, seg_table, seg_cnt, u2s, n_rows, slabs):
    n = u2s.shape[0] // slabs
    td = T_MOE
    smem_tok = pl.BlockSpec((TOP_K * td,), lambda i, *_: (i,), memory_space=pltpu.SMEM)
    smem_tile = pl.BlockSpec((N_EXPERTS,), lambda i, *_: (i,), memory_space=pltpu.SMEM)
    return pl.pallas_call(
        functools.partial(_dispatch_kernel, slabs=slabs),
        out_shape=jax.ShapeDtypeStruct((n_rows * slabs, LANES), F32),
        grid_spec=pltpu.PrefetchScalarGridSpec(
            num_scalar_prefetch=3,
            grid=(n // td,),
            in_specs=[smem_tok, smem_tile, smem_tile, smem_tile,
                      pl.BlockSpec((td * slabs, LANES), lambda i, *_: (i, 0))],
            out_specs=pl.BlockSpec(memory_space=pl.ANY),
            scratch_shapes=[pltpu.VMEM((2, TOP_K * td * slabs, LANES), F32),
                            pltpu.VMEM((ZERO_ROWS * slabs, LANES), F32),
                            pltpu.SemaphoreType.DMA((2,)),
                            pltpu.SemaphoreType.DMA],
        ),
        compiler_params=_cparams(("arbitrary",)),
        name="dispatch",
    )(pvalid, pend, nused, loff, seg_local, seg_table, seg_cnt, u2s)


def _expert_kernel(blk_e_ref, nused_ref, first_ref, slot_ref, next_e_ref, x_ref, w1_hbm, w3_hbm, w2_hbm,
                   y_ref, wf1, wf3, wf2, w1b, w3b, w2b, sems, *, slabs):
    i = pl.program_id(0)

    def weights(e, s):
        return [pltpu.make_async_copy(src.at[e], dst.at[s], sems.at[s])
                for src, dst in ((w1_hbm, wf1), (w3_hbm, wf3), (w2_hbm, wf2))]

    @pl.when(i < nused_ref[0])
    def _():
        @pl.when(first_ref[i] == 1)
        def _():
            e = blk_e_ref[i]
            s = slot_ref[i]

            @pl.when(i == 0)
            def _():
                for c in weights(e, s):
                    c.start()

            for c in weights(e, s):
                c.wait()
            w1b[...] = wf1[s].astype(BF16)
            w3b[...] = wf3[s].astype(BF16)
            w2b[...] = wf2[s].astype(BF16)

            @pl.when(next_e_ref[i] >= 0)
            def _():
                for c in weights(next_e_ref[i], 1 - s):
                    c.start()

        x = jnp.concatenate([c.astype(BF16) for c in _load_slab_rows(x_ref, slabs)], axis=1)
        h1 = _dot(x, w1b[...])
        h3 = _dot(x, w3b[...])
        a = (_silu(h1) * h3).astype(BF16)
        _store_slab_rows(y_ref, _dot(a, w2b[...]))

    @pl.when(i >= nused_ref[0])
    def _():
        y_ref[...] = jnp.zeros_like(y_ref)


def _experts(blk_e, nused, xs, w1, w3, w2, slabs):
    n_rows = xs.shape[0] // slabs
    e, d, f = w1.shape
    nb = n_rows // EXPERT_BLOCK

    blk = jnp.arange(nb, dtype=I32)
    first = ((blk == 0) | (blk_e != jnp.roll(blk_e, 1))) & (blk < nused[0])
    slot = (jnp.cumsum(first.astype(I32)) - 1) & 1
    opens_at = jnp.where(first, blk, nb)
    next_open = jnp.concatenate([lax.cummin(opens_at, reverse=True)[1:], jnp.full((1,), nb, I32)])
    next_e = jnp.where(next_open < nb, blk_e[jnp.minimum(next_open, nb - 1)], -1).astype(I32)

    def row_blk(i, blk_e_ref, nused_ref, *_):
        return (jnp.minimum(i, nused_ref[0] - 1), 0)

    any_space = pl.BlockSpec(memory_space=pl.ANY)
    return pl.pallas_call(
        functools.partial(_expert_kernel, slabs=slabs),
        out_shape=jax.ShapeDtypeStruct((n_rows * slabs, LANES), F32),
        grid_spec=pltpu.PrefetchScalarGridSpec(
            num_scalar_prefetch=5,
            grid=(nb,),
            in_specs=[pl.BlockSpec((EXPERT_BLOCK * slabs, LANES), row_blk),
                      any_space, any_space, any_space],
            out_specs=pl.BlockSpec((EXPERT_BLOCK * slabs, LANES), lambda i, *_: (i, 0)),
            scratch_shapes=[pltpu.VMEM((2, d, f), F32), pltpu.VMEM((2, d, f), F32),
                            pltpu.VMEM((2, f, d), F32),
                            pltpu.VMEM((d, f), BF16), pltpu.VMEM((d, f), BF16),
                            pltpu.VMEM((f, d), BF16),
                            pltpu.SemaphoreType.DMA((2,))],
        ),
        compiler_params=_cparams(("arbitrary",)),
        name="experts",
    )(blk_e, nused, first.astype(I32), slot.astype(I32), next_e, xs, w1, w3, w2)


def _final_kernel(loff_ref, gate_ref, seg_local, seg_table, seg_cnt, seg_local_nx, seg_table_nx, seg_cnt_nx,
                  u_ref, x1_ref, ada_ref, ws1_ref, ws3_ref, ws2_ref, g2_ref, b2_ref, ys_ref, o_ref,
                  lbuf, rbuf, sem, *, slabs):
    i = pl.program_id(0)
    last = pl.num_programs(0) - 1
    tf = u_ref.shape[0] // slabs

    def from_table(local, table, rows):
        return pltpu.make_async_copy(
            ys_ref.at[pl.ds(pl.multiple_of(table, slabs), rows * slabs)],
            lbuf.at[pl.ds(pl.multiple_of(local, slabs), rows * slabs)], sem)

    def fetch_tile(local_r, table_r, cnt_r):
        def fetch(e, c):
            _start_segment(e, local_r, table_r, cnt_r, from_table, slabs)
            return c

        lax.fori_loop(0, N_EXPERTS, fetch, 0)

    @pl.when(i == 0)
    def _():
        fetch_tile(seg_local, seg_table, seg_cnt)

    pltpu.make_async_copy(ys_ref.at[pl.ds(0, lbuf.shape[0])], lbuf, sem).wait()

    def combine(t, c):
        rows = [lbuf[pl.ds(pl.multiple_of(loff_ref[t * TOP_K + k], slabs), slabs), :] * gate_ref[t * TOP_K + k]
                for k in range(TOP_K)]
        while len(rows) > 1:
            rows = [a + b for a, b in zip(rows[0::2], rows[1::2])]
        _slab(rbuf, t, slabs)[...] = rows[0]
        return c

    lax.fori_loop(0, tf, combine, 0, unroll=MOE_ROW_UNROLL)

    @pl.when(i < last)
    def _():
        fetch_tile(seg_local_nx, seg_table_nx, seg_cnt_nx)

    u = jnp.concatenate([c.astype(BF16) for c in _load_slab_rows(u_ref, slabs)], axis=1)
    shared = _dot((_silu(_dot(u, ws1_ref[...])) * _dot(u, ws3_ref[...])).astype(BF16), ws2_ref[...])
    routed = jnp.concatenate(_load_slab_rows(rbuf, slabs), axis=1)
    gate2 = ada_ref[0, 5:6, :]
    y = shared + routed
    o_ref[...] = _layer_norm(DEEPNORM_ALPHA * x1_ref[...] + (1.0 + gate2) * y) * g2_ref[...] + b2_ref[...]


def _final(loff, gate, seg_local, seg_table, seg_cnt, u2s, x1, ada, ws1b, ws3b, ws2b, g2, b2, ys,
           seq, slabs):
    n, d = x1.shape
    tf = T_MOE
    n_tiles = n // tf
    steps_per_seq = seq // tf
    smem_tok = pl.BlockSpec((TOP_K * tf,), lambda i, *_: (i,), memory_space=pltpu.SMEM)
    smem_tile = pl.BlockSpec((N_EXPERTS,), lambda i, *_: (i,), memory_space=pltpu.SMEM)
    smem_next = pl.BlockSpec((N_EXPERTS,), lambda i, *_: (jnp.minimum(i + 1, n_tiles - 1),),
                             memory_space=pltpu.SMEM)
    const2 = lambda i, *_: (0, 0)
    return pl.pallas_call(
        functools.partial(_final_kernel, slabs=slabs),
        out_shape=jax.ShapeDtypeStruct((n, d), F32),
        grid_spec=pltpu.PrefetchScalarGridSpec(
            num_scalar_prefetch=0,
            grid=(n_tiles,),
            in_specs=[smem_tok, smem_tok, smem_tile, smem_tile, smem_tile, smem_next, smem_next, smem_next,
                      pl.BlockSpec((tf * slabs, LANES), lambda i, *_: (i, 0)),
                      pl.BlockSpec((tf, d), lambda i, *_: (i, 0)),
                      pl.BlockSpec((1,) + ada.shape[1:], lambda i, *_: (i // steps_per_seq, 0, 0)),
                      pl.BlockSpec(ws1b.shape, const2),
                      pl.BlockSpec(ws3b.shape, const2),
                      pl.BlockSpec(ws2b.shape, const2),
                      pl.BlockSpec(g2.shape, const2),
                      pl.BlockSpec(b2.shape, const2),
                      pl.BlockSpec(memory_space=pl.ANY)],
            out_specs=pl.BlockSpec((tf, d), lambda i, *_: (i, 0)),
            scratch_shapes=[pltpu.VMEM((TOP_K * tf * slabs, LANES), F32),
                            pltpu.VMEM((tf * slabs, LANES), F32),
                            pltpu.SemaphoreType.DMA],
        ),
        compiler_params=_cparams(("arbitrary",)),
        name="final",
    )(loff, gate, seg_local, seg_table, seg_cnt, seg_local, seg_table, seg_cnt, u2s, x1, ada, ws1b, ws3b,
      ws2b, g2, b2, ys)


def _swa_head_permutation():
    per_group = SWA_HEADS // SWA_KV_HEADS
    cols = []
    for c in range(per_group):
        for g in range(SWA_KV_HEADS):
            h = c + per_group * g
            cols.extend(range(h * HEAD_DIM, (h + 1) * HEAD_DIM))
    return np.asarray(cols)


def _layer(x, c, positions, w_ada, b_ada, w_in, sinks, g_swa, g_sb, w_out, ln1_g, ln1_b,
           w_router, e_bias, w1, w3, w2, ws1, ws3, ws2, ln2_g, ln2_b):
    batch, seq, d = x.shape
    n = batch * seq
    x2 = x.reshape(n, d)
    perm = _swa_head_permutation()
    per_group = SWA_HEADS // SWA_KV_HEADS

    ada = _ada(c, w_ada, b_ada)

    o_q = SWA_WIDTH + 2 * SWA_KV_WIDTH
    w_in_b = jnp.concatenate(
        [w_in[:, :SWA_WIDTH][:, perm], w_in[:, SWA_WIDTH:o_q],
         w_in[:, o_q:o_q + SB_WIDTH] * (HEAD_DIM ** -0.5), w_in[:, o_q + SB_WIDTH:]], axis=1).astype(BF16)
    half = HEAD_DIM // 2
    inv_freq = ROPE_THETA ** (-jnp.arange(half, dtype=F32) * 2.0 / HEAD_DIM)
    invf = jnp.tile(inv_freq, LANES // half).reshape(1, LANES)
    pos_col = positions.reshape(n, 1).astype(F32)
    qa, ka, va, qs, ks, vs = _inproj(x2, ada, pos_col, invf, w_in_b, seq)

    head_of_block = np.asarray([c_ + per_group * g for g in range(SWA_KV_HEADS) for c_ in range(per_group)])
    sink_col = jnp.repeat(sinks.astype(F32)[head_of_block], WINDOW).reshape(SWA_HEADS * WINDOW, 1)
    oa = _swa(qa, ka, va, sink_col, batch, seq)
    ob = _sb(qs, ks, vs, batch, seq)

    w_out_b = jnp.concatenate([w_out[:SWA_WIDTH][perm], w_out[SWA_WIDTH:]], axis=0).astype(BF16)
    wr_t = w_router.T.astype(F32)
    wr_hi = wr_t.astype(BF16)
    wr_lo = (wr_t - wr_hi.astype(F32)).astype(BF16)
    x1, u2s, logits_t = _post(oa, ob, x2, ada, g_swa[perm].reshape(1, -1), g_sb.reshape(1, -1), w_out_b,
                              ln1_g.reshape(1, d), ln1_b.reshape(1, d), wr_hi, wr_lo, seq)

    lpos_t, gate_t, seglo, cnt, seen_b = _route(logits_t, e_bias)

    counts = seen_b[-1] + cnt[-1]
    padded = (counts + EXPERT_BLOCK - 1) // EXPERT_BLOCK * EXPERT_BLOCK
    pend = jnp.cumsum(padded).astype(I32)
    pstart = pend - padded
    n_rows = n * TOP_K + N_EXPERTS * EXPERT_BLOCK
    n_blocks = n_rows // EXPERT_BLOCK
    block_row0 = jnp.arange(n_blocks, dtype=I32) * EXPERT_BLOCK
    blk_e = jnp.minimum(jnp.sum((pend[None, :] <= block_row0[:, None]).astype(I32), axis=1), N_EXPERTS - 1)
    nused = (pend[-1:] // EXPERT_BLOCK).astype(I32)
    slabs = d // LANES
    loff = (lpos_t * slabs).T.reshape(-1)
    gate = gate_t.T.reshape(-1)
    seg_local = (seglo[::8] * slabs).reshape(-1)
    seg_table = ((pstart[None, :] + seen_b[::8]) * slabs).reshape(-1)
    seg_cnt = cnt[::8].reshape(-1)

    xs = _dispatch(pstart + counts, pend, nused, loff, seg_local, seg_table, seg_cnt, u2s, n_rows, slabs)
    ys = _experts(blk_e, nused, xs, w1, w3, w2, slabs)
    out = _final(loff, gate, seg_local, seg_table, seg_cnt, u2s, x1, ada, ws1.astype(BF16),
                 ws3.astype(BF16), ws2.astype(BF16), ln2_g.reshape(1, d), ln2_b.reshape(1, d), ys, seq, slabs)
    return out.reshape(batch, seq, d)


def kernel(x, c, positions, w_ada, b_ada, w_in, attn_sinks, g_swa, g_sb, w_out, ln1_g, ln1_b,
           w_router, e_bias, w1, w3, w2, ws1, ws3, ws2, ln2_g, ln2_b):
    assert w_ada.shape[0] == DEPTH
    for l in range(DEPTH):
        x = _layer(x, c, positions, w_ada[l], b_ada[l], w_in[l], attn_sinks[l], g_swa[l], g_sb[l],
                   w_out[l], ln1_g[l], ln1_b[l], w_router[l], e_bias[l], w1[l], w3[l], w2[l],
                   ws1[l], ws3[l], ws2[l], ln2_g[l], ln2_b[l])
    return x
```

```python
import functools

import numpy as np
import jax
import jax.numpy as jnp
from jax import lax
from jax.experimental import pallas as pl
from jax.experimental.pallas import tpu as pltpu

F32 = jnp.float32
BF16 = jnp.bfloat16
I32 = jnp.int32

HEAD_DIM = 64
SWA_HEADS = 8
SWA_KV_HEADS = 2
SB_HEADS = 8
SWA_WIDTH = SWA_HEADS * HEAD_DIM
SWA_KV_WIDTH = SWA_KV_HEADS * HEAD_DIM
SB_WIDTH = SB_HEADS * HEAD_DIM
WINDOW = 128
ROPE_THETA = 10000.0
N_EXPERTS = 256
TOP_K = 8
N_GROUPS = 8
TOPK_GROUPS = 4
GROUP_SIZE = N_EXPERTS // N_GROUPS
ROUTED_SCALE = 2.5
LN_EPS = 1e-5
DEPTH = 1
DEEPNORM_ALPHA = (2 * DEPTH) ** 0.25

LANES = 128
SB_BLOCK = 128
SB_TILE_BLOCKS = 3
SB_UNDERFLOW = 110.0
NEG_BIG = -1e30
VMEM_LIMIT = 56 * 1024 * 1024

TM_PROJ = 512
TQ_SWA = 512
T_MOE = 512
EXPERT_BLOCK = 512
ZERO_ROWS = 128
SEGMENT_CHUNK = 8
MOE_ROW_UNROLL = 4


def _cparams(sem, vmem=VMEM_LIMIT):
    return pltpu.CompilerParams(dimension_semantics=sem, vmem_limit_bytes=vmem)


def _layer_norm(x):
    mu = jnp.mean(x, axis=-1, keepdims=True)
    xc = x - mu
    var = jnp.mean(xc * xc, axis=-1, keepdims=True)
    return xc * lax.rsqrt(var + LN_EPS)


def _silu(x):
    return x * jax.nn.sigmoid(x)


def _dot(a, b):
    return jnp.dot(a, b, preferred_element_type=F32)


def _dot_nt(a, b):
    return lax.dot_general(a, b, (((1,), (1,)), ((), ())), preferred_element_type=F32)


def _store_slab_rows(ref, x):
    t, d = x.shape
    slabs = d // LANES
    for s in range(slabs):
        ref[pl.ds(s, t, stride=slabs), :] = x[:, s * LANES:(s + 1) * LANES]


def _load_slab_rows(ref, slabs):
    t = ref.shape[0] // slabs
    return [ref[pl.ds(s, t, stride=slabs), :] for s in range(slabs)]


def _ada_kernel(c_ref, w_ref, b_ref, o_ref):
    sc = _silu(c_ref[...])
    o_ref[...] = _dot(sc.astype(BF16), w_ref[...].astype(BF16)) + b_ref[...]


def _ada(c, w_ada, b_ada):
    b, d = c.shape
    n_out = w_ada.shape[1]
    rows = 8
    c_pad = jnp.zeros((rows, d), F32).at[:b].set(c)
    out = pl.pallas_call(
        _ada_kernel,
        out_shape=jax.ShapeDtypeStruct((rows, n_out), F32),
        grid=(n_out // d,),
        in_specs=[
            pl.BlockSpec((rows, d), lambda j: (0, 0)),
            pl.BlockSpec((d, d), lambda j: (0, j)),
            pl.BlockSpec((1, d), lambda j: (0, j)),
        ],
        out_specs=pl.BlockSpec((rows, d), lambda j: (0, j)),
        compiler_params=_cparams(("arbitrary",)),
        name="ada",
    )(c_pad, w_ada, b_ada.reshape(1, n_out))
    return out[:b].reshape(b, n_out // d, d)


def _inproj_kernel(x_ref, ada_ref, pos_ref, invf_ref, w_ref,
                   qa_ref, ka_ref, va_ref, qs_ref, ks_ref, vs_ref):
    x = x_ref[...]
    shift = ada_ref[0, 0:1, :]
    scale = ada_ref[0, 1:2, :]
    u = _layer_norm(x) * (1.0 + scale) + shift
    h = _dot(u.astype(BF16), w_ref[...])

    ang = pos_ref[...] * invf_ref[...]
    cs = jnp.cos(ang)
    sn = jnp.sin(ang)
    lane = lax.broadcasted_iota(I32, (1, LANES), 1)
    first = (lane & (HEAD_DIM // 2)) == 0
    sn_signed = jnp.where(first, -sn, sn)

    def rope(hc):
        partner = jnp.where(first, pltpu.roll(hc, LANES - HEAD_DIM // 2, 1),
                            pltpu.roll(hc, HEAD_DIM // 2, 1))
        return hc * cs + partner * sn_signed

    q_scale = HEAD_DIM ** -0.5
    o = 0
    for c in range(SWA_WIDTH // LANES):
        qa_ref[:, c * LANES:(c + 1) * LANES] = (rope(h[:, o:o + LANES]) * q_scale).astype(BF16)
        o += LANES
    ka_ref[...] = rope(h[:, o:o + SWA_KV_WIDTH]).astype(BF16)
    o += SWA_KV_WIDTH
    va_ref[...] = h[:, o:o + SWA_KV_WIDTH].astype(BF16)
    o += SWA_KV_WIDTH
    qs_ref[...] = h[:, o:o + SB_WIDTH].astype(BF16)
    o += SB_WIDTH
    ks_ref[...] = h[:, o:o + SB_WIDTH].astype(BF16)
    o += SB_WIDTH
    vs_ref[...] = h[:, o:o + SB_WIDTH].astype(BF16)


def _inproj(x2, ada, pos_col, invf, w_in_b, seq):
    n, d = x2.shape
    tm = TM_PROJ
    steps_per_seq = seq // tm
    widths = (SWA_WIDTH, SWA_KV_WIDTH, SWA_KV_WIDTH, SB_WIDTH, SB_WIDTH, SB_WIDTH)
    return pl.pallas_call(
        _inproj_kernel,
        out_shape=[jax.ShapeDtypeStruct((n, w), BF16) for w in widths],
        grid=(n // tm,),
        in_specs=[
            pl.BlockSpec((tm, d), lambda i: (i, 0)),
            pl.BlockSpec((1,) + ada.shape[1:], lambda i: (i // steps_per_seq, 0, 0)),
            pl.BlockSpec((tm, 1), lambda i: (i, 0)),
            pl.BlockSpec((1, LANES), lambda i: (0, 0)),
            pl.BlockSpec(w_in_b.shape, lambda i: (0, 0)),
        ],
        out_specs=[pl.BlockSpec((tm, w), lambda i: (i, 0)) for w in widths],
        compiler_params=_cparams(("arbitrary",)),
        name="inproj",
    )(x2, ada, pos_col, invf, w_in_b)


def _swa_kernel(q_ref, kc_ref, kp_ref, vc_ref, vp_ref, bias0_ref, bias_ref, sink_ref,
                o_ref, kall, vall):
    tq = q_ref.shape[0]
    kall[0:WINDOW, :] = kp_ref[...]
    kall[WINDOW:, :] = kc_ref[...]
    vall[0:WINDOW, :] = vp_ref[...]
    vall[WINDOW:, :] = vc_ref[...]
    lane = lax.broadcasted_iota(I32, (1, LANES), 1)
    half0 = lane < HEAD_DIM
    half1 = lane >= HEAD_DIM
    sink = sink_ref[...]
    n_col = SWA_WIDTH // LANES
    for j in range(tq // WINDOW):
        r0 = j * WINDOW
        q = q_ref[r0:r0 + WINDOW, :]
        parts = []
        for half in (half0, half1):
            for c in range(n_col):
                parts.append(jnp.where(half, q[:, c * LANES:(c + 1) * LANES], jnp.zeros((), BF16)))
        qm = jnp.concatenate(parts, axis=0)
        kb = kall[r0:r0 + 2 * WINDOW, :]
        vb = vall[r0:r0 + 2 * WINDOW, :]
        s = _dot_nt(qm, kb)
        s = s + (bias0_ref[0] if j == 0 else bias_ref[...])
        m = jnp.maximum(jnp.max(s, axis=1, keepdims=True), sink)
        p = jnp.exp(s - m)
        den = jnp.sum(p, axis=1, keepdims=True) + jnp.exp(sink - m)
        o = _dot(p.astype(BF16), vb) / den
        for c in range(n_col):
            lo = o[c * WINDOW:(c + 1) * WINDOW]
            hi = o[(n_col + c) * WINDOW:(n_col + c + 1) * WINDOW]
            o_ref[r0:r0 + WINDOW, c * LANES:(c + 1) * LANES] = jnp.where(half0, lo, hi)


def _swa_bias():
    qi = np.arange(SWA_HEADS * WINDOW)[:, None] % WINDOW
    kj = np.arange(2 * WINDOW)[None, :]
    dist = qi + WINDOW - kj
    band = (dist >= 0) & (dist < WINDOW)
    bias = np.where(band, 0.0, NEG_BIG).astype(np.float32)
    first = np.where(band & (kj >= WINDOW), 0.0, NEG_BIG).astype(np.float32)
    return jnp.asarray(np.stack([bias, first])), jnp.asarray(bias)


def _swa(qa, ka, va, sink_col, batch, seq):
    n = qa.shape[0]
    tq = TQ_SWA
    nt = seq // tq
    blocks_per_tile = tq // WINDOW
    bias_pair, bias = _swa_bias()

    def cur(b, i):
        return (b * nt + i, 0)

    def prev(b, i):
        return (jnp.maximum((b * nt + i) * blocks_per_tile - 1, 0), 0)

    return pl.pallas_call(
        _swa_kernel,
        out_shape=jax.ShapeDtypeStruct((n, SWA_WIDTH), F32),
        grid=(batch, nt),
        in_specs=[
            pl.BlockSpec((tq, SWA_WIDTH), cur),
            pl.BlockSpec((tq, SWA_KV_WIDTH), cur),
            pl.BlockSpec((WINDOW, SWA_KV_WIDTH), prev),
            pl.BlockSpec((tq, SWA_KV_WIDTH), cur),
            pl.BlockSpec((WINDOW, SWA_KV_WIDTH), prev),
            pl.BlockSpec((1,) + bias.shape, lambda b, i: (jnp.where(i == 0, 1, 0), 0, 0)),
            pl.BlockSpec(bias.shape, lambda b, i: (0, 0)),
            pl.BlockSpec(sink_col.shape, lambda b, i: (0, 0)),
        ],
        out_specs=pl.BlockSpec((tq, SWA_WIDTH), cur),
        scratch_shapes=[pltpu.VMEM((tq + WINDOW, SWA_KV_WIDTH), BF16),
                        pltpu.VMEM((tq + WINDOW, SWA_KV_WIDTH), BF16)],
        compiler_params=_cparams(("arbitrary", "arbitrary")),
        name="swa",
    )(qa, ka, ka, va, va, bias_pair, bias, sink_col)


def _sb_kernel(q_ref, k_ref, v_ref, uu_ref, o_ref, qm, acc, run):
    i = pl.program_id(1)
    blk = SB_BLOCK
    n_sub = SB_TILE_BLOCKS
    tile = n_sub * blk
    pairs = q_ref.shape[1] // LANES
    lane = lax.broadcasted_iota(I32, (1, LANES), 1)
    half0 = lane < HEAD_DIM
    half1 = lane >= HEAD_DIM
    zero = jnp.zeros((), BF16)
    for p in range(pairs):
        q = q_ref[:, p * LANES:(p + 1) * LANES]
        qm[p] = jnp.concatenate([jnp.where(half0, q, zero), jnp.where(half1, q, zero)], axis=0)
    acc[...] = jnp.zeros_like(acc)
    run[...] = jnp.zeros_like(run)
    q_pos = i * blk + (lax.broadcasted_iota(I32, (2 * blk, blk), 0) & (blk - 1))
    col = lax.broadcasted_iota(I32, (2 * blk, blk), 1)

    def body(carry):
        end, _ = carry
        start = pl.multiple_of(jnp.maximum(end - tile, 0), blk)
        limit = jnp.minimum(end, q_pos) - start
        least = None
        for p in range(pairs):
            kt = k_ref[pl.ds(start, tile), p * LANES:(p + 1) * LANES]
            vt = v_ref[pl.ds(start, tile), p * LANES:(p + 1) * LANES]
            z = _dot_nt(qm[p], kt)
            mass_seen = run[p]
            ws = [None] * n_sub
            for b in reversed(range(n_sub)):
                zb = jnp.where(col < limit - b * blk, z[:, b * blk:(b + 1) * blk], NEG_BIG)
                mass = jnp.maximum(zb, 0.0) + jnp.log(1.0 + jnp.exp(-jnp.abs(zb)))
                mass_hi = mass.astype(BF16)
                mass_lo = (mass - mass_hi.astype(F32)).astype(BF16)
                cc = _dot(jnp.concatenate([mass_hi, mass_lo], axis=1), uu_ref[...])
                ws[b] = jnp.exp(zb - (cc[:, :blk] + mass_seen)).astype(BF16)
                mass_seen = mass_seen + cc[:, blk:]
            acc[p] += _dot(jnp.concatenate(ws, axis=1), vt)
            run[p] = mass_seen
            least = mass_seen if least is None else jnp.minimum(least, mass_seen)
        return start, jnp.min(least) > SB_UNDERFLOW

    lax.while_loop(lambda c: jnp.logical_and(c[0] > 0, jnp.logical_not(c[1])),
                   body, ((i + 1) * blk, jnp.bool_(False)))
    for p in range(pairs):
        a = acc[p]
        o_ref[:, p * LANES:(p + 1) * LANES] = jnp.where(half0, a[:blk], a[blk:])


def _sb_suffix_matrix():
    j = np.arange(2 * SB_BLOCK)[:, None] % SB_BLOCK
    s = np.arange(2 * SB_BLOCK)[None, :]
    m = np.where(s < SB_BLOCK, (j >= s), True)
    return jnp.asarray(m.astype(np.float32)).astype(BF16)


def _sb(qs, ks, vs, batch, seq):
    n = qs.shape[0]
    blk = SB_BLOCK
    nq = seq // blk
    pairs = SB_WIDTH // LANES
    uu = _sb_suffix_matrix()
    return pl.pallas_call(
        _sb_kernel,
        out_shape=jax.ShapeDtypeStruct((n, SB_WIDTH), F32),
        grid=(batch, nq),
        in_specs=[
            pl.BlockSpec((blk, SB_WIDTH), lambda b, i: (b * nq + i, 0)),
            pl.BlockSpec((seq, SB_WIDTH), lambda b, i: (b, 0)),
            pl.BlockSpec((seq, SB_WIDTH), lambda b, i: (b, 0)),
            pl.BlockSpec(uu.shape, lambda b, i: (0, 0)),
        ],
        out_specs=pl.BlockSpec((blk, SB_WIDTH), lambda b, i: (b * nq + i, 0)),
        scratch_shapes=[pltpu.VMEM((pairs, 2 * blk, LANES), BF16),
                        pltpu.VMEM((pairs, 2 * blk, LANES), F32),
                        pltpu.VMEM((pairs, 2 * blk, LANES), F32)],
        compiler_params=_cparams(("arbitrary", "arbitrary")),
        name="sb",
    )(qs, ks, vs, uu)


def _post_kernel(oa_ref, ob_ref, x_ref, ada_ref, ga_ref, gb_ref, wo_ref, g1_ref, b1_ref,
                 wrh_ref, wrl_ref, x1_ref, u2s_ref, lg_ref):
    def rms(o, g):
        return o * lax.rsqrt(jnp.mean(o * o, axis=-1, keepdims=True) + LN_EPS) * g

    na = rms(oa_ref[...], ga_ref[...]).astype(BF16)
    nb = rms(ob_ref[...], gb_ref[...]).astype(BF16)
    y = _dot(na, wo_ref[0:SWA_WIDTH, :]) + _dot(nb, wo_ref[SWA_WIDTH:, :])
    gate1 = ada_ref[0, 2:3, :]
    shift2 = ada_ref[0, 3:4, :]
    scale2 = ada_ref[0, 4:5, :]
    x1 = _layer_norm(DEEPNORM_ALPHA * x_ref[...] + (1.0 + gate1) * y) * g1_ref[...] + b1_ref[...]
    x1_ref[...] = x1
    u2 = _layer_norm(x1) * (1.0 + scale2) + shift2
    _store_slab_rows(u2s_ref, u2)
    u_hi = u2.astype(BF16)
    u_lo = (u2 - u_hi.astype(F32)).astype(BF16)
    wrh = wrh_ref[...]
    lg_ref[...] = _dot_nt(wrh, u_hi) + (_dot_nt(wrh, u_lo) + _dot_nt(wrl_ref[...], u_hi))


def _post(oa, ob, x2, ada, g_a, g_b, w_out_b, g1, b1, wr_hi, wr_lo, seq):
    n, d = x2.shape
    tm = TM_PROJ
    steps_per_seq = seq // tm
    e = wr_hi.shape[0]
    slabs = d // LANES
    const2 = lambda i: (0, 0)
    return pl.pallas_call(
        _post_kernel,
        out_shape=[jax.ShapeDtypeStruct((n, d), F32),
                   jax.ShapeDtypeStruct((n * slabs, LANES), F32),
                   jax.ShapeDtypeStruct((e, n), F32)],
        grid=(n // tm,),
        in_specs=[
            pl.BlockSpec((tm, SWA_WIDTH), lambda i: (i, 0)),
            pl.BlockSpec((tm, SB_WIDTH), lambda i: (i, 0)),
            pl.BlockSpec((tm, d), lambda i: (i, 0)),
            pl.BlockSpec((1,) + ada.shape[1:], lambda i: (i // steps_per_seq, 0, 0)),
            pl.BlockSpec(g_a.shape, const2),
            pl.BlockSpec(g_b.shape, const2),
            pl.BlockSpec(w_out_b.shape, const2),
            pl.BlockSpec(g1.shape, const2),
            pl.BlockSpec(b1.shape, const2),
            pl.BlockSpec(wr_hi.shape, const2),
            pl.BlockSpec(wr_lo.shape, const2),
        ],
        out_specs=[pl.BlockSpec((tm, d), lambda i: (i, 0)),
                   pl.BlockSpec((tm * slabs, LANES), lambda i: (i, 0)),
                   pl.BlockSpec((e, tm), lambda i: (0, i))],
        compiler_params=_cparams(("arbitrary",)),
        name="post",
    )(oa, ob, x2, ada, g_a, g_b, w_out_b, g1, b1, wr_hi, wr_lo)


def _route_kernel(lg_ref, bias_ref, su_ref, sl_ref, lpos_ref, gate_ref, seglo_ref, cnt_ref, seenb_ref,
                  seen_row):
    i = pl.program_id(0)

    @pl.when(i == 0)
    def _():
        seen_row[...] = jnp.zeros_like(seen_row)

    tt = lg_ref.shape[1]
    ninf = -jnp.inf
    scores = jax.nn.sigmoid(lg_ref[...])
    biased = scores + bias_ref[...]

    iog32 = lax.broadcasted_iota(I32, (GROUP_SIZE, tt), 0)
    groups = [biased[g * GROUP_SIZE:(g + 1) * GROUP_SIZE, :] for g in range(N_GROUPS)]
    gs_rows = []
    for blk in groups:
        m1 = jnp.max(blk, axis=0, keepdims=True)
        i1 = jnp.min(jnp.where(blk == m1, iog32, GROUP_SIZE), axis=0, keepdims=True)
        m2 = jnp.max(jnp.where(iog32 == i1, ninf, blk), axis=0, keepdims=True)
        gs_rows.append(m1 + m2)
    gs = jnp.concatenate(gs_rows, axis=0)

    iog = lax.broadcasted_iota(I32, gs.shape, 0)
    gsel = jnp.zeros(gs.shape, F32)
    cur = gs
    for _ in range(TOPK_GROUPS):
        m = jnp.max(cur, axis=0, keepdims=True)
        idx = jnp.min(jnp.where(cur == m, iog, N_GROUPS), axis=0, keepdims=True)
        hit = iog == idx
        gsel = jnp.where(hit, 1.0, gsel)
        cur = jnp.where(hit, ninf, cur)

    cand = jnp.concatenate(
        [jnp.where(gsel[g:g + 1, :] > 0.5, groups[g], ninf) for g in range(N_GROUPS)], axis=0)
    ioe = lax.broadcasted_iota(I32, cand.shape, 0)
    chosen = jnp.zeros(cand.shape, F32)
    idxs, gates = [], []
    for _ in range(TOP_K):
        m = jnp.max(cand, axis=0, keepdims=True)
        idx = jnp.min(jnp.where(cand == m, ioe, N_EXPERTS), axis=0, keepdims=True)
        hit = ioe == idx
        gates.append(jnp.sum(jnp.where(hit, scores, 0.0), axis=0, keepdims=True))
        cand = jnp.where(hit, ninf, cand)
        chosen = jnp.where(hit, 1.0, chosen)
        idxs.append(idx)
    gsum = gates[0]
    for g in gates[1:]:
        gsum = gsum + g
    gates = [g / gsum * ROUTED_SCALE for g in gates]

    def hi_lo(x):
        hi = x.astype(BF16)
        return hi, (x - hi.astype(F32)).astype(BF16)

    chosen_b = chosen.astype(BF16)
    rk = _dot(chosen_b, su_ref[...])
    c_hi, c_lo = hi_lo(rk[:, tt:])
    first_col = _dot(sl_ref[...], c_hi) + _dot(sl_ref[...], c_lo)
    slot_mat = rk[:, :tt] + jnp.concatenate([first_col] * (tt // LANES), axis=1)
    slots = [jnp.sum(jnp.where(ioe == idx, slot_mat, 0.0), axis=0, keepdims=True) for idx in idxs]
    lpos_ref[...] = jnp.concatenate(slots, axis=0).astype(I32)
    gate_ref[...] = jnp.concatenate(gates, axis=0)
    cnt_row = _dot_nt(jnp.ones((8, tt), BF16), chosen_b)
    r_hi, r_lo = hi_lo(cnt_row)
    before_row = seen_row[...]
    seen_row[...] = before_row + cnt_row
    seglo_ref[...] = (_dot_nt(r_hi, sl_ref[...]) + _dot_nt(r_lo, sl_ref[...])).astype(I32)
    cnt_ref[...] = cnt_row.astype(I32)
    seenb_ref[...] = before_row.astype(I32)


def _route_prefix_matrix(tt):
    a = np.arange(tt)[:, None] < np.arange(tt)[None, :]
    m = np.concatenate([a, np.ones((tt, LANES), bool)], axis=1)
    return jnp.asarray(m.astype(np.float32)).astype(BF16)


def _route(logits_t, e_bias):
    e, n = logits_t.shape
    tt = T_MOE
    bias_b = jnp.broadcast_to(e_bias.astype(F32)[:, None], (e, tt))
    su = _route_prefix_matrix(tt)
    sl = jnp.asarray((np.arange(e)[None, :] < np.arange(e)[:, None]).astype(np.float32)).astype(BF16)
    tok = lambda i: (0, i)
    per_tile = jax.ShapeDtypeStruct((n // tt * 8, e), I32)
    per_tile_spec = pl.BlockSpec((8, e), lambda i: (i, 0))
    return pl.pallas_call(
        _route_kernel,
        out_shape=[jax.ShapeDtypeStruct((TOP_K, n), I32),
                   jax.ShapeDtypeStruct((TOP_K, n), F32),
                   per_tile, per_tile, per_tile],
        grid=(n // tt,),
        in_specs=[pl.BlockSpec((e, tt), tok),
                  pl.BlockSpec((e, tt), lambda i: (0, 0)),
                  pl.BlockSpec(su.shape, lambda i: (0, 0)),
                  pl.BlockSpec(sl.shape, lambda i: (0, 0))],
        out_specs=[pl.BlockSpec((TOP_K, tt), tok), pl.BlockSpec((TOP_K, tt), tok),
                   per_tile_spec, per_tile_spec, per_tile_spec],
        scratch_shapes=[pltpu.VMEM((8, e), F32)],
        compiler_params=_cparams(("arbitrary",)),
        name="route",
    )(logits_t, bias_b, su, sl)


def _slab(ref, row, slabs):
    return ref.at[pl.ds(pl.multiple_of(row * slabs, slabs), slabs)]


def _start_segment(e, local_ref, table_ref, cnt_ref, make_copy, slabs):
    cnt = cnt_ref[e]
    local = local_ref[e]
    table = table_ref[e]
    n_chunks = lax.shift_right_logical(cnt, SEGMENT_CHUNK.bit_length() - 1)
    step = SEGMENT_CHUNK * slabs

    def chunk(j, c):
        make_copy(local + j * step, table + j * step, SEGMENT_CHUNK).start()
        return c

    lax.fori_loop(0, n_chunks, chunk, 0)
    done = n_chunks * step
    rows = SEGMENT_CHUNK // 2
    while rows >= 1:
        take = (cnt & rows) != 0

        @pl.when(take)
        def _(done=done, rows=rows):
            make_copy(local + done, table + done, rows).start()

        done = done + jnp.where(take, rows * slabs, 0)
        rows //= 2


def _dispatch_kernel(pvalid_ref, pend_ref, nused_ref, loff_ref, seg_local_ref, seg_table_ref, seg_cnt_ref,
                     u_ref, xs_ref, obuf, zbuf, sems, sem, *, slabs):
    i = pl.program_id(0)
    last = pl.num_programs(0) - 1
    td = u_ref.shape[0] // slabs
    zrows = zbuf.shape[0] // slabs
    chunks_per_block = EXPERT_BLOCK // zrows
    n_chunks = xs_ref.shape[0] // slabs // zrows

    def zero_rows(row0, rows):
        return pltpu.make_async_copy(
            zbuf.at[pl.ds(0, rows * slabs)],
            xs_ref.at[pl.ds(pl.multiple_of(row0 * slabs, slabs), rows * slabs)], sem)

    def zero_fill(wait):
        def go(copy):
            if wait:
                copy.wait()
            else:
                copy.start()

        def unused(b, c):
            go(zero_rows(b * zrows, zrows))
            return c

        lax.fori_loop(nused_ref[0] * chunks_per_block, n_chunks, unused, 0)

        def padding(e, c):
            first = pvalid_ref[e]
            end = pend_ref[e]
            for g in range(chunks_per_block):
                row0 = end - zrows * (g + 1)

                @pl.when(row0 >= first)
                def _(row0=row0):
                    go(zero_rows(row0, zrows))
            rem = (end - first) & (zrows - 1)
            done = jnp.int32(0)
            rows = zrows // 2
            while rows >= 1:
                take = (rem & rows) != 0

                @pl.when(take)
                def _(done=done, rows=rows):
                    go(zero_rows(first + done, rows))

                done = done + jnp.where(take, rows, 0)
                rows //= 2
            return c

        lax.fori_loop(0, N_EXPERTS, padding, 0)

    @pl.when(i == 0)
    def _():
        zbuf[...] = jnp.zeros_like(zbuf)
        zero_fill(wait=False)

    slot = i % 2
    buf = obuf.at[slot]
    out_sem = sems.at[slot]

    def drained(b, s):
        return pltpu.make_async_copy(obuf.at[b], xs_ref.at[pl.ds(0, obuf.shape[1])], sems.at[s])

    @pl.when(i >= 2)
    def _():
        drained(slot, slot).wait()

    def place(t, c):
        row = _slab(u_ref, t, slabs)[...]
        for k in range(TOP_K):
            buf[pl.ds(pl.multiple_of(loff_ref[t * TOP_K + k], slabs), slabs), :] = row
        return c

    lax.fori_loop(0, td, place, 0, unroll=MOE_ROW_UNROLL)

    def to_table(local, table, rows):
        return pltpu.make_async_copy(
            buf.at[pl.ds(pl.multiple_of(local, slabs), rows * slabs)],
            xs_ref.at[pl.ds(pl.multiple_of(table, slabs), rows * slabs)], out_sem)

    def send(e, c):
        _start_segment(e, seg_local_ref, seg_table_ref, seg_cnt_ref, to_table, slabs)
        return c

    lax.fori_loop(0, N_EXPERTS, send, 0)

    @pl.when(i == last)
    def _():
        drained(slot, slot).wait()

        @pl.when(last >= 1)
        def _():
            drained(1 - slot, 1 - slot).wait()

        zero_fill(wait=True)


def _dispatch(pvalid, pend, nused, loff, seg_local, seg_table, seg_cnt, u2s, n_rows, slabs):
    n = u2s.shape[0] // slabs
    td = T_MOE
    smem_tok = pl.BlockSpec((TOP_K * td,), lambda i, *_: (i,), memory_space=pltpu.SMEM)
    smem_tile = pl.BlockSpec((N_EXPERTS,), lambda i, *_: (i,), memory_space=pltpu.SMEM)
    return pl.pallas_call(
        functools.partial(_dispatch_kernel, slabs=slabs),
        out_shape=jax.ShapeDtypeStruct((n_rows * slabs, LANES), F32),
        grid_spec=pltpu.PrefetchScalarGridSpec(
            num_scalar_prefetch=3,
            grid=(n // td,),
            in_specs=[smem_tok, smem_tile, smem_tile, smem_tile,
                      pl.BlockSpec((td * slabs, LANES), lambda i, *_: (i, 0))],
            out_specs=pl.BlockSpec(memory_space=pl.ANY),
            scratch_shapes=[pltpu.VMEM((2, TOP_K * td * slabs, LANES), F32),
                            pltpu.VMEM((ZERO_ROWS * slabs, LANES), F32),
                            pltpu.SemaphoreType.DMA((2,)),
                            pltpu.SemaphoreType.DMA],
        ),
        compiler_params=_cparams(("arbitrary",)),
        name="dispatch",
    )(pvalid, pend, nused, loff, seg_local, seg_table, seg_cnt, u2s)


def _expert_kernel(blk_e_ref, nused_ref, first_ref, slot_ref, next_e_ref, x_ref, w1_hbm, w3_hbm, w2_hbm,
                   y_ref, wf1, wf3, wf2, w1b, w3b, w2b, sems, *, slabs):
    i = pl.program_id(0)

    def weights(e, s):
        return [pltpu.make_async_copy(src.at[e], dst.at[s], sems.at[s])
                for src, dst in ((w1_hbm, wf1), (w3_hbm, wf3), (w2_hbm, wf2))]

    @pl.when(i < nused_ref[0])
    def _():
        @pl.when(first_ref[i] == 1)
        def _():
            e = blk_e_ref[i]
            s = slot_ref[i]

            @pl.when(i == 0)
            def _():
                for c in weights(e, s):
                    c.start()

            for c in weights(e, s):
                c.wait()
            w1b[...] = wf1[s].astype(BF16)
            w3b[...] = wf3[s].astype(BF16)
            w2b[...] = wf2[s].astype(BF16)

            @pl.when(next_e_ref[i] >= 0)
            def _():
                for c in weights(next_e_ref[i], 1 - s):
                    c.start()

        x = jnp.concatenate([c.astype(BF16) for c in _load_slab_rows(x_ref, slabs)], axis=1)
        h1 = _dot(x, w1b[...])
        h3 = _dot(x, w3b[...])
        a = (_silu(h1) * h3).astype(BF16)
        _store_slab_rows(y_ref, _dot(a, w2b[...]))

    @pl.when(i >= nused_ref[0])
    def _():
        y_ref[...] = jnp.zeros_like(y_ref)


def _experts(blk_e, nused, xs, w1, w3, w2, slabs):
    n_rows = xs.shape[0] // slabs
    e, d, f = w1.shape
    nb = n_rows // EXPERT_BLOCK

    blk = jnp.arange(nb, dtype=I32)
    first = ((blk == 0) | (blk_e != jnp.roll(blk_e, 1))) & (blk < nused[0])
    slot = (jnp.cumsum(first.astype(I32)) - 1) & 1
    opens_at = jnp.where(first, blk, nb)
    next_open = jnp.concatenate([lax.cummin(opens_at, reverse=True)[1:], jnp.full((1,), nb, I32)])
    next_e = jnp.where(next_open < nb, blk_e[jnp.minimum(next_open, nb - 1)], -1).astype(I32)

    def row_blk(i, blk_e_ref, nused_ref, *_):
        return (jnp.minimum(i, nused_ref[0] - 1), 0)

    any_space = pl.BlockSpec(memory_space=pl.ANY)
    return pl.pallas_call(
        functools.partial(_expert_kernel, slabs=slabs),
        out_shape=jax.ShapeDtypeStruct((n_rows * slabs, LANES), F32),
        grid_spec=pltpu.PrefetchScalarGridSpec(
            num_scalar_prefetch=5,
            grid=(nb,),
            in_specs=[pl.BlockSpec((EXPERT_BLOCK * slabs, LANES), row_blk),
                      any_space, any_space, any_space],
            out_specs=pl.BlockSpec((EXPERT_BLOCK * slabs, LANES), lambda i, *_: (i, 0)),
            scratch_shapes=[pltpu.VMEM((2, d, f), F32), pltpu.VMEM((2, d, f), F32),
                            pltpu.VMEM((2, f, d), F32),
                            pltpu.VMEM((d, f), BF16), pltpu.VMEM((d, f), BF16),
                            pltpu.VMEM((f, d), BF16),
                            pltpu.SemaphoreType.DMA((2,))],
        ),
        compiler_params=_cparams(("arbitrary",)),
        name="experts",
    )(blk_e, nused, first.astype(I32), slot.astype(I32), next_e, xs, w1, w3, w2)


def _final_kernel(loff_ref, gate_ref, seg_local, seg_table, seg_cnt, seg_local_nx, seg_table_nx, seg_cnt_nx,
                  u_ref, x1_ref, ada_ref, ws1_ref, ws3_ref, ws2_ref, g2_ref, b2_ref, ys_ref, o_ref,
                  lbuf, rbuf, sem, *, slabs):
    i = pl.program_id(0)
    last = pl.num_programs(0) - 1
    tf = u_ref.shape[0] // slabs

    def from_table(local, table, rows):
        return pltpu.make_async_copy(
            ys_ref.at[pl.ds(pl.multiple_of(table, slabs), rows * slabs)],
            lbuf.at[pl.ds(pl.multiple_of(local, slabs), rows * slabs)], sem)

    def fetch_tile(local_r, table_r, cnt_r):
        def fetch(e, c):
            _start_segment(e, local_r, table_r, cnt_r, from_table, slabs)
            return c

        lax.fori_loop(0, N_EXPERTS, fetch, 0)

    @pl.when(i == 0)
    def _():
        fetch_tile(seg_local, seg_table, seg_cnt)

    pltpu.make_async_copy(ys_ref.at[pl.ds(0, lbuf.shape[0])], lbuf, sem).wait()

    def combine(t, c):
        rows = [lbuf[pl.ds(pl.multiple_of(loff_ref[t * TOP_K + k], slabs), slabs), :] * gate_ref[t * TOP_K + k]
                for k in range(TOP_K)]
        while len(rows) > 1:
            rows = [a + b for a, b in zip(rows[0::2], rows[1::2])]
        _slab(rbuf, t, slabs)[...] = rows[0]
        return c

    lax.fori_loop(0, tf, combine, 0, unroll=MOE_ROW_UNROLL)

    @pl.when(i < last)
    def _():
        fetch_tile(seg_local_nx, seg_table_nx, seg_cnt_nx)

    u = jnp.concatenate([c.astype(BF16) for c in _load_slab_rows(u_ref, slabs)], axis=1)
    shared = _dot((_silu(_dot(u, ws1_ref[...])) * _dot(u, ws3_ref[...])).astype(BF16), ws2_ref[...])
    routed = jnp.concatenate(_load_slab_rows(rbuf, slabs), axis=1)
    gate2 = ada_ref[0, 5:6, :]
    y = shared + routed
    o_ref[...] = _layer_norm(DEEPNORM_ALPHA * x1_ref[...] + (1.0 + gate2) * y) * g2_ref[...] + b2_ref[...]


def _final(loff, gate, seg_local, seg_table, seg_cnt, u2s, x1, ada, ws1b, ws3b, ws2b, g2, b2, ys,
           seq, slabs):
    n, d = x1.shape
    tf = T_MOE
    n_tiles = n // tf
    steps_per_seq = seq // tf
    smem_tok = pl.BlockSpec((TOP_K * tf,), lambda i, *_: (i,), memory_space=pltpu.SMEM)
    smem_tile = pl.BlockSpec((N_EXPERTS,), lambda i, *_: (i,), memory_space=pltpu.SMEM)
    smem_next = pl.BlockSpec((N_EXPERTS,), lambda i, *_: (jnp.minimum(i + 1, n_tiles - 1),),
                             memory_space=pltpu.SMEM)
    const2 = lambda i, *_: (0, 0)
    return pl.pallas_call(
        functools.partial(_final_kernel, slabs=slabs),
        out_shape=jax.ShapeDtypeStruct((n, d), F32),
        grid_spec=pltpu.PrefetchScalarGridSpec(
            num_scalar_prefetch=0,
            grid=(n_tiles,),
            in_specs=[smem_tok, smem_tok, smem_tile, smem_tile, smem_tile, smem_next, smem_next, smem_next,
                      pl.BlockSpec((tf * slabs, LANES), lambda i, *_: (i, 0)),
                      pl.BlockSpec((tf, d), lambda i, *_: (i, 0)),
                      pl.BlockSpec((1,) + ada.shape[1:], lambda i, *_: (i // steps_per_seq, 0, 0)),
                      pl.BlockSpec(ws1b.shape, const2),
                      pl.BlockSpec(ws3b.shape, const2),
                      pl.BlockSpec(ws2b.shape, const2),
                      pl.BlockSpec(g2.shape, const2),
                      pl.BlockSpec(b2.shape, const2),
                      pl.BlockSpec(memory_space=pl.ANY)],
            out_specs=pl.BlockSpec((tf, d), lambda i, *_: (i, 0)),
            scratch_shapes=[pltpu.VMEM((TOP_K * tf * slabs, LANES), F32),
                            pltpu.VMEM((tf * slabs, LANES), F32),
                            pltpu.SemaphoreType.DMA],
        ),
        compiler_params=_cparams(("arbitrary",)),
        name="final",
    )(loff, gate, seg_local, seg_table, seg_cnt, seg_local, seg_table, seg_cnt, u2s, x1, ada, ws1b, ws3b,
      ws2b, g2, b2, ys)


def _swa_head_permutation():
    per_group = SWA_HEADS // SWA_KV_HEADS
    cols = []
    for c in range(per_group):
        for g in range(SWA_KV_HEADS):
            h = c + per_group * g
            cols.extend(range(h * HEAD_DIM, (h + 1) * HEAD_DIM))
    return np.asarray(cols)


def _layer(x, c, positions, w_ada, b_ada, w_in, sinks, g_swa, g_sb, w_out, ln1_g, ln1_b,
           w_router, e_bias, w1, w3, w2, ws1, ws3, ws2, ln2_g, ln2_b):
    batch, seq, d = x.shape
    n = batch * seq
    x2 = x.reshape(n, d)
    perm = _swa_head_permutation()
    per_group = SWA_HEADS // SWA_KV_HEADS

    ada = _ada(c, w_ada, b_ada)

    o_q = SWA_WIDTH + 2 * SWA_KV_WIDTH
    w_in_b = jnp.concatenate(
        [w_in[:, :SWA_WIDTH][:, perm], w_in[:, SWA_WIDTH:o_q],
         w_in[:, o_q:o_q + SB_WIDTH] * (HEAD_DIM ** -0.5), w_in[:, o_q + SB_WIDTH:]], axis=1).astype(BF16)
    half = HEAD_DIM // 2
    inv_freq = ROPE_THETA ** (-jnp.arange(half, dtype=F32) * 2.0 / HEAD_DIM)
    invf = jnp.tile(inv_freq, LANES // half).reshape(1, LANES)
    pos_col = positions.reshape(n, 1).astype(F32)
    qa, ka, va, qs, ks, vs = _inproj(x2, ada, pos_col, invf, w_in_b, seq)

    head_of_block = np.asarray([c_ + per_group * g for g in range(SWA_KV_HEADS) for c_ in range(per_group)])
    sink_col = jnp.repeat(sinks.astype(F32)[head_of_block], WINDOW).reshape(SWA_HEADS * WINDOW, 1)
    oa = _swa(qa, ka, va, sink_col, batch, seq)
    ob = _sb(qs, ks, vs, batch, seq)

    w_out_b = jnp.concatenate([w_out[:SWA_WIDTH][perm], w_out[SWA_WIDTH:]], axis=0).astype(BF16)
    wr_t = w_router.T.astype(F32)
    wr_hi = wr_t.astype(BF16)
    wr_lo = (wr_t - wr_hi.astype(F32)).astype(BF16)
    x1, u2s, logits_t = _post(oa, ob, x2, ada, g_swa[perm].reshape(1, -1), g_sb.reshape(1, -1), w_out_b,
                              ln1_g.reshape(1, d), ln1_b.reshape(1, d), wr_hi, wr_lo, seq)

    lpos_t, gate_t, seglo, cnt, seen_b = _route(logits_t, e_bias)

    counts = seen_b[-1] + cnt[-1]
    padded = (counts + EXPERT_BLOCK - 1) // EXPERT_BLOCK * EXPERT_BLOCK
    pend = jnp.cumsum(padded).astype(I32)
    pstart = pend - padded
    n_rows = n * TOP_K + N_EXPERTS * EXPERT_BLOCK
    n_blocks = n_rows // EXPERT_BLOCK
    block_row0 = jnp.arange(n_blocks, dtype=I32) * EXPERT_BLOCK
    blk_e = jnp.minimum(jnp.sum((pend[None, :] <= block_row0[:, None]).astype(I32), axis=1), N_EXPERTS - 1)
    nused = (pend[-1:] // EXPERT_BLOCK).astype(I32)
    slabs = d // LANES
    loff = (lpos_t * slabs).T.reshape(-1)
    gate = gate_t.T.reshape(-1)
    seg_local = (seglo[::8] * slabs).reshape(-1)
    seg_table = ((pstart[None, :] + seen_b[::8]) * slabs).reshape(-1)
    seg_cnt = cnt[::8].reshape(-1)

    xs = _dispatch(pstart + counts, pend, nused, loff, seg_local, seg_table, seg_cnt, u2s, n_rows, slabs)
    ys = _experts(blk_e, nused, xs, w1, w3, w2, slabs)
    out = _final(loff, gate, seg_local, seg_table, seg_cnt, u2s, x1, ada, ws1.astype(BF16),
                 ws3.astype(BF16), ws2.astype(BF16), ln2_g.reshape(1, d), ln2_b.reshape(1, d), ys, seq, slabs)
    return out.reshape(batch, seq, d)


def kernel(x, c, positions, w_ada, b_ada, w_in, attn_sinks, g_swa, g_sb, w_out, ln1_g, ln1_b,
           w_router, e_bias, w1, w3, w2, ws1, ws3, ws2, ln2_g, ln2_b):
    assert w_ada.shape[0] == DEPTH
    for l in range(DEPTH):
        x = _layer(x, c, positions, w_ada[l], b_ada[l], w_in[l], attn_sinks[l], g_swa[l], g_sb[l],
                   w_out[l], ln1_g[l], ln1_b[l], w_router[l], e_bias[l], w1[l], w3[l], w2[l],
                   ws1[l], ws3[l], ws2[l], ln2_g[l], ln2_b[l])
    return x
```

```python
import functools

import numpy as np
import jax
import jax.numpy as jnp
from jax import lax
from jax.experimental import pallas as pl
from jax.experimental.pallas import tpu as pltpu

F32 = jnp.float32
BF16 = jnp.bfloat16
I32 = jnp.int32

HEAD_DIM = 64
SWA_HEADS = 8
SWA_KV_HEADS = 2
SB_HEADS = 8
SWA_WIDTH = SWA_HEADS * HEAD_DIM
SWA_KV_WIDTH = SWA_KV_HEADS * HEAD_DIM
SB_WIDTH = SB_HEADS * HEAD_DIM
WINDOW = 128
ROPE_THETA = 10000.0
N_EXPERTS = 256
TOP_K = 8
N_GROUPS = 8
TOPK_GROUPS = 4
GROUP_SIZE = N_EXPERTS // N_GROUPS
ROUTED_SCALE = 2.5
LN_EPS = 1e-5
DEPTH = 1
DEEPNORM_ALPHA = (2 * DEPTH) ** 0.25

LANES = 128
SB_BLOCK = 128
SB_TILE_BLOCKS = 3
SB_UNDERFLOW = 110.0
NEG_BIG = -1e30
VMEM_LIMIT = 56 * 1024 * 1024

TM_PROJ = 512
TQ_SWA = 512
T_MOE = 512
ROW_PAD = 128
EXPERT_BLOCK = 512
ZERO_ROWS = 128
SEGMENT_CHUNK = 8
MOE_ROW_UNROLL = 4


def _cparams(sem, vmem=VMEM_LIMIT):
    return pltpu.CompilerParams(dimension_semantics=sem, vmem_limit_bytes=vmem)


def _layer_norm(x):
    mu = jnp.mean(x, axis=-1, keepdims=True)
    xc = x - mu
    var = jnp.mean(xc * xc, axis=-1, keepdims=True)
    return xc * lax.rsqrt(var + LN_EPS)


def _silu(x):
    return x * jax.nn.sigmoid(x)


def _dot(a, b):
    return jnp.dot(a, b, preferred_element_type=F32)


def _dot_nt(a, b):
    return lax.dot_general(a, b, (((1,), (1,)), ((), ())), preferred_element_type=F32)


def _store_slab_rows(ref, x):
    t, d = x.shape
    slabs = d // LANES
    for s in range(slabs):
        ref[pl.ds(s, t, stride=slabs), :] = x[:, s * LANES:(s + 1) * LANES]


def _load_slab_rows(ref, slabs):
    t = ref.shape[0] // slabs
    return [ref[pl.ds(s, t, stride=slabs), :] for s in range(slabs)]


def _ada_kernel(c_ref, w_ref, b_ref, o_ref):
    sc = _silu(c_ref[...])
    o_ref[...] = _dot(sc.astype(BF16), w_ref[...].astype(BF16)) + b_ref[...]


def _ada(c, w_ada, b_ada):
    b, d = c.shape
    n_out = w_ada.shape[1]
    rows = 8
    c_pad = jnp.zeros((rows, d), F32).at[:b].set(c)
    out = pl.pallas_call(
        _ada_kernel,
        out_shape=jax.ShapeDtypeStruct((rows, n_out), F32),
        grid=(n_out // d,),
        in_specs=[
            pl.BlockSpec((rows, d), lambda j: (0, 0)),
            pl.BlockSpec((d, d), lambda j: (0, j)),
            pl.BlockSpec((1, d), lambda j: (0, j)),
        ],
        out_specs=pl.BlockSpec((rows, d), lambda j: (0, j)),
        compiler_params=_cparams(("arbitrary",)),
        name="ada",
    )(c_pad, w_ada, b_ada.reshape(1, n_out))
    return out[:b].reshape(b, n_out // d, d)


def _inproj_kernel(x_ref, ada_ref, pos_ref, invf_ref, w_ref,
                   qa_ref, ka_ref, va_ref, qs_ref, ks_ref, vs_ref):
    x = x_ref[...]
    shift = ada_ref[0, 0:1, :]
    scale = ada_ref[0, 1:2, :]
    u = _layer_norm(x) * (1.0 + scale) + shift
    h = _dot(u.astype(BF16), w_ref[...])

    ang = pos_ref[...] * invf_ref[...]
    cs = jnp.cos(ang)
    sn = jnp.sin(ang)
    lane = lax.broadcasted_iota(I32, (1, LANES), 1)
    first = (lane & (HEAD_DIM // 2)) == 0
    sn_signed = jnp.where(first, -sn, sn)

    def rope(hc):
        partner = jnp.where(first, pltpu.roll(hc, LANES - HEAD_DIM // 2, 1),
                            pltpu.roll(hc, HEAD_DIM // 2, 1))
        return hc * cs + partner * sn_signed

    q_scale = HEAD_DIM ** -0.5
    o = 0
    for c in range(SWA_WIDTH // LANES):
        qa_ref[:, c * LANES:(c + 1) * LANES] = (rope(h[:, o:o + LANES]) * q_scale).astype(BF16)
        o += LANES
    ka_ref[...] = rope(h[:, o:o + SWA_KV_WIDTH]).astype(BF16)
    o += SWA_KV_WIDTH
    va_ref[...] = h[:, o:o + SWA_KV_WIDTH].astype(BF16)
    o += SWA_KV_WIDTH
    qs_ref[...] = h[:, o:o + SB_WIDTH].astype(BF16)
    o += SB_WIDTH
    ks_ref[...] = h[:, o:o + SB_WIDTH].astype(BF16)
    o += SB_WIDTH
    vs_ref[...] = h[:, o:o + SB_WIDTH].astype(BF16)


def _inproj(x2, ada, pos_col, invf, w_in_b, seq):
    n, d = x2.shape
    tm = TM_PROJ
    steps_per_seq = seq // tm
    widths = (SWA_WIDTH, SWA_KV_WIDTH, SWA_KV_WIDTH, SB_WIDTH, SB_WIDTH, SB_WIDTH)
    return pl.pallas_call(
        _inproj_kernel,
        out_shape=[jax.ShapeDtypeStruct((n, w), BF16) for w in widths],
        grid=(n // tm,),
        in_specs=[
            pl.BlockSpec((tm, d), lambda i: (i, 0)),
            pl.BlockSpec((1,) + ada.shape[1:], lambda i: (i // steps_per_seq, 0, 0)),
            pl.BlockSpec((tm, 1), lambda i: (i, 0)),
            pl.BlockSpec((1, LANES), lambda i: (0, 0)),
            pl.BlockSpec(w_in_b.shape, lambda i: (0, 0)),
        ],
        out_specs=[pl.BlockSpec((tm, w), lambda i: (i, 0)) for w in widths],
        compiler_params=_cparams(("arbitrary",)),
        name="inproj",
    )(x2, ada, pos_col, invf, w_in_b)


def _swa_kernel(q_ref, kc_ref, kp_ref, vc_ref, vp_ref, bias0_ref, bias_ref, sink_ref,
                o_ref, kall, vall):
    tq = q_ref.shape[0]
    kall[0:WINDOW, :] = kp_ref[...]
    kall[WINDOW:, :] = kc_ref[...]
    vall[0:WINDOW, :] = vp_ref[...]
    vall[WINDOW:, :] = vc_ref[...]
    lane = lax.broadcasted_iota(I32, (1, LANES), 1)
    half0 = lane < HEAD_DIM
    half1 = lane >= HEAD_DIM
    sink = sink_ref[...]
    n_col = SWA_WIDTH // LANES
    for j in range(tq // WINDOW):
        r0 = j * WINDOW
        q = q_ref[r0:r0 + WINDOW, :]
        parts = []
        for half in (half0, half1):
            for c in range(n_col):
                parts.append(jnp.where(half, q[:, c * LANES:(c + 1) * LANES], jnp.zeros((), BF16)))
        qm = jnp.concatenate(parts, axis=0)
        kb = kall[r0:r0 + 2 * WINDOW, :]
        vb = vall[r0:r0 + 2 * WINDOW, :]
        s = _dot_nt(qm, kb)
        s = s + (bias0_ref[0] if j == 0 else bias_ref[...])
        m = jnp.maximum(jnp.max(s, axis=1, keepdims=True), sink)
        p = jnp.exp(s - m)
        den = jnp.sum(p, axis=1, keepdims=True) + jnp.exp(sink - m)
        o = _dot(p.astype(BF16), vb) / den
        for c in range(n_col):
            lo = o[c * WINDOW:(c + 1) * WINDOW]
            hi = o[(n_col + c) * WINDOW:(n_col + c + 1) * WINDOW]
            o_ref[r0:r0 + WINDOW, c * LANES:(c + 1) * LANES] = jnp.where(half0, lo, hi)


def _swa_bias():
    qi = np.arange(SWA_HEADS * WINDOW)[:, None] % WINDOW
    kj = np.arange(2 * WINDOW)[None, :]
    dist = qi + WINDOW - kj
    band = (dist >= 0) & (dist < WINDOW)
    bias = np.where(band, 0.0, NEG_BIG).astype(np.float32)
    first = np.where(band & (kj >= WINDOW), 0.0, NEG_BIG).astype(np.float32)
    return jnp.asarray(np.stack([bias, first])), jnp.asarray(bias)


def _swa(qa, ka, va, sink_col, batch, seq):
    n = qa.shape[0]
    tq = TQ_SWA
    nt = seq // tq
    blocks_per_tile = tq // WINDOW
    bias_pair, bias = _swa_bias()

    def cur(b, i):
        return (b * nt + i, 0)

    def prev(b, i):
        return (jnp.maximum((b * nt + i) * blocks_per_tile - 1, 0), 0)

    return pl.pallas_call(
        _swa_kernel,
        out_shape=jax.ShapeDtypeStruct((n, SWA_WIDTH), F32),
        grid=(batch, nt),
        in_specs=[
            pl.BlockSpec((tq, SWA_WIDTH), cur),
            pl.BlockSpec((tq, SWA_KV_WIDTH), cur),
            pl.BlockSpec((WINDOW, SWA_KV_WIDTH), prev),
            pl.BlockSpec((tq, SWA_KV_WIDTH), cur),
            pl.BlockSpec((WINDOW, SWA_KV_WIDTH), prev),
            pl.BlockSpec((1,) + bias.shape, lambda b, i: (jnp.where(i == 0, 1, 0), 0, 0)),
            pl.BlockSpec(bias.shape, lambda b, i: (0, 0)),
            pl.BlockSpec(sink_col.shape, lambda b, i: (0, 0)),
        ],
        out_specs=pl.BlockSpec((tq, SWA_WIDTH), cur),
        scratch_shapes=[pltpu.VMEM((tq + WINDOW, SWA_KV_WIDTH), BF16),
                        pltpu.VMEM((tq + WINDOW, SWA_KV_WIDTH), BF16)],
        compiler_params=_cparams(("arbitrary", "arbitrary")),
        name="swa",
    )(qa, ka, ka, va, va, bias_pair, bias, sink_col)


def _sb_kernel(q_ref, k_ref, v_ref, uu_ref, o_ref, qm, acc, run):
    i = pl.program_id(1)
    blk = SB_BLOCK
    n_sub = SB_TILE_BLOCKS
    tile = n_sub * blk
    pairs = q_ref.shape[1] // LANES
    lane = lax.broadcasted_iota(I32, (1, LANES), 1)
    half0 = lane < HEAD_DIM
    half1 = lane >= HEAD_DIM
    zero = jnp.zeros((), BF16)
    for p in range(pairs):
        q = q_ref[:, p * LANES:(p + 1) * LANES]
        qm[p] = jnp.concatenate([jnp.where(half0, q, zero), jnp.where(half1, q, zero)], axis=0)
    acc[...] = jnp.zeros_like(acc)
    run[...] = jnp.zeros_like(run)
    q_pos = i * blk + (lax.broadcasted_iota(I32, (2 * blk, blk), 0) & (blk - 1))
    col = lax.broadcasted_iota(I32, (2 * blk, blk), 1)

    def body(carry):
        end, _ = carry
        start = pl.multiple_of(jnp.maximum(end - tile, 0), blk)
        limit = jnp.minimum(end, q_pos) - start
        least = None
        for p in range(pairs):
            kt = k_ref[pl.ds(start, tile), p * LANES:(p + 1) * LANES]
            vt = v_ref[pl.ds(start, tile), p * LANES:(p + 1) * LANES]
            z = _dot_nt(qm[p], kt)
            mass_seen = run[p]
            ws = [None] * n_sub
            for b in reversed(range(n_sub)):
                zb = jnp.where(col < limit - b * blk, z[:, b * blk:(b + 1) * blk], NEG_BIG)
                mass = jnp.maximum(zb, 0.0) + jnp.log(1.0 + jnp.exp(-jnp.abs(zb)))
                mass_hi = mass.astype(BF16)
                mass_lo = (mass - mass_hi.astype(F32)).astype(BF16)
                cc = _dot(jnp.concatenate([mass_hi, mass_lo], axis=1), uu_ref[...])
                ws[b] = jnp.exp(zb - (cc[:, :blk] + mass_seen)).astype(BF16)
                mass_seen = mass_seen + cc[:, blk:]
            acc[p] += _dot(jnp.concatenate(ws, axis=1), vt)
            run[p] = mass_seen
            least = mass_seen if least is None else jnp.minimum(least, mass_seen)
        return start, jnp.min(least) > SB_UNDERFLOW

    lax.while_loop(lambda c: jnp.logical_and(c[0] > 0, jnp.logical_not(c[1])),
                   body, ((i + 1) * blk, jnp.bool_(False)))
    for p in range(pairs):
        a = acc[p]
        o_ref[:, p * LANES:(p + 1) * LANES] = jnp.where(half0, a[:blk], a[blk:])


def _sb_suffix_matrix():
    j = np.arange(2 * SB_BLOCK)[:, None] % SB_BLOCK
    s = np.arange(2 * SB_BLOCK)[None, :]
    m = np.where(s < SB_BLOCK, (j >= s), True)
    return jnp.asarray(m.astype(np.float32)).astype(BF16)


def _sb(qs, ks, vs, batch, seq):
    n = qs.shape[0]
    blk = SB_BLOCK
    nq = seq // blk
    pairs = SB_WIDTH // LANES
    uu = _sb_suffix_matrix()
    return pl.pallas_call(
        _sb_kernel,
        out_shape=jax.ShapeDtypeStruct((n, SB_WIDTH), F32),
        grid=(batch, nq),
        in_specs=[
            pl.BlockSpec((blk, SB_WIDTH), lambda b, i: (b * nq + i, 0)),
            pl.BlockSpec((seq, SB_WIDTH), lambda b, i: (b, 0)),
            pl.BlockSpec((seq, SB_WIDTH), lambda b, i: (b, 0)),
            pl.BlockSpec(uu.shape, lambda b, i: (0, 0)),
        ],
        out_specs=pl.BlockSpec((blk, SB_WIDTH), lambda b, i: (b * nq + i, 0)),
        scratch_shapes=[pltpu.VMEM((pairs, 2 * blk, LANES), BF16),
                        pltpu.VMEM((pairs, 2 * blk, LANES), F32),
                        pltpu.VMEM((pairs, 2 * blk, LANES), F32)],
        compiler_params=_cparams(("arbitrary", "arbitrary")),
        name="sb",
    )(qs, ks, vs, uu)


def _post_kernel(oa_ref, ob_ref, x_ref, ada_ref, ga_ref, gb_ref, wo_ref, g1_ref, b1_ref,
                 wrh_ref, wrl_ref, x1_ref, u2s_ref, lg_ref):
    def rms(o, g):
        return o * lax.rsqrt(jnp.mean(o * o, axis=-1, keepdims=True) + LN_EPS) * g

    na = rms(oa_ref[...], ga_ref[...]).astype(BF16)
    nb = rms(ob_ref[...], gb_ref[...]).astype(BF16)
    y = _dot(na, wo_ref[0:SWA_WIDTH, :]) + _dot(nb, wo_ref[SWA_WIDTH:, :])
    gate1 = ada_ref[0, 2:3, :]
    shift2 = ada_ref[0, 3:4, :]
    scale2 = ada_ref[0, 4:5, :]
    x1 = _layer_norm(DEEPNORM_ALPHA * x_ref[...] + (1.0 + gate1) * y) * g1_ref[...] + b1_ref[...]
    x1_ref[...] = x1
    u2 = _layer_norm(x1) * (1.0 + scale2) + shift2
    _store_slab_rows(u2s_ref, u2)
    u_hi = u2.astype(BF16)
    u_lo = (u2 - u_hi.astype(F32)).astype(BF16)
    wrh = wrh_ref[...]
    lg_ref[...] = _dot_nt(wrh, u_hi) + (_dot_nt(wrh, u_lo) + _dot_nt(wrl_ref[...], u_hi))


def _post(oa, ob, x2, ada, g_a, g_b, w_out_b, g1, b1, wr_hi, wr_lo, seq):
    n, d = x2.shape
    tm = TM_PROJ
    steps_per_seq = seq // tm
    e = wr_hi.shape[0]
    slabs = d // LANES
    const2 = lambda i: (0, 0)
    return pl.pallas_call(
        _post_kernel,
        out_shape=[jax.ShapeDtypeStruct((n, d), F32),
                   jax.ShapeDtypeStruct((n * slabs, LANES), F32),
                   jax.ShapeDtypeStruct((e, n), F32)],
        grid=(n // tm,),
        in_specs=[
            pl.BlockSpec((tm, SWA_WIDTH), lambda i: (i, 0)),
            pl.BlockSpec((tm, SB_WIDTH), lambda i: (i, 0)),
            pl.BlockSpec((tm, d), lambda i: (i, 0)),
            pl.BlockSpec((1,) + ada.shape[1:], lambda i: (i // steps_per_seq, 0, 0)),
            pl.BlockSpec(g_a.shape, const2),
            pl.BlockSpec(g_b.shape, const2),
            pl.BlockSpec(w_out_b.shape, const2),
            pl.BlockSpec(g1.shape, const2),
            pl.BlockSpec(b1.shape, const2),
            pl.BlockSpec(wr_hi.shape, const2),
            pl.BlockSpec(wr_lo.shape, const2),
        ],
        out_specs=[pl.BlockSpec((tm, d), lambda i: (i, 0)),
                   pl.BlockSpec((tm * slabs, LANES), lambda i: (i, 0)),
                   pl.BlockSpec((e, tm), lambda i: (0, i))],
        compiler_params=_cparams(("arbitrary",)),
        name="post",
    )(oa, ob, x2, ada, g_a, g_b, w_out_b, g1, b1, wr_hi, wr_lo)


def _route_kernel(lg_ref, bias_ref, su_ref, sl_ref, lpos_ref, gate_ref, seglo_ref, cnt_ref, seenb_ref,
                  seen_row):
    i = pl.program_id(0)

    @pl.when(i == 0)
    def _():
        seen_row[...] = jnp.zeros_like(seen_row)

    tt = lg_ref.shape[1]
    ninf = -jnp.inf
    scores = jax.nn.sigmoid(lg_ref[...])
    biased = scores + bias_ref[...]

    iog32 = lax.broadcasted_iota(I32, (GROUP_SIZE, tt), 0)
    groups = [biased[g * GROUP_SIZE:(g + 1) * GROUP_SIZE, :] for g in range(N_GROUPS)]
    gs_rows = []
    for blk in groups:
        m1 = jnp.max(blk, axis=0, keepdims=True)
        i1 = jnp.min(jnp.where(blk == m1, iog32, GROUP_SIZE), axis=0, keepdims=True)
        m2 = jnp.max(jnp.where(iog32 == i1, ninf, blk), axis=0, keepdims=True)
        gs_rows.append(m1 + m2)
    gs = jnp.concatenate(gs_rows, axis=0)

    iog = lax.broadcasted_iota(I32, gs.shape, 0)
    gsel = jnp.zeros(gs.shape, F32)
    cur = gs
    for _ in range(TOPK_GROUPS):
        m = jnp.max(cur, axis=0, keepdims=True)
        idx = jnp.min(jnp.where(cur == m, iog, N_GROUPS), axis=0, keepdims=True)
        hit = iog == idx
        gsel = jnp.where(hit, 1.0, gsel)
        cur = jnp.where(hit, ninf, cur)

    cand = jnp.concatenate(
        [jnp.where(gsel[g:g + 1, :] > 0.5, groups[g], ninf) for g in range(N_GROUPS)], axis=0)
    ioe = lax.broadcasted_iota(I32, cand.shape, 0)
    chosen = jnp.zeros(cand.shape, F32)
    idxs, gates = [], []
    for _ in range(TOP_K):
        m = jnp.max(cand, axis=0, keepdims=True)
        idx = jnp.min(jnp.where(cand == m, ioe, N_EXPERTS), axis=0, keepdims=True)
        hit = ioe == idx
        gates.append(jnp.sum(jnp.where(hit, scores, 0.0), axis=0, keepdims=True))
        cand = jnp.where(hit, ninf, cand)
        chosen = jnp.where(hit, 1.0, chosen)
        idxs.append(idx)
    gsum = gates[0]
    for g in gates[1:]:
        gsum = gsum + g
    gates = [g / gsum * ROUTED_SCALE for g in gates]

    def hi_lo(x):
        hi = x.astype(BF16)
        return hi, (x - hi.astype(F32)).astype(BF16)

    chosen_b = chosen.astype(BF16)
    rk = _dot(chosen_b, su_ref[...])
    c_hi, c_lo = hi_lo(rk[:, tt:])
    first_col = _dot(sl_ref[...], c_hi) + _dot(sl_ref[...], c_lo)
    slot_mat = rk[:, :tt] + jnp.concatenate([first_col] * (tt // LANES), axis=1)
    slots = [jnp.sum(jnp.where(ioe == idx, slot_mat, 0.0), axis=0, keepdims=True) for idx in idxs]
    lpos_ref[...] = jnp.concatenate(slots, axis=0).astype(I32)
    gate_ref[...] = jnp.concatenate(gates, axis=0)
    cnt_row = _dot_nt(jnp.ones((8, tt), BF16), chosen_b)
    r_hi, r_lo = hi_lo(cnt_row)
    before_row = seen_row[...]
    seen_row[...] = before_row + cnt_row
    seglo_ref[...] = (_dot_nt(r_hi, sl_ref[...]) + _dot_nt(r_lo, sl_ref[...])).astype(I32)
    cnt_ref[...] = cnt_row.astype(I32)
    seenb_ref[...] = before_row.astype(I32)


def _route_prefix_matrix(tt):
    a = np.arange(tt)[:, None] < np.arange(tt)[None, :]
    m = np.concatenate([a, np.ones((tt, LANES), bool)], axis=1)
    return jnp.asarray(m.astype(np.float32)).astype(BF16)


def _route(logits_t, e_bias):
    e, n = logits_t.shape
    tt = T_MOE
    bias_b = jnp.broadcast_to(e_bias.astype(F32)[:, None], (e, tt))
    su = _route_prefix_matrix(tt)
    sl = jnp.asarray((np.arange(e)[None, :] < np.arange(e)[:, None]).astype(np.float32)).astype(BF16)
    tok = lambda i: (0, i)
    per_tile = jax.ShapeDtypeStruct((n // tt * 8, e), I32)
    per_tile_spec = pl.BlockSpec((8, e), lambda i: (i, 0))
    return pl.pallas_call(
        _route_kernel,
        out_shape=[jax.ShapeDtypeStruct((TOP_K, n), I32),
                   jax.ShapeDtypeStruct((TOP_K, n), F32),
                   per_tile, per_tile, per_tile],
        grid=(n // tt,),
        in_specs=[pl.BlockSpec((e, tt), tok),
                  pl.BlockSpec((e, tt), lambda i: (0, 0)),
                  pl.BlockSpec(su.shape, lambda i: (0, 0)),
                  pl.BlockSpec(sl.shape, lambda i: (0, 0))],
        out_specs=[pl.BlockSpec((TOP_K, tt), tok), pl.BlockSpec((TOP_K, tt), tok),
                   per_tile_spec, per_tile_spec, per_tile_spec],
        scratch_shapes=[pltpu.VMEM((8, e), F32)],
        compiler_params=_cparams(("arbitrary",)),
        name="route",
    )(logits_t, bias_b, su, sl)


def _slab(ref, row, slabs):
    return ref.at[pl.ds(pl.multiple_of(row * slabs, slabs), slabs)]


def _start_segment(e, local_ref, table_ref, cnt_ref, make_copy, slabs):
    cnt = cnt_ref[e]
    local = local_ref[e]
    table = table_ref[e]
    n_chunks = lax.shift_right_logical(cnt, SEGMENT_CHUNK.bit_length() - 1)
    step = SEGMENT_CHUNK * slabs

    def chunk(j, c):
        make_copy(local + j * step, table + j * step, SEGMENT_CHUNK).start()
        return c

    lax.fori_loop(0, n_chunks, chunk, 0)
    done = n_chunks * step
    rows = SEGMENT_CHUNK // 2
    while rows >= 1:
        take = (cnt & rows) != 0

        @pl.when(take)
        def _(done=done, rows=rows):
            make_copy(local + done, table + done, rows).start()

        done = done + jnp.where(take, rows * slabs, 0)
        rows //= 2


def _dispatch_kernel(pvalid_ref, pend_ref, nused_ref, loff_ref, seg_local_ref, seg_table_ref, seg_cnt_ref,
                     u_ref, xs_ref, obuf, zbuf, sems, sem, *, slabs):
    i = pl.program_id(0)
    last = pl.num_programs(0) - 1
    td = u_ref.shape[0] // slabs
    zrows = zbuf.shape[0] // slabs
    chunks_per_pad = ROW_PAD // zrows
    n_chunks = xs_ref.shape[0] // slabs // zrows

    def zero_rows(row0, rows):
        return pltpu.make_async_copy(
            zbuf.at[pl.ds(0, rows * slabs)],
            xs_ref.at[pl.ds(pl.multiple_of(row0 * slabs, slabs), rows * slabs)], sem)

    def zero_fill(wait):
        def go(copy):
            if wait:
                copy.wait()
            else:
                copy.start()

        def unused(b, c):
            go(zero_rows(b * zrows, zrows))
            return c

        lax.fori_loop(nused_ref[0], n_chunks, unused, 0)

        def padding(e, c):
            first = pvalid_ref[e]
            end = pend_ref[e]
            for g in range(chunks_per_pad):
                row0 = end - zrows * (g + 1)

                @pl.when(row0 >= first)
                def _(row0=row0):
                    go(zero_rows(row0, zrows))
            rem = (end - first) & (zrows - 1)
            done = jnp.int32(0)
            rows = zrows // 2
            while rows >= 1:
                take = (rem & rows) != 0

                @pl.when(take)
                def _(done=done, rows=rows):
                    go(zero_rows(first + done, rows))

                done = done + jnp.where(take, rows, 0)
                rows //= 2
            return c

        lax.fori_loop(0, N_EXPERTS, padding, 0)

    @pl.when(i == 0)
    def _():
        zbuf[...] = jnp.zeros_like(zbuf)
        zero_fill(wait=False)

    slot = i % 2
    buf = obuf.at[slot]
    out_sem = sems.at[slot]

    def drained(b, s):
        return pltpu.make_async_copy(obuf.at[b], xs_ref.at[pl.ds(0, obuf.shape[1])], sems.at[s])

    @pl.when(i >= 2)
    def _():
        drained(slot, slot).wait()

    def place(t, c):
        row = _slab(u_ref, t, slabs)[...]
        for k in range(TOP_K):
            buf[pl.ds(pl.multiple_of(loff_ref[t * TOP_K + k], slabs), slabs), :] = row
        return c

    lax.fori_loop(0, td, place, 0, unroll=MOE_ROW_UNROLL)

    def to_table(local, table, rows):
        return pltpu.make_async_copy(
            buf.at[pl.ds(pl.multiple_of(local, slabs), rows * slabs)],
            xs_ref.at[pl.ds(pl.multiple_of(table, slabs), rows * slabs)], out_sem)

    def send(e, c):
        _start_segment(e, seg_local_ref, seg_table_ref, seg_cnt_ref, to_table, slabs)
        return c

    lax.fori_loop(0, N_EXPERTS, send, 0)

    @pl.when(i == last)
    def _():
        drained(slot, slot).wait()

        @pl.when(last >= 1)
        def _():
            drained(1 - slot, 1 - slot).wait()

        zero_fill(wait=True)


def _dispatch(pvalid, pend, nused, loff, seg_local, seg_table, seg_cnt, u2s, n_rows, slabs):
    n = u2s.shape[0] // slabs
    td = T_MOE
    smem_tok = pl.BlockSpec((TOP_K * td,), lambda i, *_: (i,), memory_space=pltpu.SMEM)
    smem_tile = pl.BlockSpec((N_EXPERTS,), lambda i, *_: (i,), memory_space=pltpu.SMEM)
    return pl.pallas_call(
        functools.partial(_dispatch_kernel, slabs=slabs),
        out_shape=jax.ShapeDtypeStruct((n_rows * slabs, LANES), F32),
        grid_spec=pltpu.PrefetchScalarGridSpec(
            num_scalar_prefetch=3,
            grid=(n // td,),
            in_specs=[smem_tok, smem_tile, smem_tile, smem_tile,
                      pl.BlockSpec((td * slabs, LANES), lambda i, *_: (i, 0))],
            out_specs=pl.BlockSpec(memory_space=pl.ANY),
            scratch_shapes=[pltpu.VMEM((2, TOP_K * td * slabs, LANES), F32),
                            pltpu.VMEM((ZERO_ROWS * slabs, LANES), F32),
                            pltpu.SemaphoreType.DMA((2,)),
                            pltpu.SemaphoreType.DMA],
        ),
        compiler_params=_cparams(("arbitrary",)),
        name="dispatch",
    )(pvalid, pend, nused, loff, seg_local, seg_table, seg_cnt, u2s)


def _expert_kernel(step_e_ref, nsteps_ref, first_ref, slot_ref, next_e_ref, row0_ref, nsub_ref, tail_ref,
                   xs_ref, w1_hbm, w3_hbm, w2_hbm, ys_ref,
                   xbuf, ybuf, zbuf, wf1, wf3, wf2, w1b, w3b, w2b, in_sems, out_sems, w_sems, z_sem, *, slabs):
    i = pl.program_id(0)
    nsteps = nsteps_ref[0]
    sub = ROW_PAD * slabs
    subs_per_window = EXPERT_BLOCK // ROW_PAD
    n_tail = ys_ref.shape[0] // sub

    def weights(e, s):
        return [pltpu.make_async_copy(src.at[e], dst.at[s], w_sems.at[s])
                for src, dst in ((w1_hbm, wf1), (w3_hbm, wf3), (w2_hbm, wf2))]

    def rows_in(step, s, j):
        off = pl.multiple_of(row0_ref[step] * slabs + j * sub, sub)
        return pltpu.make_async_copy(xs_ref.at[pl.ds(off, sub)], xbuf.at[s, pl.ds(j * sub, sub)], in_sems.at[s])

    def rows_out(step, s, j):
        off = pl.multiple_of(row0_ref[step] * slabs + j * sub, sub)
        return pltpu.make_async_copy(ybuf.at[s, pl.ds(j * sub, sub)], ys_ref.at[pl.ds(off, sub)], out_sems.at[s])

    def each_live(make, step, s, wait):
        for j in range(subs_per_window):
            @pl.when(j < nsub_ref[step])
            def _(j=j):
                if wait:
                    make(step, s, j).wait()
                else:
                    make(step, s, j).start()

    def zero_tail(wait):
        def body(c, carry):
            copy = pltpu.make_async_copy(zbuf, ys_ref.at[pl.ds(pl.multiple_of(c * sub, sub), sub)], z_sem)
            if wait:
                copy.wait()
            else:
                copy.start()
            return carry

        lax.fori_loop(tail_ref[0], n_tail, body, 0)

    @pl.when(i < nsteps)
    def _():
        s = i % 2

        @pl.when(i == 0)
        def _():
            xbuf[...] = jnp.zeros_like(xbuf)
            zbuf[...] = jnp.zeros_like(zbuf)
            zero_tail(wait=False)
            each_live(rows_in, i, s, wait=False)

        each_live(rows_in, i, s, wait=True)

        @pl.when(i + 1 < nsteps)
        def _():
            each_live(rows_in, i + 1, 1 - s, wait=False)

        @pl.when(first_ref[i] == 1)
        def _():
            e = step_e_ref[i]
            ws = slot_ref[i]

            @pl.when(i == 0)
            def _():
                for c in weights(e, ws):
                    c.start()

            for c in weights(e, ws):
                c.wait()
            w1b[...] = wf1[ws].astype(BF16)
            w3b[...] = wf3[ws].astype(BF16)
            w2b[...] = wf2[ws].astype(BF16)

            @pl.when(next_e_ref[i] >= 0)
            def _():
                for c in weights(next_e_ref[i], 1 - ws):
                    c.start()

        @pl.when(i >= 2)
        def _():
            each_live(rows_out, i - 2, s, wait=True)

        x = jnp.concatenate([c.astype(BF16) for c in _load_slab_rows(xbuf.at[s], slabs)], axis=1)
        h1 = _dot(x, w1b[...])
        h3 = _dot(x, w3b[...])
        a = (_silu(h1) * h3).astype(BF16)
        _store_slab_rows(ybuf.at[s], _dot(a, w2b[...]))
        each_live(rows_out, i, s, wait=False)

        @pl.when(i == nsteps - 1)
        def _():
            each_live(rows_out, i, s, wait=True)

            @pl.when(i >= 1)
            def _():
                each_live(rows_out, i - 1, 1 - s, wait=True)

            zero_tail(wait=True)


def _experts(pstart, pend, xs, w1, w3, w2, slabs):
    n_rows = xs.shape[0] // slabs
    e, d, f = w1.shape
    padded = pend - pstart
    steps_e = (padded + EXPERT_BLOCK - 1) // EXPERT_BLOCK
    step_end = jnp.cumsum(steps_e).astype(I32)
    step_start = step_end - steps_e
    ns = -(-n_rows // EXPERT_BLOCK) + e
    nsteps = step_end[-1:]
    step = jnp.arange(ns, dtype=I32)
    step_e = jnp.minimum(jnp.sum((step_end[None, :] <= step[:, None]).astype(I32), axis=1), e - 1)
    live = step < nsteps[0]
    row0 = jnp.where(live, pstart[step_e] + EXPERT_BLOCK * (step - step_start[step_e]), 0).astype(I32)
    nsub = jnp.where(live, jnp.clip((pend[step_e] - row0) // ROW_PAD, 0, EXPERT_BLOCK // ROW_PAD), 0).astype(I32)
    first = ((step == 0) | (step_e != jnp.roll(step_e, 1))) & live
    slot = (jnp.cumsum(first.astype(I32)) - 1) & 1
    opens_at = jnp.where(first, step, ns)
    next_open = jnp.concatenate([lax.cummin(opens_at, reverse=True)[1:], jnp.full((1,), ns, I32)])
    next_e = jnp.where(next_open < ns, step_e[jnp.minimum(next_open, ns - 1)], -1).astype(I32)
    tail = (pend[-1:] // ROW_PAD).astype(I32)

    any_space = pl.BlockSpec(memory_space=pl.ANY)
    window = EXPERT_BLOCK * slabs
    return pl.pallas_call(
        functools.partial(_expert_kernel, slabs=slabs),
        out_shape=jax.ShapeDtypeStruct((n_rows * slabs, LANES), F32),
        grid_spec=pltpu.PrefetchScalarGridSpec(
            num_scalar_prefetch=8,
            grid=(ns,),
            in_specs=[any_space, any_space, any_space, any_space],
            out_specs=any_space,
            scratch_shapes=[pltpu.VMEM((2, window, LANES), F32), pltpu.VMEM((2, window, LANES), F32),
                            pltpu.VMEM((ROW_PAD * slabs, LANES), F32),
                            pltpu.VMEM((2, d, f), F32), pltpu.VMEM((2, d, f), F32),
                            pltpu.VMEM((2, f, d), F32),
                            pltpu.VMEM((d, f), BF16), pltpu.VMEM((d, f), BF16),
                            pltpu.VMEM((f, d), BF16),
                            pltpu.SemaphoreType.DMA((2,)), pltpu.SemaphoreType.DMA((2,)),
                            pltpu.SemaphoreType.DMA((2,)), pltpu.SemaphoreType.DMA],
        ),
        compiler_params=_cparams(("arbitrary",)),
        name="experts",
    )(step_e, nsteps, first.astype(I32), slot.astype(I32), next_e, row0, nsub, tail, xs, w1, w3, w2)


def _final_kernel(loff_ref, gate_ref, seg_local, seg_table, seg_cnt, seg_local_nx, seg_table_nx, seg_cnt_nx,
                  u_ref, x1_ref, ada_ref, ws1_ref, ws3_ref, ws2_ref, g2_ref, b2_ref, ys_ref, o_ref,
                  lbuf, rbuf, sem, *, slabs):
    i = pl.program_id(0)
    last = pl.num_programs(0) - 1
    tf = u_ref.shape[0] // slabs

    def from_table(local, table, rows):
        return pltpu.make_async_copy(
            ys_ref.at[pl.ds(pl.multiple_of(table, slabs), rows * slabs)],
            lbuf.at[pl.ds(pl.multiple_of(local, slabs), rows * slabs)], sem)

    def fetch_tile(local_r, table_r, cnt_r):
        def fetch(e, c):
            _start_segment(e, local_r, table_r, cnt_r, from_table, slabs)
            return c

        lax.fori_loop(0, N_EXPERTS, fetch, 0)

    @pl.when(i == 0)
    def _():
        fetch_tile(seg_local, seg_table, seg_cnt)

    pltpu.make_async_copy(ys_ref.at[pl.ds(0, lbuf.shape[0])], lbuf, sem).wait()

    def combine(t, c):
        rows = [lbuf[pl.ds(pl.multiple_of(loff_ref[t * TOP_K + k], slabs), slabs), :] * gate_ref[t * TOP_K + k]
                for k in range(TOP_K)]
        while len(rows) > 1:
            rows = [a + b for a, b in zip(rows[0::2], rows[1::2])]
        _slab(rbuf, t, slabs)[...] = rows[0]
        return c

    lax.fori_loop(0, tf, combine, 0, unroll=MOE_ROW_UNROLL)

    @pl.when(i < last)
    def _():
        fetch_tile(seg_local_nx, seg_table_nx, seg_cnt_nx)

    u = jnp.concatenate([c.astype(BF16) for c in _load_slab_rows(u_ref, slabs)], axis=1)
    shared = _dot((_silu(_dot(u, ws1_ref[...])) * _dot(u, ws3_ref[...])).astype(BF16), ws2_ref[...])
    routed = jnp.concatenate(_load_slab_rows(rbuf, slabs), axis=1)
    gate2 = ada_ref[0, 5:6, :]
    y = shared + routed
    o_ref[...] = _layer_norm(DEEPNORM_ALPHA * x1_ref[...] + (1.0 + gate2) * y) * g2_ref[...] + b2_ref[...]


def _final(loff, gate, seg_local, seg_table, seg_cnt, u2s, x1, ada, ws1b, ws3b, ws2b, g2, b2, ys,
           seq, slabs):
    n, d = x1.shape
    tf = T_MOE
    n_tiles = n // tf
    steps_per_seq = seq // tf
    smem_tok = pl.BlockSpec((TOP_K * tf,), lambda i, *_: (i,), memory_space=pltpu.SMEM)
    smem_tile = pl.BlockSpec((N_EXPERTS,), lambda i, *_: (i,), memory_space=pltpu.SMEM)
    smem_next = pl.BlockSpec((N_EXPERTS,), lambda i, *_: (jnp.minimum(i + 1, n_tiles - 1),),
                             memory_space=pltpu.SMEM)
    const2 = lambda i, *_: (0, 0)
    return pl.pallas_call(
        functools.partial(_final_kernel, slabs=slabs),
        out_shape=jax.ShapeDtypeStruct((n, d), F32),
        grid_spec=pltpu.PrefetchScalarGridSpec(
            num_scalar_prefetch=0,
            grid=(n_tiles,),
            in_specs=[smem_tok, smem_tok, smem_tile, smem_tile, smem_tile, smem_next, smem_next, smem_next,
                      pl.BlockSpec((tf * slabs, LANES), lambda i, *_: (i, 0)),
                      pl.BlockSpec((tf, d), lambda i, *_: (i, 0)),
                      pl.BlockSpec((1,) + ada.shape[1:], lambda i, *_: (i // steps_per_seq, 0, 0)),
                      pl.BlockSpec(ws1b.shape, const2),
                      pl.BlockSpec(ws3b.shape, const2),
                      pl.BlockSpec(ws2b.shape, const2),
                      pl.BlockSpec(g2.shape, const2),
                      pl.BlockSpec(b2.shape, const2),
                      pl.BlockSpec(memory_space=pl.ANY)],
            out_specs=pl.BlockSpec((tf, d), lambda i, *_: (i, 0)),
            scratch_shapes=[pltpu.VMEM((TOP_K * tf * slabs, LANES), F32),
                            pltpu.VMEM((tf * slabs, LANES), F32),
                            pltpu.SemaphoreType.DMA],
        ),
        compiler_params=_cparams(("arbitrary",)),
        name="final",
    )(loff, gate, seg_local, seg_table, seg_cnt, seg_local, seg_table, seg_cnt, u2s, x1, ada, ws1b, ws3b,
      ws2b, g2, b2, ys)


def _swa_head_permutation():
    per_group = SWA_HEADS // SWA_KV_HEADS
    cols = []
    for c in range(per_group):
        for g in range(SWA_KV_HEADS):
            h = c + per_group * g
            cols.extend(range(h * HEAD_DIM, (h + 1) * HEAD_DIM))
    return np.asarray(cols)


def _layer(x, c, positions, w_ada, b_ada, w_in, sinks, g_swa, g_sb, w_out, ln1_g, ln1_b,
           w_router, e_bias, w1, w3, w2, ws1, ws3, ws2, ln2_g, ln2_b):
    batch, seq, d = x.shape
    n = batch * seq
    x2 = x.reshape(n, d)
    perm = _swa_head_permutation()
    per_group = SWA_HEADS // SWA_KV_HEADS

    ada = _ada(c, w_ada, b_ada)

    o_q = SWA_WIDTH + 2 * SWA_KV_WIDTH
    w_in_b = jnp.concatenate(
        [w_in[:, :SWA_WIDTH][:, perm], w_in[:, SWA_WIDTH:o_q],
         w_in[:, o_q:o_q + SB_WIDTH] * (HEAD_DIM ** -0.5), w_in[:, o_q + SB_WIDTH:]], axis=1).astype(BF16)
    half = HEAD_DIM // 2
    inv_freq = ROPE_THETA ** (-jnp.arange(half, dtype=F32) * 2.0 / HEAD_DIM)
    invf = jnp.tile(inv_freq, LANES // half).reshape(1, LANES)
    pos_col = positions.reshape(n, 1).astype(F32)
    qa, ka, va, qs, ks, vs = _inproj(x2, ada, pos_col, invf, w_in_b, seq)

    head_of_block = np.asarray([c_ + per_group * g for g in range(SWA_KV_HEADS) for c_ in range(per_group)])
    sink_col = jnp.repeat(sinks.astype(F32)[head_of_block], WINDOW).reshape(SWA_HEADS * WINDOW, 1)
    oa = _swa(qa, ka, va, sink_col, batch, seq)
    ob = _sb(qs, ks, vs, batch, seq)

    w_out_b = jnp.concatenate([w_out[:SWA_WIDTH][perm], w_out[SWA_WIDTH:]], axis=0).astype(BF16)
    wr_t = w_router.T.astype(F32)
    wr_hi = wr_t.astype(BF16)
    wr_lo = (wr_t - wr_hi.astype(F32)).astype(BF16)
    x1, u2s, logits_t = _post(oa, ob, x2, ada, g_swa[perm].reshape(1, -1), g_sb.reshape(1, -1), w_out_b,
                              ln1_g.reshape(1, d), ln1_b.reshape(1, d), wr_hi, wr_lo, seq)

    lpos_t, gate_t, seglo, cnt, seen_b = _route(logits_t, e_bias)

    counts = seen_b[-1] + cnt[-1]
    padded = (counts + ROW_PAD - 1) // ROW_PAD * ROW_PAD
    pend = jnp.cumsum(padded).astype(I32)
    pstart = pend - padded
    n_rows = n * TOP_K + N_EXPERTS * ROW_PAD
    nused = (pend[-1:] // ZERO_ROWS).astype(I32)
    slabs = d // LANES
    loff = (lpos_t * slabs).T.reshape(-1)
    gate = gate_t.T.reshape(-1)
    seg_local = (seglo[::8] * slabs).reshape(-1)
    seg_table = ((pstart[None, :] + seen_b[::8]) * slabs).reshape(-1)
    seg_cnt = cnt[::8].reshape(-1)

    xs = _dispatch(pstart + counts, pend, nused, loff, seg_local, seg_table, seg_cnt, u2s, n_rows, slabs)
    ys = _experts(pstart, pend, xs, w1, w3, w2, slabs)
    out = _final(loff, gate, seg_local, seg_table, seg_cnt, u2s, x1, ada, ws1.astype(BF16),
                 ws3.astype(BF16), ws2.astype(BF16), ln2_g.reshape(1, d), ln2_b.reshape(1, d), ys, seq, slabs)
    return out.reshape(batch, seq, d)


def kernel(x, c, positions, w_ada, b_ada, w_in, attn_sinks, g_swa, g_sb, w_out, ln1_g, ln1_b,
           w_router, e_bias, w1, w3, w2, ws1, ws3, ws2, ln2_g, ln2_b):
    assert w_ada.shape[0] == DEPTH
    for l in range(DEPTH):
        x = _layer(x, c, positions, w_ada[l], b_ada[l], w_in[l], attn_sinks[l], g_swa[l], g_sb[l],
                   w_out[l], ln1_g[l], ln1_b[l], w_router[l], e_bias[l], w1[l], w3[l], w2[l],
                   ws1[l], ws3[l], ws2[l], ln2_g[l], ln2_b[l])
    return x
```

```python
import functools

import numpy as np
import jax
import jax.numpy as jnp
from jax import lax
from jax.experimental import pallas as pl
from jax.experimental.pallas import tpu as pltpu

F32 = jnp.float32
BF16 = jnp.bfloat16
I32 = jnp.int32

HEAD_DIM = 64
SWA_HEADS = 8
SWA_KV_HEADS = 2
SB_HEADS = 8
SWA_WIDTH = SWA_HEADS * HEAD_DIM
SWA_KV_WIDTH = SWA_KV_HEADS * HEAD_DIM
SB_WIDTH = SB_HEADS * HEAD_DIM
WINDOW = 128
ROPE_THETA = 10000.0
N_EXPERTS = 256
TOP_K = 8
N_GROUPS = 8
TOPK_GROUPS = 4
GROUP_SIZE = N_EXPERTS // N_GROUPS
ROUTED_SCALE = 2.5
LN_EPS = 1e-5
DEPTH = 1
DEEPNORM_ALPHA = (2 * DEPTH) ** 0.25

LANES = 128
SB_BLOCK = 128
SB_TILE_BLOCKS = 3
SB_UNDERFLOW = 110.0
NEG_BIG = -1e30
VMEM_LIMIT = 56 * 1024 * 1024

TM_PROJ = 512
TQ_SWA = 512
T_MOE = 512
ROW_PAD = 128
EXPERT_BLOCK = 512
ZERO_ROWS = 128
SEND_CHUNK = 8
FETCH_CHUNK = 16
MOE_ROW_UNROLL = 4


def _cparams(sem, vmem=VMEM_LIMIT):
    return pltpu.CompilerParams(dimension_semantics=sem, vmem_limit_bytes=vmem)


def _layer_norm(x):
    mu = jnp.mean(x, axis=-1, keepdims=True)
    xc = x - mu
    var = jnp.mean(xc * xc, axis=-1, keepdims=True)
    return xc * lax.rsqrt(var + LN_EPS)


def _silu(x):
    return x * jax.nn.sigmoid(x)


def _dot(a, b):
    return jnp.dot(a, b, preferred_element_type=F32)


def _dot_nt(a, b):
    return lax.dot_general(a, b, (((1,), (1,)), ((), ())), preferred_element_type=F32)


def _store_slab_rows(ref, x):
    t, d = x.shape
    slabs = d // LANES
    for s in range(slabs):
        ref[pl.ds(s, t, stride=slabs), :] = x[:, s * LANES:(s + 1) * LANES]


def _load_slab_rows(ref, slabs):
    t = ref.shape[0] // slabs
    return [ref[pl.ds(s, t, stride=slabs), :] for s in range(slabs)]


def _ada_kernel(c_ref, w_ref, b_ref, o_ref):
    sc = _silu(c_ref[...])
    o_ref[...] = _dot(sc.astype(BF16), w_ref[...].astype(BF16)) + b_ref[...]


def _ada(c, w_ada, b_ada):
    b, d = c.shape
    n_out = w_ada.shape[1]
    rows = 8
    c_pad = jnp.zeros((rows, d), F32).at[:b].set(c)
    out = pl.pallas_call(
        _ada_kernel,
        out_shape=jax.ShapeDtypeStruct((rows, n_out), F32),
        grid=(n_out // d,),
        in_specs=[
            pl.BlockSpec((rows, d), lambda j: (0, 0)),
            pl.BlockSpec((d, d), lambda j: (0, j)),
            pl.BlockSpec((1, d), lambda j: (0, j)),
        ],
        out_specs=pl.BlockSpec((rows, d), lambda j: (0, j)),
        compiler_params=_cparams(("arbitrary",)),
        name="ada",
    )(c_pad, w_ada, b_ada.reshape(1, n_out))
    return out[:b].reshape(b, n_out // d, d)


def _inproj_kernel(x_ref, ada_ref, pos_ref, invf_ref, w_ref,
                   qa_ref, ka_ref, va_ref, qs_ref, ks_ref, vs_ref):
    x = x_ref[...]
    shift = ada_ref[0, 0:1, :]
    scale = ada_ref[0, 1:2, :]
    u = _layer_norm(x) * (1.0 + scale) + shift
    h = _dot(u.astype(BF16), w_ref[...])

    ang = pos_ref[...] * invf_ref[...]
    cs = jnp.cos(ang)
    sn = jnp.sin(ang)
    lane = lax.broadcasted_iota(I32, (1, LANES), 1)
    first = (lane & (HEAD_DIM // 2)) == 0
    sn_signed = jnp.where(first, -sn, sn)

    def rope(hc):
        partner = jnp.where(first, pltpu.roll(hc, LANES - HEAD_DIM // 2, 1),
                            pltpu.roll(hc, HEAD_DIM // 2, 1))
        return hc * cs + partner * sn_signed

    q_scale = HEAD_DIM ** -0.5
    o = 0
    for c in range(SWA_WIDTH // LANES):
        qa_ref[:, c * LANES:(c + 1) * LANES] = (rope(h[:, o:o + LANES]) * q_scale).astype(BF16)
        o += LANES
    ka_ref[...] = rope(h[:, o:o + SWA_KV_WIDTH]).astype(BF16)
    o += SWA_KV_WIDTH
    va_ref[...] = h[:, o:o + SWA_KV_WIDTH].astype(BF16)
    o += SWA_KV_WIDTH
    qs_ref[...] = h[:, o:o + SB_WIDTH].astype(BF16)
    o += SB_WIDTH
    ks_ref[...] = h[:, o:o + SB_WIDTH].astype(BF16)
    o += SB_WIDTH
    vs_ref[...] = h[:, o:o + SB_WIDTH].astype(BF16)


def _inproj(x2, ada, pos_col, invf, w_in_b, seq):
    n, d = x2.shape
    tm = TM_PROJ
    steps_per_seq = seq // tm
    widths = (SWA_WIDTH, SWA_KV_WIDTH, SWA_KV_WIDTH, SB_WIDTH, SB_WIDTH, SB_WIDTH)
    return pl.pallas_call(
        _inproj_kernel,
        out_shape=[jax.ShapeDtypeStruct((n, w), BF16) for w in widths],
        grid=(n // tm,),
        in_specs=[
            pl.BlockSpec((tm, d), lambda i: (i, 0)),
            pl.BlockSpec((1,) + ada.shape[1:], lambda i: (i // steps_per_seq, 0, 0)),
            pl.BlockSpec((tm, 1), lambda i: (i, 0)),
            pl.BlockSpec((1, LANES), lambda i: (0, 0)),
            pl.BlockSpec(w_in_b.shape, lambda i: (0, 0)),
        ],
        out_specs=[pl.BlockSpec((tm, w), lambda i: (i, 0)) for w in widths],
        compiler_params=_cparams(("arbitrary",)),
        name="inproj",
    )(x2, ada, pos_col, invf, w_in_b)


def _swa_kernel(q_ref, kc_ref, kp_ref, vc_ref, vp_ref, bias0_ref, bias_ref, sink_ref,
                o_ref, kall, vall):
    tq = q_ref.shape[0]
    kall[0:WINDOW, :] = kp_ref[...]
    kall[WINDOW:, :] = kc_ref[...]
    vall[0:WINDOW, :] = vp_ref[...]
    vall[WINDOW:, :] = vc_ref[...]
    lane = lax.broadcasted_iota(I32, (1, LANES), 1)
    half0 = lane < HEAD_DIM
    half1 = lane >= HEAD_DIM
    sink = sink_ref[...]
    n_col = SWA_WIDTH // LANES
    for j in range(tq // WINDOW):
        r0 = j * WINDOW
        q = q_ref[r0:r0 + WINDOW, :]
        parts = []
        for half in (half0, half1):
            for c in range(n_col):
                parts.append(jnp.where(half, q[:, c * LANES:(c + 1) * LANES], jnp.zeros((), BF16)))
        qm = jnp.concatenate(parts, axis=0)
        kb = kall[r0:r0 + 2 * WINDOW, :]
        vb = vall[r0:r0 + 2 * WINDOW, :]
        s = _dot_nt(qm, kb)
        s = s + (bias0_ref[0] if j == 0 else bias_ref[...])
        m = jnp.maximum(jnp.max(s, axis=1, keepdims=True), sink)
        p = jnp.exp(s - m)
        den = jnp.sum(p, axis=1, keepdims=True) + jnp.exp(sink - m)
        o = _dot(p.astype(BF16), vb) / den
        for c in range(n_col):
            lo = o[c * WINDOW:(c + 1) * WINDOW]
            hi = o[(n_col + c) * WINDOW:(n_col + c + 1) * WINDOW]
            o_ref[r0:r0 + WINDOW, c * LANES:(c + 1) * LANES] = jnp.where(half0, lo, hi)


def _swa_bias():
    qi = np.arange(SWA_HEADS * WINDOW)[:, None] % WINDOW
    kj = np.arange(2 * WINDOW)[None, :]
    dist = qi + WINDOW - kj
    band = (dist >= 0) & (dist < WINDOW)
    bias = np.where(band, 0.0, NEG_BIG).astype(np.float32)
    first = np.where(band & (kj >= WINDOW), 0.0, NEG_BIG).astype(np.float32)
    return jnp.asarray(np.stack([bias, first])), jnp.asarray(bias)


def _swa(qa, ka, va, sink_col, batch, seq):
    n = qa.shape[0]
    tq = TQ_SWA
    nt = seq // tq
    blocks_per_tile = tq // WINDOW
    bias_pair, bias = _swa_bias()

    def cur(b, i):
        return (b * nt + i, 0)

    def prev(b, i):
        return (jnp.maximum((b * nt + i) * blocks_per_tile - 1, 0), 0)

    return pl.pallas_call(
        _swa_kernel,
        out_shape=jax.ShapeDtypeStruct((n, SWA_WIDTH), F32),
        grid=(batch, nt),
        in_specs=[
            pl.BlockSpec((tq, SWA_WIDTH), cur),
            pl.BlockSpec((tq, SWA_KV_WIDTH), cur),
            pl.BlockSpec((WINDOW, SWA_KV_WIDTH), prev),
            pl.BlockSpec((tq, SWA_KV_WIDTH), cur),
            pl.BlockSpec((WINDOW, SWA_KV_WIDTH), prev),
            pl.BlockSpec((1,) + bias.shape, lambda b, i: (jnp.where(i == 0, 1, 0), 0, 0)),
            pl.BlockSpec(bias.shape, lambda b, i: (0, 0)),
            pl.BlockSpec(sink_col.shape, lambda b, i: (0, 0)),
        ],
        out_specs=pl.BlockSpec((tq, SWA_WIDTH), cur),
        scratch_shapes=[pltpu.VMEM((tq + WINDOW, SWA_KV_WIDTH), BF16),
                        pltpu.VMEM((tq + WINDOW, SWA_KV_WIDTH), BF16)],
        compiler_params=_cparams(("arbitrary", "arbitrary")),
        name="swa",
    )(qa, ka, ka, va, va, bias_pair, bias, sink_col)


def _sb_kernel(q_ref, k_ref, v_ref, uu_ref, o_ref, qm, acc, run):
    i = pl.program_id(1)
    blk = SB_BLOCK
    n_sub = SB_TILE_BLOCKS
    tile = n_sub * blk
    pairs = q_ref.shape[1] // LANES
    lane = lax.broadcasted_iota(I32, (1, LANES), 1)
    half0 = lane < HEAD_DIM
    half1 = lane >= HEAD_DIM
    zero = jnp.zeros((), BF16)
    for p in range(pairs):
        q = q_ref[:, p * LANES:(p + 1) * LANES]
        qm[p] = jnp.concatenate([jnp.where(half0, q, zero), jnp.where(half1, q, zero)], axis=0)
    acc[...] = jnp.zeros_like(acc)
    run[...] = jnp.zeros_like(run)
    q_pos = i * blk + (lax.broadcasted_iota(I32, (2 * blk, blk), 0) & (blk - 1))
    col = lax.broadcasted_iota(I32, (2 * blk, blk), 1)

    def body(carry):
        end, _ = carry
        start = pl.multiple_of(jnp.maximum(end - tile, 0), blk)
        limit = jnp.minimum(end, q_pos) - start
        least = None
        for p in range(pairs):
            kt = k_ref[pl.ds(start, tile), p * LANES:(p + 1) * LANES]
            vt = v_ref[pl.ds(start, tile), p * LANES:(p + 1) * LANES]
            z = _dot_nt(qm[p], kt)
            mass_seen = run[p]
            ws = [None] * n_sub
            for b in reversed(range(n_sub)):
                zb = jnp.where(col < limit - b * blk, z[:, b * blk:(b + 1) * blk], NEG_BIG)
                mass = jnp.maximum(zb, 0.0) + jnp.log(1.0 + jnp.exp(-jnp.abs(zb)))
                mass_hi = mass.astype(BF16)
                mass_lo = (mass - mass_hi.astype(F32)).astype(BF16)
                cc = _dot(jnp.concatenate([mass_hi, mass_lo], axis=1), uu_ref[...])
                ws[b] = jnp.exp(zb - (cc[:, :blk] + mass_seen)).astype(BF16)
                mass_seen = mass_seen + cc[:, blk:]
            acc[p] += _dot(jnp.concatenate(ws, axis=1), vt)
            run[p] = mass_seen
            least = mass_seen if least is None else jnp.minimum(least, mass_seen)
        return start, jnp.min(least) > SB_UNDERFLOW

    lax.while_loop(lambda c: jnp.logical_and(c[0] > 0, jnp.logical_not(c[1])),
                   body, ((i + 1) * blk, jnp.bool_(False)))
    for p in range(pairs):
        a = acc[p]
        o_ref[:, p * LANES:(p + 1) * LANES] = jnp.where(half0, a[:blk], a[blk:])


def _sb_suffix_matrix():
    j = np.arange(2 * SB_BLOCK)[:, None] % SB_BLOCK
    s = np.arange(2 * SB_BLOCK)[None, :]
    m = np.where(s < SB_BLOCK, (j >= s), True)
    return jnp.asarray(m.astype(np.float32)).astype(BF16)


def _sb(qs, ks, vs, batch, seq):
    n = qs.shape[0]
    blk = SB_BLOCK
    nq = seq // blk
    pairs = SB_WIDTH // LANES
    uu = _sb_suffix_matrix()
    return pl.pallas_call(
        _sb_kernel,
        out_shape=jax.ShapeDtypeStruct((n, SB_WIDTH), F32),
        grid=(batch, nq),
        in_specs=[
            pl.BlockSpec((blk, SB_WIDTH), lambda b, i: (b * nq + i, 0)),
            pl.BlockSpec((seq, SB_WIDTH), lambda b, i: (b, 0)),
            pl.BlockSpec((seq, SB_WIDTH), lambda b, i: (b, 0)),
            pl.BlockSpec(uu.shape, lambda b, i: (0, 0)),
        ],
        out_specs=pl.BlockSpec((blk, SB_WIDTH), lambda b, i: (b * nq + i, 0)),
        scratch_shapes=[pltpu.VMEM((pairs, 2 * blk, LANES), BF16),
                        pltpu.VMEM((pairs, 2 * blk, LANES), F32),
                        pltpu.VMEM((pairs, 2 * blk, LANES), F32)],
        compiler_params=_cparams(("arbitrary", "arbitrary")),
        name="sb",
    )(qs, ks, vs, uu)


def _post_kernel(oa_ref, ob_ref, x_ref, ada_ref, ga_ref, gb_ref, wo_ref, g1_ref, b1_ref,
                 wrh_ref, wrl_ref, x1_ref, u2s_ref, lg_ref):
    def rms(o, g):
        return o * lax.rsqrt(jnp.mean(o * o, axis=-1, keepdims=True) + LN_EPS) * g

    na = rms(oa_ref[...], ga_ref[...]).astype(BF16)
    nb = rms(ob_ref[...], gb_ref[...]).astype(BF16)
    y = _dot(na, wo_ref[0:SWA_WIDTH, :]) + _dot(nb, wo_ref[SWA_WIDTH:, :])
    gate1 = ada_ref[0, 2:3, :]
    shift2 = ada_ref[0, 3:4, :]
    scale2 = ada_ref[0, 4:5, :]
    x1 = _layer_norm(DEEPNORM_ALPHA * x_ref[...] + (1.0 + gate1) * y) * g1_ref[...] + b1_ref[...]
    x1_ref[...] = x1
    u2 = _layer_norm(x1) * (1.0 + scale2) + shift2
    _store_slab_rows(u2s_ref, u2)
    u_hi = u2.astype(BF16)
    u_lo = (u2 - u_hi.astype(F32)).astype(BF16)
    wrh = wrh_ref[...]
    lg_ref[...] = _dot_nt(wrh, u_hi) + (_dot_nt(wrh, u_lo) + _dot_nt(wrl_ref[...], u_hi))


def _post(oa, ob, x2, ada, g_a, g_b, w_out_b, g1, b1, wr_hi, wr_lo, seq):
    n, d = x2.shape
    tm = TM_PROJ
    steps_per_seq = seq // tm
    e = wr_hi.shape[0]
    slabs = d // LANES
    const2 = lambda i: (0, 0)
    return pl.pallas_call(
        _post_kernel,
        out_shape=[jax.ShapeDtypeStruct((n, d), F32),
                   jax.ShapeDtypeStruct((n * slabs, LANES), F32),
                   jax.ShapeDtypeStruct((e, n), F32)],
        grid=(n // tm,),
        in_specs=[
            pl.BlockSpec((tm, SWA_WIDTH), lambda i: (i, 0)),
            pl.BlockSpec((tm, SB_WIDTH), lambda i: (i, 0)),
            pl.BlockSpec((tm, d), lambda i: (i, 0)),
            pl.BlockSpec((1,) + ada.shape[1:], lambda i: (i // steps_per_seq, 0, 0)),
            pl.BlockSpec(g_a.shape, const2),
            pl.BlockSpec(g_b.shape, const2),
            pl.BlockSpec(w_out_b.shape, const2),
            pl.BlockSpec(g1.shape, const2),
            pl.BlockSpec(b1.shape, const2),
            pl.BlockSpec(wr_hi.shape, const2),
            pl.BlockSpec(wr_lo.shape, const2),
        ],
        out_specs=[pl.BlockSpec((tm, d), lambda i: (i, 0)),
                   pl.BlockSpec((tm * slabs, LANES), lambda i: (i, 0)),
                   pl.BlockSpec((e, tm), lambda i: (0, i))],
        compiler_params=_cparams(("arbitrary",)),
        name="post",
    )(oa, ob, x2, ada, g_a, g_b, w_out_b, g1, b1, wr_hi, wr_lo)


def _route_kernel(lg_ref, bias_ref, su_ref, sl_ref, lpos_ref, gate_ref, seglo_ref, cnt_ref, seenb_ref,
                  seen_row):
    i = pl.program_id(0)

    @pl.when(i == 0)
    def _():
        seen_row[...] = jnp.zeros_like(seen_row)

    tt = lg_ref.shape[1]
    ninf = -jnp.inf
    scores = jax.nn.sigmoid(lg_ref[...])
    biased = scores + bias_ref[...]

    iog32 = lax.broadcasted_iota(I32, (GROUP_SIZE, tt), 0)
    groups = [biased[g * GROUP_SIZE:(g + 1) * GROUP_SIZE, :] for g in range(N_GROUPS)]
    gs_rows = []
    for blk in groups:
        m1 = jnp.max(blk, axis=0, keepdims=True)
        i1 = jnp.min(jnp.where(blk == m1, iog32, GROUP_SIZE), axis=0, keepdims=True)
        m2 = jnp.max(jnp.where(iog32 == i1, ninf, blk), axis=0, keepdims=True)
        gs_rows.append(m1 + m2)
    gs = jnp.concatenate(gs_rows, axis=0)

    iog = lax.broadcasted_iota(I32, gs.shape, 0)
    gsel = jnp.zeros(gs.shape, F32)
    cur = gs
    for _ in range(TOPK_GROUPS):
        m = jnp.max(cur, axis=0, keepdims=True)
        idx = jnp.min(jnp.where(cur == m, iog, N_GROUPS), axis=0, keepdims=True)
        hit = iog == idx
        gsel = jnp.where(hit, 1.0, gsel)
        cur = jnp.where(hit, ninf, cur)

    cand = jnp.concatenate(
        [jnp.where(gsel[g:g + 1, :] > 0.5, groups[g], ninf) for g in range(N_GROUPS)], axis=0)
    ioe = lax.broadcasted_iota(I32, cand.shape, 0)
    chosen = jnp.zeros(cand.shape, F32)
    idxs, gates = [], []
    for _ in range(TOP_K):
        m = jnp.max(cand, axis=0, keepdims=True)
        idx = jnp.min(jnp.where(cand == m, ioe, N_EXPERTS), axis=0, keepdims=True)
        hit = ioe == idx
        gates.append(jnp.sum(jnp.where(hit, scores, 0.0), axis=0, keepdims=True))
        cand = jnp.where(hit, ninf, cand)
        chosen = jnp.where(hit, 1.0, chosen)
        idxs.append(idx)
    gsum = gates[0]
    for g in gates[1:]:
        gsum = gsum + g
    gates = [g / gsum * ROUTED_SCALE for g in gates]

    def hi_lo(x):
        hi = x.astype(BF16)
        return hi, (x - hi.astype(F32)).astype(BF16)

    chosen_b = chosen.astype(BF16)
    rk = _dot(chosen_b, su_ref[...])
    c_hi, c_lo = hi_lo(rk[:, tt:])
    first_col = _dot(sl_ref[...], c_hi) + _dot(sl_ref[...], c_lo)
    slot_mat = rk[:, :tt] + jnp.concatenate([first_col] * (tt // LANES), axis=1)
    slots = [jnp.sum(jnp.where(ioe == idx, slot_mat, 0.0), axis=0, keepdims=True) for idx in idxs]
    lpos_ref[...] = jnp.concatenate(slots, axis=0).astype(I32)
    gate_ref[...] = jnp.concatenate(gates, axis=0)
    cnt_row = _dot_nt(jnp.ones((8, tt), BF16), chosen_b)
    r_hi, r_lo = hi_lo(cnt_row)
    before_row = seen_row[...]
    seen_row[...] = before_row + cnt_row
    seglo_ref[...] = (_dot_nt(r_hi, sl_ref[...]) + _dot_nt(r_lo, sl_ref[...])).astype(I32)
    cnt_ref[...] = cnt_row.astype(I32)
    seenb_ref[...] = before_row.astype(I32)


def _route_prefix_matrix(tt):
    a = np.arange(tt)[:, None] < np.arange(tt)[None, :]
    m = np.concatenate([a, np.ones((tt, LANES), bool)], axis=1)
    return jnp.asarray(m.astype(np.float32)).astype(BF16)


def _route(logits_t, e_bias):
    e, n = logits_t.shape
    tt = T_MOE
    bias_b = jnp.broadcast_to(e_bias.astype(F32)[:, None], (e, tt))
    su = _route_prefix_matrix(tt)
    sl = jnp.asarray((np.arange(e)[None, :] < np.arange(e)[:, None]).astype(np.float32)).astype(BF16)
    tok = lambda i: (0, i)
    per_tile = jax.ShapeDtypeStruct((n // tt * 8, e), I32)
    per_tile_spec = pl.BlockSpec((8, e), lambda i: (i, 0))
    return pl.pallas_call(
        _route_kernel,
        out_shape=[jax.ShapeDtypeStruct((TOP_K, n), I32),
                   jax.ShapeDtypeStruct((TOP_K, n), F32),
                   per_tile, per_tile, per_tile],
        grid=(n // tt,),
        in_specs=[pl.BlockSpec((e, tt), tok),
                  pl.BlockSpec((e, tt), lambda i: (0, 0)),
                  pl.BlockSpec(su.shape, lambda i: (0, 0)),
                  pl.BlockSpec(sl.shape, lambda i: (0, 0))],
        out_specs=[pl.BlockSpec((TOP_K, tt), tok), pl.BlockSpec((TOP_K, tt), tok),
                   per_tile_spec, per_tile_spec, per_tile_spec],
        scratch_shapes=[pltpu.VMEM((8, e), F32)],
        compiler_params=_cparams(("arbitrary",)),
        name="route",
    )(logits_t, bias_b, su, sl)


def _slab(ref, row, slabs):
    return ref.at[pl.ds(pl.multiple_of(row * slabs, slabs), slabs)]


def _start_segment(e, local_ref, table_ref, cnt_ref, make_copy, slabs, chunk_rows):
    cnt = cnt_ref[e]
    local = local_ref[e]
    table = table_ref[e]
    n_chunks = lax.shift_right_logical(cnt, chunk_rows.bit_length() - 1)
    step = chunk_rows * slabs

    def chunk(j, c):
        make_copy(local + j * step, table + j * step, chunk_rows).start()
        return c

    lax.fori_loop(0, n_chunks, chunk, 0)
    done = n_chunks * step
    rows = chunk_rows // 2
    while rows >= 1:
        take = (cnt & rows) != 0

        @pl.when(take)
        def _(done=done, rows=rows):
            make_copy(local + done, table + done, rows).start()

        done = done + jnp.where(take, rows * slabs, 0)
        rows //= 2


def _dispatch_kernel(pvalid_ref, pend_ref, nused_ref, loff_ref, seg_local_ref, seg_table_ref, seg_cnt_ref,
                     u_ref, xs_ref, obuf, zbuf, sems, sem, *, slabs):
    i = pl.program_id(0)
    last = pl.num_programs(0) - 1
    td = u_ref.shape[0] // slabs
    zrows = zbuf.shape[0] // slabs
    chunks_per_pad = ROW_PAD // zrows
    n_chunks = xs_ref.shape[0] // slabs // zrows

    def zero_rows(row0, rows):
        return pltpu.make_async_copy(
            zbuf.at[pl.ds(0, rows * slabs)],
            xs_ref.at[pl.ds(pl.multiple_of(row0 * slabs, slabs), rows * slabs)], sem)

    def zero_fill(wait):
        def go(copy):
            if wait:
                copy.wait()
            else:
                copy.start()

        def unused(b, c):
            go(zero_rows(b * zrows, zrows))
            return c

        lax.fori_loop(nused_ref[0], n_chunks, unused, 0)

        def padding(e, c):
            first = pvalid_ref[e]
            end = pend_ref[e]
            for g in range(chunks_per_pad):
                row0 = end - zrows * (g + 1)

                @pl.when(row0 >= first)
                def _(row0=row0):
                    go(zero_rows(row0, zrows))
            rem = (end - first) & (zrows - 1)
            done = jnp.int32(0)
            rows = zrows // 2
            while rows >= 1:
                take = (rem & rows) != 0

                @pl.when(take)
                def _(done=done, rows=rows):
                    go(zero_rows(first + done, rows))

                done = done + jnp.where(take, rows, 0)
                rows //= 2
            return c

        lax.fori_loop(0, N_EXPERTS, padding, 0)

    @pl.when(i == 0)
    def _():
        zbuf[...] = jnp.zeros_like(zbuf)
        zero_fill(wait=False)

    slot = i % 2
    buf = obuf.at[slot]
    out_sem = sems.at[slot]

    def drained(b, s):
        return pltpu.make_async_copy(obuf.at[b], xs_ref.at[pl.ds(0, obuf.shape[1])], sems.at[s])

    @pl.when(i >= 2)
    def _():
        drained(slot, slot).wait()

    def place(t, c):
        row = _slab(u_ref, t, slabs)[...]
        for k in range(TOP_K):
            buf[pl.ds(pl.multiple_of(loff_ref[t * TOP_K + k], slabs), slabs), :] = row
        return c

    lax.fori_loop(0, td, place, 0, unroll=MOE_ROW_UNROLL)

    def to_table(local, table, rows):
        return pltpu.make_async_copy(
            buf.at[pl.ds(pl.multiple_of(local, slabs), rows * slabs)],
            xs_ref.at[pl.ds(pl.multiple_of(table, slabs), rows * slabs)], out_sem)

    def send(e, c):
        _start_segment(e, seg_local_ref, seg_table_ref, seg_cnt_ref, to_table, slabs, SEND_CHUNK)
        return c

    lax.fori_loop(0, N_EXPERTS, send, 0)

    @pl.when(i == last)
    def _():
        drained(slot, slot).wait()

        @pl.when(last >= 1)
        def _():
            drained(1 - slot, 1 - slot).wait()

        zero_fill(wait=True)


def _dispatch(pvalid, pend, nused, loff, seg_local, seg_table, seg_cnt, u2s, n_rows, slabs):
    n = u2s.shape[0] // slabs
    td = T_MOE
    smem_tok = pl.BlockSpec((TOP_K * td,), lambda i, *_: (i,), memory_space=pltpu.SMEM)
    smem_tile = pl.BlockSpec((N_EXPERTS,), lambda i, *_: (i,), memory_space=pltpu.SMEM)
    return pl.pallas_call(
        functools.partial(_dispatch_kernel, slabs=slabs),
        out_shape=jax.ShapeDtypeStruct((n_rows * slabs, LANES), F32),
        grid_spec=pltpu.PrefetchScalarGridSpec(
            num_scalar_prefetch=3,
            grid=(n // td,),
            in_specs=[smem_tok, smem_tile, smem_tile, smem_tile,
                      pl.BlockSpec((td * slabs, LANES), lambda i, *_: (i, 0))],
            out_specs=pl.BlockSpec(memory_space=pl.ANY),
            scratch_shapes=[pltpu.VMEM((2, TOP_K * td * slabs, LANES), F32),
                            pltpu.VMEM((ZERO_ROWS * slabs, LANES), F32),
                            pltpu.SemaphoreType.DMA((2,)),
                            pltpu.SemaphoreType.DMA],
        ),
        compiler_params=_cparams(("arbitrary",)),
        name="dispatch",
    )(pvalid, pend, nused, loff, seg_local, seg_table, seg_cnt, u2s)


def _expert_kernel(step_e_ref, nsteps_ref, first_ref, slot_ref, next_e_ref, row0_ref, nsub_ref, tail_ref,
                   xs_ref, w1_hbm, w3_hbm, w2_hbm, ys_ref,
                   xbuf, ybuf, zbuf, wf1, wf3, wf2, w1b, w3b, w2b, in_sems, out_sems, w_sems, z_sem, *, slabs):
    i = pl.program_id(0)
    nsteps = nsteps_ref[0]
    sub = ROW_PAD * slabs
    subs_per_window = EXPERT_BLOCK // ROW_PAD
    n_tail = ys_ref.shape[0] // sub

    def weights(e, s):
        return [pltpu.make_async_copy(src.at[e], dst.at[s], w_sems.at[s])
                for src, dst in ((w1_hbm, wf1), (w3_hbm, wf3), (w2_hbm, wf2))]

    def rows_in(step, s, j):
        off = pl.multiple_of(row0_ref[step] * slabs + j * sub, sub)
        return pltpu.make_async_copy(xs_ref.at[pl.ds(off, sub)], xbuf.at[s, pl.ds(j * sub, sub)], in_sems.at[s])

    def rows_out(step, s, j):
        off = pl.multiple_of(row0_ref[step] * slabs + j * sub, sub)
        return pltpu.make_async_copy(ybuf.at[s, pl.ds(j * sub, sub)], ys_ref.at[pl.ds(off, sub)], out_sems.at[s])

    def each_live(make, step, s, wait):
        for j in range(subs_per_window):
            @pl.when(j < nsub_ref[step])
            def _(j=j):
                if wait:
                    make(step, s, j).wait()
                else:
                    make(step, s, j).start()

    def zero_tail(wait):
        def body(c, carry):
            copy = pltpu.make_async_copy(zbuf, ys_ref.at[pl.ds(pl.multiple_of(c * sub, sub), sub)], z_sem)
            if wait:
                copy.wait()
            else:
                copy.start()
            return carry

        lax.fori_loop(tail_ref[0], n_tail, body, 0)

    @pl.when(i < nsteps)
    def _():
        s = i % 2

        @pl.when(i == 0)
        def _():
            xbuf[...] = jnp.zeros_like(xbuf)
            zbuf[...] = jnp.zeros_like(zbuf)
            zero_tail(wait=False)
            each_live(rows_in, i, s, wait=False)

        each_live(rows_in, i, s, wait=True)

        @pl.when(i + 1 < nsteps)
        def _():
            each_live(rows_in, i + 1, 1 - s, wait=False)

        @pl.when(first_ref[i] == 1)
        def _():
            e = step_e_ref[i]
            ws = slot_ref[i]

            @pl.when(i == 0)
            def _():
                for c in weights(e, ws):
                    c.start()

            for c in weights(e, ws):
                c.wait()
            w1b[...] = wf1[ws].astype(BF16)
            w3b[...] = wf3[ws].astype(BF16)
            w2b[...] = wf2[ws].astype(BF16)

            @pl.when(next_e_ref[i] >= 0)
            def _():
                for c in weights(next_e_ref[i], 1 - ws):
                    c.start()

        @pl.when(i >= 2)
        def _():
            each_live(rows_out, i - 2, s, wait=True)

        x = jnp.concatenate([c.astype(BF16) for c in _load_slab_rows(xbuf.at[s], slabs)], axis=1)
        h1 = _dot(x, w1b[...])
        h3 = _dot(x, w3b[...])
        a = (_silu(h1) * h3).astype(BF16)
        _store_slab_rows(ybuf.at[s], _dot(a, w2b[...]))
        each_live(rows_out, i, s, wait=False)

        @pl.when(i == nsteps - 1)
        def _():
            each_live(rows_out, i, s, wait=True)

            @pl.when(i >= 1)
            def _():
                each_live(rows_out, i - 1, 1 - s, wait=True)

            zero_tail(wait=True)


def _experts(pstart, pend, xs, w1, w3, w2, slabs):
    n_rows = xs.shape[0] // slabs
    e, d, f = w1.shape
    padded = pend - pstart
    steps_e = (padded + EXPERT_BLOCK - 1) // EXPERT_BLOCK
    step_end = jnp.cumsum(steps_e).astype(I32)
    step_start = step_end - steps_e
    ns = -(-n_rows // EXPERT_BLOCK) + -(-e * (EXPERT_BLOCK - ROW_PAD) // EXPERT_BLOCK)
    nsteps = step_end[-1:]
    step = jnp.arange(ns, dtype=I32)
    step_e = jnp.minimum(jnp.sum((step_end[None, :] <= step[:, None]).astype(I32), axis=1), e - 1)
    live = step < nsteps[0]
    row0 = jnp.where(live, pstart[step_e] + EXPERT_BLOCK * (step - step_start[step_e]), 0).astype(I32)
    nsub = jnp.where(live, jnp.clip((pend[step_e] - row0) // ROW_PAD, 0, EXPERT_BLOCK // ROW_PAD), 0).astype(I32)
    first = ((step == 0) | (step_e != jnp.roll(step_e, 1))) & live
    slot = (jnp.cumsum(first.astype(I32)) - 1) & 1
    opens_at = jnp.where(first, step, ns)
    next_open = jnp.concatenate([lax.cummin(opens_at, reverse=True)[1:], jnp.full((1,), ns, I32)])
    next_e = jnp.where(next_open < ns, step_e[jnp.minimum(next_open, ns - 1)], -1).astype(I32)
    tail = (pend[-1:] // ROW_PAD).astype(I32)

    any_space = pl.BlockSpec(memory_space=pl.ANY)
    window = EXPERT_BLOCK * slabs
    return pl.pallas_call(
        functools.partial(_expert_kernel, slabs=slabs),
        out_shape=jax.ShapeDtypeStruct((n_rows * slabs, LANES), F32),
        grid_spec=pltpu.PrefetchScalarGridSpec(
            num_scalar_prefetch=8,
            grid=(ns,),
            in_specs=[any_space, any_space, any_space, any_space],
            out_specs=any_space,
            scratch_shapes=[pltpu.VMEM((2, window, LANES), F32), pltpu.VMEM((2, window, LANES), F32),
                            pltpu.VMEM((ROW_PAD * slabs, LANES), F32),
                            pltpu.VMEM((2, d, f), F32), pltpu.VMEM((2, d, f), F32),
                            pltpu.VMEM((2, f, d), F32),
                            pltpu.VMEM((d, f), BF16), pltpu.VMEM((d, f), BF16),
                            pltpu.VMEM((f, d), BF16),
                            pltpu.SemaphoreType.DMA((2,)), pltpu.SemaphoreType.DMA((2,)),
                            pltpu.SemaphoreType.DMA((2,)), pltpu.SemaphoreType.DMA],
        ),
        compiler_params=_cparams(("arbitrary",)),
        name="experts",
    )(step_e, nsteps, first.astype(I32), slot.astype(I32), next_e, row0, nsub, tail, xs, w1, w3, w2)


def _final_kernel(loff_ref, gate_ref, seg_local, seg_table, seg_cnt, seg_local_nx, seg_table_nx, seg_cnt_nx,
                  u_ref, x1_ref, ada_ref, ws1_ref, ws3_ref, ws2_ref, g2_ref, b2_ref, ys_ref, o_ref,
                  lbuf, rbuf, sem, *, slabs):
    i = pl.program_id(0)
    last = pl.num_programs(0) - 1
    tf = u_ref.shape[0] // slabs

    def from_table(local, table, rows):
        return pltpu.make_async_copy(
            ys_ref.at[pl.ds(pl.multiple_of(table, slabs), rows * slabs)],
            lbuf.at[pl.ds(pl.multiple_of(local, slabs), rows * slabs)], sem)

    def fetch_tile(local_r, table_r, cnt_r):
        def fetch(e, c):
            _start_segment(e, local_r, table_r, cnt_r, from_table, slabs, FETCH_CHUNK)
            return c

        lax.fori_loop(0, N_EXPERTS, fetch, 0)

    @pl.when(i == 0)
    def _():
        fetch_tile(seg_local, seg_table, seg_cnt)

    pltpu.make_async_copy(ys_ref.at[pl.ds(0, lbuf.shape[0])], lbuf, sem).wait()

    def combine(t, c):
        rows = [lbuf[pl.ds(pl.multiple_of(loff_ref[t * TOP_K + k], slabs), slabs), :] * gate_ref[t * TOP_K + k]
                for k in range(TOP_K)]
        while len(rows) > 1:
            rows = [a + b for a, b in zip(rows[0::2], rows[1::2])]
        _slab(rbuf, t, slabs)[...] = rows[0]
        return c

    lax.fori_loop(0, tf, combine, 0, unroll=MOE_ROW_UNROLL)

    @pl.when(i < last)
    def _():
        fetch_tile(seg_local_nx, seg_table_nx, seg_cnt_nx)

    u = jnp.concatenate([c.astype(BF16) for c in _load_slab_rows(u_ref, slabs)], axis=1)
    shared = _dot((_silu(_dot(u, ws1_ref[...])) * _dot(u, ws3_ref[...])).astype(BF16), ws2_ref[...])
    routed = jnp.concatenate(_load_slab_rows(rbuf, slabs), axis=1)
    gate2 = ada_ref[0, 5:6, :]
    y = shared + routed
    o_ref[...] = _layer_norm(DEEPNORM_ALPHA * x1_ref[...] + (1.0 + gate2) * y) * g2_ref[...] + b2_ref[...]


def _final(loff, gate, seg_local, seg_table, seg_cnt, u2s, x1, ada, ws1b, ws3b, ws2b, g2, b2, ys,
           seq, slabs):
    n, d = x1.shape
    tf = T_MOE
    n_tiles = n // tf
    steps_per_seq = seq // tf
    smem_tok = pl.BlockSpec((TOP_K * tf,), lambda i, *_: (i,), memory_space=pltpu.SMEM)
    smem_tile = pl.BlockSpec((N_EXPERTS,), lambda i, *_: (i,), memory_space=pltpu.SMEM)
    smem_next = pl.BlockSpec((N_EXPERTS,), lambda i, *_: (jnp.minimum(i + 1, n_tiles - 1),),
                             memory_space=pltpu.SMEM)
    const2 = lambda i, *_: (0, 0)
    return pl.pallas_call(
        functools.partial(_final_kernel, slabs=slabs),
        out_shape=jax.ShapeDtypeStruct((n, d), F32),
        grid_spec=pltpu.PrefetchScalarGridSpec(
            num_scalar_prefetch=0,
            grid=(n_tiles,),
            in_specs=[smem_tok, smem_tok, smem_tile, smem_tile, smem_tile, smem_next, smem_next, smem_next,
                      pl.BlockSpec((tf * slabs, LANES), lambda i, *_: (i, 0)),
                      pl.BlockSpec((tf, d), lambda i, *_: (i, 0)),
                      pl.BlockSpec((1,) + ada.shape[1:], lambda i, *_: (i // steps_per_seq, 0, 0)),
                      pl.BlockSpec(ws1b.shape, const2),
                      pl.BlockSpec(ws3b.shape, const2),
                      pl.BlockSpec(ws2b.shape, const2),
                      pl.BlockSpec(g2.shape, const2),
                      pl.BlockSpec(b2.shape, const2),
                      pl.BlockSpec(memory_space=pl.ANY)],
            out_specs=pl.BlockSpec((tf, d), lambda i, *_: (i, 0)),
            scratch_shapes=[pltpu.VMEM((TOP_K * tf * slabs, LANES), F32),
                            pltpu.VMEM((tf * slabs, LANES), F32),
                            pltpu.SemaphoreType.DMA],
        ),
        compiler_params=_cparams(("arbitrary",)),
        name="final",
    )(loff, gate, seg_local, seg_table, seg_cnt, seg_local, seg_table, seg_cnt, u2s, x1, ada, ws1b, ws3b,
      ws2b, g2, b2, ys)


def _swa_head_permutation():
    per_group = SWA_HEADS // SWA_KV_HEADS
    cols = []
    for c in range(per_group):
        for g in range(SWA_KV_HEADS):
            h = c + per_group * g
            cols.extend(range(h * HEAD_DIM, (h + 1) * HEAD_DIM))
    return np.asarray(cols)


def _layer(x, c, positions, w_ada, b_ada, w_in, sinks, g_swa, g_sb, w_out, ln1_g, ln1_b,
           w_router, e_bias, w1, w3, w2, ws1, ws3, ws2, ln2_g, ln2_b):
    batch, seq, d = x.shape
    n = batch * seq
    x2 = x.reshape(n, d)
    perm = _swa_head_permutation()
    per_group = SWA_HEADS // SWA_KV_HEADS

    ada = _ada(c, w_ada, b_ada)

    o_q = SWA_WIDTH + 2 * SWA_KV_WIDTH
    w_in_b = jnp.concatenate(
        [w_in[:, :SWA_WIDTH][:, perm], w_in[:, SWA_WIDTH:o_q],
         w_in[:, o_q:o_q + SB_WIDTH] * (HEAD_DIM ** -0.5), w_in[:, o_q + SB_WIDTH:]], axis=1).astype(BF16)
    half = HEAD_DIM // 2
    inv_freq = ROPE_THETA ** (-jnp.arange(half, dtype=F32) * 2.0 / HEAD_DIM)
    invf = jnp.tile(inv_freq, LANES // half).reshape(1, LANES)
    pos_col = positions.reshape(n, 1).astype(F32)
    qa, ka, va, qs, ks, vs = _inproj(x2, ada, pos_col, invf, w_in_b, seq)

    head_of_block = np.asarray([c_ + per_group * g for g in range(SWA_KV_HEADS) for c_ in range(per_group)])
    sink_col = jnp.repeat(sinks.astype(F32)[head_of_block], WINDOW).reshape(SWA_HEADS * WINDOW, 1)
    oa = _swa(qa, ka, va, sink_col, batch, seq)
    ob = _sb(qs, ks, vs, batch, seq)

    w_out_b = jnp.concatenate([w_out[:SWA_WIDTH][perm], w_out[SWA_WIDTH:]], axis=0).astype(BF16)
    wr_t = w_router.T.astype(F32)
    wr_hi = wr_t.astype(BF16)
    wr_lo = (wr_t - wr_hi.astype(F32)).astype(BF16)
    x1, u2s, logits_t = _post(oa, ob, x2, ada, g_swa[perm].reshape(1, -1), g_sb.reshape(1, -1), w_out_b,
                              ln1_g.reshape(1, d), ln1_b.reshape(1, d), wr_hi, wr_lo, seq)

    lpos_t, gate_t, seglo, cnt, seen_b = _route(logits_t, e_bias)

    counts = seen_b[-1] + cnt[-1]
    padded = (counts + ROW_PAD - 1) // ROW_PAD * ROW_PAD
    pend = jnp.cumsum(padded).astype(I32)
    pstart = pend - padded
    n_rows = n * TOP_K + N_EXPERTS * ROW_PAD
    nused = (pend[-1:] // ZERO_ROWS).astype(I32)
    slabs = d // LANES
    loff = (lpos_t * slabs).T.reshape(-1)
    gate = gate_t.T.reshape(-1)
    seg_local = (seglo[::8] * slabs).reshape(-1)
    seg_table = ((pstart[None, :] + seen_b[::8]) * slabs).reshape(-1)
    seg_cnt = cnt[::8].reshape(-1)

    xs = _dispatch(pstart + counts, pend, nused, loff, seg_local, seg_table, seg_cnt, u2s, n_rows, slabs)
    ys = _experts(pstart, pend, xs, w1, w3, w2, slabs)
    out = _final(loff, gate, seg_local, seg_table, seg_cnt, u2s, x1, ada, ws1.astype(BF16),
                 ws3.astype(BF16), ws2.astype(BF16), ln2_g.reshape(1, d), ln2_b.reshape(1, d), ys, seq, slabs)
    return out.reshape(batch, seq, d)


def kernel(x, c, positions, w_ada, b_ada, w_in, attn_sinks, g_swa, g_sb, w_out, ln1_g, ln1_b,
           w_router, e_bias, w1, w3, w2, ws1, ws3, ws2, ln2_g, ln2_b):
    assert w_ada.shape[0] == DEPTH
    for l in range(DEPTH):
        x = _layer(x, c, positions, w_ada[l], b_ada[l], w_in[l], attn_sinks[l], g_swa[l], g_sb[l],
                   w_out[l], ln1_g[l], ln1_b[l], w_router[l], e_bias[l], w1[l], w3[l], w2[l],
                   ws1[l], ws3[l], ws2[l], ln2_g[l], ln2_b[l])
    return x
```

```python
import functools

import numpy as np
import jax
import jax.numpy as jnp
from jax import lax
from jax.experimental import pallas as pl
from jax.experimental.pallas import tpu as pltpu

F32 = jnp.float32
BF16 = jnp.bfloat16
I32 = jnp.int32

HEAD_DIM = 64
SWA_HEADS = 8
SWA_KV_HEADS = 2
SB_HEADS = 8
SWA_WIDTH = SWA_HEADS * HEAD_DIM
SWA_KV_WIDTH = SWA_KV_HEADS * HEAD_DIM
SB_WIDTH = SB_HEADS * HEAD_DIM
WINDOW = 128
ROPE_THETA = 10000.0
N_EXPERTS = 256
TOP_K = 8
N_GROUPS = 8
TOPK_GROUPS = 4
GROUP_SIZE = N_EXPERTS // N_GROUPS
ROUTED_SCALE = 2.5
LN_EPS = 1e-5
DEPTH = 1
DEEPNORM_ALPHA = (2 * DEPTH) ** 0.25

LANES = 128
SB_BLOCK = 128
SB_TILE_BLOCKS = 3
SB_UNDERFLOW = 110.0
NEG_BIG = -1e30
VMEM_LIMIT = 56 * 1024 * 1024

TM_PROJ = 512
TQ_SWA = 512
T_MOE = 512
ROW_PAD = 128
EXPERT_BLOCK = 512
ZERO_ROWS = 128
SEND_CHUNK = 8
FETCH_CHUNK = 16
MOE_ROW_UNROLL = 4


def _cparams(sem, vmem=VMEM_LIMIT):
    return pltpu.CompilerParams(dimension_semantics=sem, vmem_limit_bytes=vmem)


def _layer_norm(x):
    mu = jnp.mean(x, axis=-1, keepdims=True)
    xc = x - mu
    var = jnp.mean(xc * xc, axis=-1, keepdims=True)
    return xc * lax.rsqrt(var + LN_EPS)


def _silu(x):
    return x * jax.nn.sigmoid(x)


def _dot(a, b):
    return jnp.dot(a, b, preferred_element_type=F32)


def _dot_nt(a, b):
    return lax.dot_general(a, b, (((1,), (1,)), ((), ())), preferred_element_type=F32)


def _store_slab_rows(ref, x):
    t, d = x.shape
    slabs = d // LANES
    for s in range(slabs):
        ref[pl.ds(s, t, stride=slabs), :] = x[:, s * LANES:(s + 1) * LANES]


def _load_slab_rows(ref, slabs):
    t = ref.shape[0] // slabs
    return [ref[pl.ds(s, t, stride=slabs), :] for s in range(slabs)]


def _ada_kernel(c_ref, w_ref, b_ref, o_ref):
    sc = _silu(c_ref[...])
    o_ref[...] = _dot(sc.astype(BF16), w_ref[...].astype(BF16)) + b_ref[...]


def _ada(c, w_ada, b_ada):
    b, d = c.shape
    n_out = w_ada.shape[1]
    rows = 8
    c_pad = jnp.zeros((rows, d), F32).at[:b].set(c)
    out = pl.pallas_call(
        _ada_kernel,
        out_shape=jax.ShapeDtypeStruct((rows, n_out), F32),
        grid=(n_out // d,),
        in_specs=[
            pl.BlockSpec((rows, d), lambda j: (0, 0)),
            pl.BlockSpec((d, d), lambda j: (0, j)),
            pl.BlockSpec((1, d), lambda j: (0, j)),
        ],
        out_specs=pl.BlockSpec((rows, d), lambda j: (0, j)),
        compiler_params=_cparams(("arbitrary",)),
        name="ada",
    )(c_pad, w_ada, b_ada.reshape(1, n_out))
    return out[:b].reshape(b, n_out // d, d)


def _inproj_kernel(x_ref, ada_ref, pos_ref, invf_ref, w_ref,
                   qa_ref, ka_ref, va_ref, qs_ref, ks_ref, vs_ref):
    x = x_ref[...]
    shift = ada_ref[0, 0:1, :]
    scale = ada_ref[0, 1:2, :]
    u = _layer_norm(x) * (1.0 + scale) + shift
    h = _dot(u.astype(BF16), w_ref[...])

    ang = pos_ref[...] * invf_ref[...]
    cs = jnp.cos(ang)
    sn = jnp.sin(ang)
    lane = lax.broadcasted_iota(I32, (1, LANES), 1)
    first = (lane & (HEAD_DIM // 2)) == 0
    sn_signed = jnp.where(first, -sn, sn)

    def rope(hc):
        partner = jnp.where(first, pltpu.roll(hc, LANES - HEAD_DIM // 2, 1),
                            pltpu.roll(hc, HEAD_DIM // 2, 1))
        return hc * cs + partner * sn_signed

    q_scale = HEAD_DIM ** -0.5
    o = 0
    for c in range(SWA_WIDTH // LANES):
        qa_ref[:, c * LANES:(c + 1) * LANES] = (rope(h[:, o:o + LANES]) * q_scale).astype(BF16)
        o += LANES
    ka_ref[...] = rope(h[:, o:o + SWA_KV_WIDTH]).astype(BF16)
    o += SWA_KV_WIDTH
    va_ref[...] = h[:, o:o + SWA_KV_WIDTH].astype(BF16)
    o += SWA_KV_WIDTH
    qs_ref[...] = h[:, o:o + SB_WIDTH].astype(BF16)
    o += SB_WIDTH
    ks_ref[...] = h[:, o:o + SB_WIDTH].astype(BF16)
    o += SB_WIDTH
    vs_ref[...] = h[:, o:o + SB_WIDTH].astype(BF16)


def _inproj(x2, ada, pos_col, invf, w_in_b, seq):
    n, d = x2.shape
    tm = TM_PROJ
    steps_per_seq = seq // tm
    widths = (SWA_WIDTH, SWA_KV_WIDTH, SWA_KV_WIDTH, SB_WIDTH, SB_WIDTH, SB_WIDTH)
    return pl.pallas_call(
        _inproj_kernel,
        out_shape=[jax.ShapeDtypeStruct((n, w), BF16) for w in widths],
        grid=(n // tm,),
        in_specs=[
            pl.BlockSpec((tm, d), lambda i: (i, 0)),
            pl.BlockSpec((1,) + ada.shape[1:], lambda i: (i // steps_per_seq, 0, 0)),
            pl.BlockSpec((tm, 1), lambda i: (i, 0)),
            pl.BlockSpec((1, LANES), lambda i: (0, 0)),
            pl.BlockSpec(w_in_b.shape, lambda i: (0, 0)),
        ],
        out_specs=[pl.BlockSpec((tm, w), lambda i: (i, 0)) for w in widths],
        compiler_params=_cparams(("arbitrary",)),
        name="inproj",
    )(x2, ada, pos_col, invf, w_in_b)


def _swa_kernel(q_ref, kc_ref, kp_ref, vc_ref, vp_ref, bias0_ref, bias_ref, sink_ref,
                o_ref, kall, vall):
    tq = q_ref.shape[0]
    kall[0:WINDOW, :] = kp_ref[...]
    kall[WINDOW:, :] = kc_ref[...]
    vall[0:WINDOW, :] = vp_ref[...]
    vall[WINDOW:, :] = vc_ref[...]
    lane = lax.broadcasted_iota(I32, (1, LANES), 1)
    half0 = lane < HEAD_DIM
    half1 = lane >= HEAD_DIM
    sink = sink_ref[...]
    n_col = SWA_WIDTH // LANES
    for j in range(tq // WINDOW):
        r0 = j * WINDOW
        q = q_ref[r0:r0 + WINDOW, :]
        parts = []
        for half in (half0, half1):
            for c in range(n_col):
                parts.append(jnp.where(half, q[:, c * LANES:(c + 1) * LANES], jnp.zeros((), BF16)))
        qm = jnp.concatenate(parts, axis=0)
        kb = kall[r0:r0 + 2 * WINDOW, :]
        vb = vall[r0:r0 + 2 * WINDOW, :]
        s = _dot_nt(qm, kb)
        s = s + (bias0_ref[0] if j == 0 else bias_ref[...])
        m = jnp.maximum(jnp.max(s, axis=1, keepdims=True), sink)
        p = jnp.exp(s - m)
        den = jnp.sum(p, axis=1, keepdims=True) + jnp.exp(sink - m)
        o = _dot(p.astype(BF16), vb) / den
        for c in range(n_col):
            lo = o[c * WINDOW:(c + 1) * WINDOW]
            hi = o[(n_col + c) * WINDOW:(n_col + c + 1) * WINDOW]
            o_ref[r0:r0 + WINDOW, c * LANES:(c + 1) * LANES] = jnp.where(half0, lo, hi)


def _swa_bias():
    qi = np.arange(SWA_HEADS * WINDOW)[:, None] % WINDOW
    kj = np.arange(2 * WINDOW)[None, :]
    dist = qi + WINDOW - kj
    band = (dist >= 0) & (dist < WINDOW)
    bias = np.where(band, 0.0, NEG_BIG).astype(np.float32)
    first = np.where(band & (kj >= WINDOW), 0.0, NEG_BIG).astype(np.float32)
    return jnp.asarray(np.stack([bias, first])), jnp.asarray(bias)


def _swa(qa, ka, va, sink_col, batch, seq):
    n = qa.shape[0]
    tq = TQ_SWA
    nt = seq // tq
    blocks_per_tile = tq // WINDOW
    bias_pair, bias = _swa_bias()

    def cur(b, i):
        return (b * nt + i, 0)

    def prev(b, i):
        return (jnp.maximum((b * nt + i) * blocks_per_tile - 1, 0), 0)

    return pl.pallas_call(
        _swa_kernel,
        out_shape=jax.ShapeDtypeStruct((n, SWA_WIDTH), F32),
        grid=(batch, nt),
        in_specs=[
            pl.BlockSpec((tq, SWA_WIDTH), cur),
            pl.BlockSpec((tq, SWA_KV_WIDTH), cur),
            pl.BlockSpec((WINDOW, SWA_KV_WIDTH), prev),
            pl.BlockSpec((tq, SWA_KV_WIDTH), cur),
            pl.BlockSpec((WINDOW, SWA_KV_WIDTH), prev),
            pl.BlockSpec((1,) + bias.shape, lambda b, i: (jnp.where(i == 0, 1, 0), 0, 0)),
            pl.BlockSpec(bias.shape, lambda b, i: (0, 0)),
            pl.BlockSpec(sink_col.shape, lambda b, i: (0, 0)),
        ],
        out_specs=pl.BlockSpec((tq, SWA_WIDTH), cur),
        scratch_shapes=[pltpu.VMEM((tq + WINDOW, SWA_KV_WIDTH), BF16),
                        pltpu.VMEM((tq + WINDOW, SWA_KV_WIDTH), BF16)],
        compiler_params=_cparams(("arbitrary", "arbitrary")),
        name="swa",
    )(qa, ka, ka, va, va, bias_pair, bias, sink_col)


def _sb_kernel(q_ref, k_ref, v_ref, uu_ref, o_ref, qm, acc, run):
    i = pl.program_id(1)
    blk = SB_BLOCK
    n_sub = SB_TILE_BLOCKS
    tile = n_sub * blk
    pairs = q_ref.shape[1] // LANES
    lane = lax.broadcasted_iota(I32, (1, LANES), 1)
    half0 = lane < HEAD_DIM
    half1 = lane >= HEAD_DIM
    zero = jnp.zeros((), BF16)
    for p in range(pairs):
        q = q_ref[:, p * LANES:(p + 1) * LANES]
        qm[p] = jnp.concatenate([jnp.where(half0, q, zero), jnp.where(half1, q, zero)], axis=0)
    acc[...] = jnp.zeros_like(acc)
    run[...] = jnp.zeros_like(run)
    q_pos = i * blk + (lax.broadcasted_iota(I32, (2 * blk, blk), 0) & (blk - 1))
    col = lax.broadcasted_iota(I32, (2 * blk, blk), 1)

    def body(carry):
        end, _ = carry
        start = pl.multiple_of(jnp.maximum(end - tile, 0), blk)
        limit = jnp.minimum(end, q_pos) - start
        least = None
        for p in range(pairs):
            kt = k_ref[pl.ds(start, tile), p * LANES:(p + 1) * LANES]
            vt = v_ref[pl.ds(start, tile), p * LANES:(p + 1) * LANES]
            z = _dot_nt(qm[p], kt)
            mass_seen = run[p]
            ws = [None] * n_sub
            for b in reversed(range(n_sub)):
                zb = jnp.where(col < limit - b * blk, z[:, b * blk:(b + 1) * blk], NEG_BIG)
                mass = jnp.maximum(zb, 0.0) + jnp.log(1.0 + jnp.exp(-jnp.abs(zb)))
                mass_hi = mass.astype(BF16)
                mass_lo = (mass - mass_hi.astype(F32)).astype(BF16)
                cc = _dot(jnp.concatenate([mass_hi, mass_lo], axis=1), uu_ref[...])
                ws[b] = jnp.exp(zb - (cc[:, :blk] + mass_seen)).astype(BF16)
                mass_seen = mass_seen + cc[:, blk:]
            acc[p] += _dot(jnp.concatenate(ws, axis=1), vt)
            run[p] = mass_seen
            least = mass_seen if least is None else jnp.minimum(least, mass_seen)
        return start, jnp.min(least) > SB_UNDERFLOW

    lax.while_loop(lambda c: jnp.logical_and(c[0] > 0, jnp.logical_not(c[1])),
                   body, ((i + 1) * blk, jnp.bool_(False)))
    for p in range(pairs):
        a = acc[p]
        o_ref[:, p * LANES:(p + 1) * LANES] = jnp.where(half0, a[:blk], a[blk:])


def _sb_suffix_matrix():
    j = np.arange(2 * SB_BLOCK)[:, None] % SB_BLOCK
    s = np.arange(2 * SB_BLOCK)[None, :]
    m = np.where(s < SB_BLOCK, (j >= s), True)
    return jnp.asarray(m.astype(np.float32)).astype(BF16)


def _sb(qs, ks, vs, batch, seq):
    n = qs.shape[0]
    blk = SB_BLOCK
    nq = seq // blk
    pairs = SB_WIDTH // LANES
    uu = _sb_suffix_matrix()
    return pl.pallas_call(
        _sb_kernel,
        out_shape=jax.ShapeDtypeStruct((n, SB_WIDTH), F32),
        grid=(batch, nq),
        in_specs=[
            pl.BlockSpec((blk, SB_WIDTH), lambda b, i: (b * nq + i, 0)),
            pl.BlockSpec((seq, SB_WIDTH), lambda b, i: (b, 0)),
            pl.BlockSpec((seq, SB_WIDTH), lambda b, i: (b, 0)),
            pl.BlockSpec(uu.shape, lambda b, i: (0, 0)),
        ],
        out_specs=pl.BlockSpec((blk, SB_WIDTH), lambda b, i: (b * nq + i, 0)),
        scratch_shapes=[pltpu.VMEM((pairs, 2 * blk, LANES), BF16),
                        pltpu.VMEM((pairs, 2 * blk, LANES), F32),
                        pltpu.VMEM((pairs, 2 * blk, LANES), F32)],
        compiler_params=_cparams(("arbitrary", "arbitrary")),
        name="sb",
    )(qs, ks, vs, uu)


def _post_kernel(oa_ref, ob_ref, x_ref, ada_ref, ga_ref, gb_ref, wo_ref, g1_ref, b1_ref,
                 wrh_ref, wrl_ref, x1_ref, u2s_ref, lg_ref):
    def rms(o, g):
        return o * lax.rsqrt(jnp.mean(o * o, axis=-1, keepdims=True) + LN_EPS) * g

    na = rms(oa_ref[...], ga_ref[...]).astype(BF16)
    nb = rms(ob_ref[...], gb_ref[...]).astype(BF16)
    y = _dot(na, wo_ref[0:SWA_WIDTH, :]) + _dot(nb, wo_ref[SWA_WIDTH:, :])
    gate1 = ada_ref[0, 2:3, :]
    shift2 = ada_ref[0, 3:4, :]
    scale2 = ada_ref[0, 4:5, :]
    x1 = _layer_norm(DEEPNORM_ALPHA * x_ref[...] + (1.0 + gate1) * y) * g1_ref[...] + b1_ref[...]
    x1_ref[...] = x1
    u2 = _layer_norm(x1) * (1.0 + scale2) + shift2
    _store_slab_rows(u2s_ref, u2)
    u_hi = u2.astype(BF16)
    u_lo = (u2 - u_hi.astype(F32)).astype(BF16)
    wrh = wrh_ref[...]
    lg_ref[...] = _dot_nt(wrh, u_hi) + (_dot_nt(wrh, u_lo) + _dot_nt(wrl_ref[...], u_hi))


def _post(oa, ob, x2, ada, g_a, g_b, w_out_b, g1, b1, wr_hi, wr_lo, seq):
    n, d = x2.shape
    tm = TM_PROJ
    steps_per_seq = seq // tm
    e = wr_hi.shape[0]
    slabs = d // LANES
    const2 = lambda i: (0, 0)
    return pl.pallas_call(
        _post_kernel,
        out_shape=[jax.ShapeDtypeStruct((n, d), F32),
                   jax.ShapeDtypeStruct((n * slabs, LANES), F32),
                   jax.ShapeDtypeStruct((e, n), F32)],
        grid=(n // tm,),
        in_specs=[
            pl.BlockSpec((tm, SWA_WIDTH), lambda i: (i, 0)),
            pl.BlockSpec((tm, SB_WIDTH), lambda i: (i, 0)),
            pl.BlockSpec((tm, d), lambda i: (i, 0)),
            pl.BlockSpec((1,) + ada.shape[1:], lambda i: (i // steps_per_seq, 0, 0)),
            pl.BlockSpec(g_a.shape, const2),
            pl.BlockSpec(g_b.shape, const2),
            pl.BlockSpec(w_out_b.shape, const2),
            pl.BlockSpec(g1.shape, const2),
            pl.BlockSpec(b1.shape, const2),
            pl.BlockSpec(wr_hi.shape, const2),
            pl.BlockSpec(wr_lo.shape, const2),
        ],
        out_specs=[pl.BlockSpec((tm, d), lambda i: (i, 0)),
                   pl.BlockSpec((tm * slabs, LANES), lambda i: (i, 0)),
                   pl.BlockSpec((e, tm), lambda i: (0, i))],
        compiler_params=_cparams(("arbitrary",)),
        name="post",
    )(oa, ob, x2, ada, g_a, g_b, w_out_b, g1, b1, wr_hi, wr_lo)


def _route_kernel(lg_ref, bias_ref, su_ref, sl_ref, lpos_ref, gate_ref, seglo_ref, cnt_ref, seenb_ref,
                  seen_row):
    i = pl.program_id(0)

    @pl.when(i == 0)
    def _():
        seen_row[...] = jnp.zeros_like(seen_row)

    tt = lg_ref.shape[1]
    ninf = -jnp.inf
    scores = jax.nn.sigmoid(lg_ref[...])
    biased = scores + bias_ref[...]

    iog32 = lax.broadcasted_iota(I32, (GROUP_SIZE, tt), 0)
    groups = [biased[g * GROUP_SIZE:(g + 1) * GROUP_SIZE, :] for g in range(N_GROUPS)]
    gs_rows = []
    for blk in groups:
        m1 = jnp.max(blk, axis=0, keepdims=True)
        i1 = jnp.min(jnp.where(blk == m1, iog32, GROUP_SIZE), axis=0, keepdims=True)
        m2 = jnp.max(jnp.where(iog32 == i1, ninf, blk), axis=0, keepdims=True)
        gs_rows.append(m1 + m2)
    gs = jnp.concatenate(gs_rows, axis=0)

    iog = lax.broadcasted_iota(I32, gs.shape, 0)
    gsel = jnp.zeros(gs.shape, F32)
    cur = gs
    for _ in range(TOPK_GROUPS):
        m = jnp.max(cur, axis=0, keepdims=True)
        idx = jnp.min(jnp.where(cur == m, iog, N_GROUPS), axis=0, keepdims=True)
        hit = iog == idx
        gsel = jnp.where(hit, 1.0, gsel)
        cur = jnp.where(hit, ninf, cur)

    cand = jnp.concatenate(
        [jnp.where(gsel[g:g + 1, :] > 0.5, groups[g], ninf) for g in range(N_GROUPS)], axis=0)
    ioe = lax.broadcasted_iota(I32, cand.shape, 0)
    chosen = jnp.zeros(cand.shape, F32)
    idxs, gates = [], []
    for _ in range(TOP_K):
        m = jnp.max(cand, axis=0, keepdims=True)
        idx = jnp.min(jnp.where(cand == m, ioe, N_EXPERTS), axis=0, keepdims=True)
        hit = ioe == idx
        gates.append(jnp.sum(jnp.where(hit, scores, 0.0), axis=0, keepdims=True))
        cand = jnp.where(hit, ninf, cand)
        chosen = jnp.where(hit, 1.0, chosen)
        idxs.append(idx)
    gsum = gates[0]
    for g in gates[1:]:
        gsum = gsum + g
    gates = [g / gsum * ROUTED_SCALE for g in gates]

    def hi_lo(x):
        hi = x.astype(BF16)
        return hi, (x - hi.astype(F32)).astype(BF16)

    chosen_b = chosen.astype(BF16)
    rk = _dot(chosen_b, su_ref[...])
    c_hi, c_lo = hi_lo(rk[:, tt:])
    first_col = _dot(sl_ref[...], c_hi) + _dot(sl_ref[...], c_lo)
    slot_mat = rk[:, :tt] + jnp.concatenate([first_col] * (tt // LANES), axis=1)
    slots = [jnp.sum(jnp.where(ioe == idx, slot_mat, 0.0), axis=0, keepdims=True) for idx in idxs]
    lpos_ref[...] = jnp.concatenate(slots, axis=0).astype(I32)
    gate_ref[...] = jnp.concatenate(gates, axis=0)
    cnt_row = _dot_nt(jnp.ones((8, tt), BF16), chosen_b)
    r_hi, r_lo = hi_lo(cnt_row)
    before_row = seen_row[...]
    seen_row[...] = before_row + cnt_row
    seglo_ref[...] = (_dot_nt(r_hi, sl_ref[...]) + _dot_nt(r_lo, sl_ref[...])).astype(I32)
    cnt_ref[...] = cnt_row.astype(I32)
    seenb_ref[...] = before_row.astype(I32)


def _route_prefix_matrix(tt):
    a = np.arange(tt)[:, None] < np.arange(tt)[None, :]
    m = np.concatenate([a, np.ones((tt, LANES), bool)], axis=1)
    return jnp.asarray(m.astype(np.float32)).astype(BF16)


def _route(logits_t, e_bias):
    e, n = logits_t.shape
    tt = T_MOE
    bias_b = jnp.broadcast_to(e_bias.astype(F32)[:, None], (e, tt))
    su = _route_prefix_matrix(tt)
    sl = jnp.asarray((np.arange(e)[None, :] < np.arange(e)[:, None]).astype(np.float32)).astype(BF16)
    tok = lambda i: (0, i)
    per_tile = jax.ShapeDtypeStruct((n // tt * 8, e), I32)
    per_tile_spec = pl.BlockSpec((8, e), lambda i: (i, 0))
    return pl.pallas_call(
        _route_kernel,
        out_shape=[jax.ShapeDtypeStruct((TOP_K, n), I32),
                   jax.ShapeDtypeStruct((TOP_K, n), F32),
                   per_tile, per_tile, per_tile],
        grid=(n // tt,),
        in_specs=[pl.BlockSpec((e, tt), tok),
                  pl.BlockSpec((e, tt), lambda i: (0, 0)),
                  pl.BlockSpec(su.shape, lambda i: (0, 0)),
                  pl.BlockSpec(sl.shape, lambda i: (0, 0))],
        out_specs=[pl.BlockSpec((TOP_K, tt), tok), pl.BlockSpec((TOP_K, tt), tok),
                   per_tile_spec, per_tile_spec, per_tile_spec],
        scratch_shapes=[pltpu.VMEM((8, e), F32)],
        compiler_params=_cparams(("arbitrary",)),
        name="route",
    )(logits_t, bias_b, su, sl)


def _post_route_kernel(oa_ref, ob_ref, x_ref, ada_ref, ga_ref, gb_ref, wo_ref, g1_ref, b1_ref,
                       wrh_ref, wrl_ref, bias_ref, su_ref, sl_ref,
                       x1_ref, u2s_ref, lpos_ref, gate_ref, seglo_ref, cnt_ref, seenb_ref, lg, seen_row):
    _post_kernel(oa_ref, ob_ref, x_ref, ada_ref, ga_ref, gb_ref, wo_ref, g1_ref, b1_ref,
                 wrh_ref, wrl_ref, x1_ref, u2s_ref, lg)
    _route_kernel(lg, bias_ref, su_ref, sl_ref, lpos_ref, gate_ref, seglo_ref, cnt_ref, seenb_ref, seen_row)


def _post_route(oa, ob, x2, ada, g_a, g_b, w_out_b, g1, b1, wr_hi, wr_lo, e_bias, seq):
    n, d = x2.shape
    tm = T_MOE
    steps_per_seq = seq // tm
    e = wr_hi.shape[0]
    slabs = d // LANES
    bias_b = jnp.broadcast_to(e_bias.astype(F32)[:, None], (e, tm))
    su = _route_prefix_matrix(tm)
    sl = jnp.asarray((np.arange(e)[None, :] < np.arange(e)[:, None]).astype(np.float32)).astype(BF16)
    const2 = lambda i: (0, 0)
    row = lambda i: (i, 0)
    tok = lambda i: (0, i)
    per_tile = jax.ShapeDtypeStruct((n // tm * 8, e), I32)
    per_tile_spec = pl.BlockSpec((8, e), row)
    consts = (g_a, g_b, w_out_b, g1, b1, wr_hi, wr_lo, bias_b, su, sl)
    return pl.pallas_call(
        _post_route_kernel,
        out_shape=[jax.ShapeDtypeStruct((n, d), F32),
                   jax.ShapeDtypeStruct((n * slabs, LANES), F32),
                   jax.ShapeDtypeStruct((TOP_K, n), I32),
                   jax.ShapeDtypeStruct((TOP_K, n), F32),
                   per_tile, per_tile, per_tile],
        grid=(n // tm,),
        in_specs=[pl.BlockSpec((tm, SWA_WIDTH), row),
                  pl.BlockSpec((tm, SB_WIDTH), row),
                  pl.BlockSpec((tm, d), row),
                  pl.BlockSpec((1,) + ada.shape[1:], lambda i: (i // steps_per_seq, 0, 0))]
                 + [pl.BlockSpec(c.shape, const2) for c in consts],
        out_specs=[pl.BlockSpec((tm, d), row),
                   pl.BlockSpec((tm * slabs, LANES), row),
                   pl.BlockSpec((TOP_K, tm), tok), pl.BlockSpec((TOP_K, tm), tok),
                   per_tile_spec, per_tile_spec, per_tile_spec],
        scratch_shapes=[pltpu.VMEM((e, tm), F32), pltpu.VMEM((8, e), F32)],
        compiler_params=_cparams(("arbitrary",)),
        name="post_route",
    )(oa, ob, x2, ada, *consts)


def _slab(ref, row, slabs):
    return ref.at[pl.ds(pl.multiple_of(row * slabs, slabs), slabs)]


def _start_segment(e, local_ref, table_ref, cnt_ref, make_copy, slabs, chunk_rows):
    cnt = cnt_ref[e]
    local = local_ref[e]
    table = table_ref[e]
    n_chunks = lax.shift_right_logical(cnt, chunk_rows.bit_length() - 1)
    step = chunk_rows * slabs

    def chunk(j, c):
        make_copy(local + j * step, table + j * step, chunk_rows).start()
        return c

    lax.fori_loop(0, n_chunks, chunk, 0)
    done = n_chunks * step
    rows = chunk_rows // 2
    while rows >= 1:
        take = (cnt & rows) != 0

        @pl.when(take)
        def _(done=done, rows=rows):
            make_copy(local + done, table + done, rows).start()

        done = done + jnp.where(take, rows * slabs, 0)
        rows //= 2


def _dispatch_kernel(pvalid_ref, pend_ref, nused_ref, loff_ref, seg_local_ref, seg_table_ref, seg_cnt_ref,
                     u_ref, xs_ref, obuf, zbuf, sems, sem, *, slabs):
    i = pl.program_id(0)
    last = pl.num_programs(0) - 1
    td = u_ref.shape[0] // slabs
    zrows = zbuf.shape[0] // slabs
    chunks_per_pad = ROW_PAD // zrows
    n_chunks = xs_ref.shape[0] // slabs // zrows

    def zero_rows(row0, rows):
        return pltpu.make_async_copy(
            zbuf.at[pl.ds(0, rows * slabs)],
            xs_ref.at[pl.ds(pl.multiple_of(row0 * slabs, slabs), rows * slabs)], sem)

    def zero_fill(wait):
        def go(copy):
            if wait:
                copy.wait()
            else:
                copy.start()

        def unused(b, c):
            go(zero_rows(b * zrows, zrows))
            return c

        lax.fori_loop(nused_ref[0], n_chunks, unused, 0)

        def padding(e, c):
            first = pvalid_ref[e]
            end = pend_ref[e]
            for g in range(chunks_per_pad):
                row0 = end - zrows * (g + 1)

                @pl.when(row0 >= first)
                def _(row0=row0):
                    go(zero_rows(row0, zrows))
            rem = (end - first) & (zrows - 1)
            done = jnp.int32(0)
            rows = zrows // 2
            while rows >= 1:
                take = (rem & rows) != 0

                @pl.when(take)
                def _(done=done, rows=rows):
                    go(zero_rows(first + done, rows))

                done = done + jnp.where(take, rows, 0)
                rows //= 2
            return c

        lax.fori_loop(0, N_EXPERTS, padding, 0)

    @pl.when(i == 0)
    def _():
        zbuf[...] = jnp.zeros_like(zbuf)
        zero_fill(wait=False)

    slot = i % 2
    buf = obuf.at[slot]
    out_sem = sems.at[slot]

    def drained(b, s):
        return pltpu.make_async_copy(obuf.at[b], xs_ref.at[pl.ds(0, obuf.shape[1])], sems.at[s])

    @pl.when(i >= 2)
    def _():
        drained(slot, slot).wait()

    def place(t, c):
        row = _slab(u_ref, t, slabs)[...]
        for k in range(TOP_K):
            buf[pl.ds(pl.multiple_of(loff_ref[t * TOP_K + k], slabs), slabs), :] = row
        return c

    lax.fori_loop(0, td, place, 0, unroll=MOE_ROW_UNROLL)

    def to_table(local, table, rows):
        return pltpu.make_async_copy(
            buf.at[pl.ds(pl.multiple_of(local, slabs), rows * slabs)],
            xs_ref.at[pl.ds(pl.multiple_of(table, slabs), rows * slabs)], out_sem)

    def send(e, c):
        _start_segment(e, seg_local_ref, seg_table_ref, seg_cnt_ref, to_table, slabs, SEND_CHUNK)
        return c

    lax.fori_loop(0, N_EXPERTS, send, 0)

    @pl.when(i == last)
    def _():
        drained(slot, slot).wait()

        @pl.when(last >= 1)
        def _():
            drained(1 - slot, 1 - slot).wait()

        zero_fill(wait=True)


def _dispatch(pvalid, pend, nused, loff, seg_local, seg_table, seg_cnt, u2s, n_rows, slabs):
    n = u2s.shape[0] // slabs
    td = T_MOE
    smem_tok = pl.BlockSpec((TOP_K * td,), lambda i, *_: (i,), memory_space=pltpu.SMEM)
    smem_tile = pl.BlockSpec((N_EXPERTS,), lambda i, *_: (i,), memory_space=pltpu.SMEM)
    return pl.pallas_call(
        functools.partial(_dispatch_kernel, slabs=slabs),
        out_shape=jax.ShapeDtypeStruct((n_rows * slabs, LANES), F32),
        grid_spec=pltpu.PrefetchScalarGridSpec(
            num_scalar_prefetch=3,
            grid=(n // td,),
            in_specs=[smem_tok, smem_tile, smem_tile, smem_tile,
                      pl.BlockSpec((td * slabs, LANES), lambda i, *_: (i, 0))],
            out_specs=pl.BlockSpec(memory_space=pl.ANY),
            scratch_shapes=[pltpu.VMEM((2, TOP_K * td * slabs, LANES), F32),
                            pltpu.VMEM((ZERO_ROWS * slabs, LANES), F32),
                            pltpu.SemaphoreType.DMA((2,)),
                            pltpu.SemaphoreType.DMA],
        ),
        compiler_params=_cparams(("arbitrary",)),
        name="dispatch",
    )(pvalid, pend, nused, loff, seg_local, seg_table, seg_cnt, u2s)


def _expert_kernel(step_e_ref, nsteps_ref, first_ref, slot_ref, next_e_ref, row0_ref, nsub_ref, tail_ref,
                   xs_ref, w1_hbm, w3_hbm, w2_hbm, ys_ref,
                   xbuf, ybuf, zbuf, wf1, wf3, wf2, w1b, w3b, w2b, in_sems, out_sems, w_sems, z_sem, *, slabs):
    i = pl.program_id(0)
    nsteps = nsteps_ref[0]
    sub = ROW_PAD * slabs
    subs_per_window = EXPERT_BLOCK // ROW_PAD
    n_tail = ys_ref.shape[0] // sub

    def weights(e, s):
        return [pltpu.make_async_copy(src.at[e], dst.at[s], w_sems.at[s])
                for src, dst in ((w1_hbm, wf1), (w3_hbm, wf3), (w2_hbm, wf2))]

    def rows_in(step, s, j):
        off = pl.multiple_of(row0_ref[step] * slabs + j * sub, sub)
        return pltpu.make_async_copy(xs_ref.at[pl.ds(off, sub)], xbuf.at[s, pl.ds(j * sub, sub)], in_sems.at[s])

    def rows_out(step, s, j):
        off = pl.multiple_of(row0_ref[step] * slabs + j * sub, sub)
        return pltpu.make_async_copy(ybuf.at[s, pl.ds(j * sub, sub)], ys_ref.at[pl.ds(off, sub)], out_sems.at[s])

    def each_live(make, step, s, wait):
        for j in range(subs_per_window):
            @pl.when(j < nsub_ref[step])
            def _(j=j):
                if wait:
                    make(step, s, j).wait()
                else:
                    make(step, s, j).start()

    def zero_tail(wait):
        def body(c, carry):
            copy = pltpu.make_async_copy(zbuf, ys_ref.at[pl.ds(pl.multiple_of(c * sub, sub), sub)], z_sem)
            if wait:
                copy.wait()
            else:
                copy.start()
            return carry

        lax.fori_loop(tail_ref[0], n_tail, body, 0)

    @pl.when(i < nsteps)
    def _():
        s = i % 2

        @pl.when(i == 0)
        def _():
            xbuf[...] = jnp.zeros_like(xbuf)
            zbuf[...] = jnp.zeros_like(zbuf)
            zero_tail(wait=False)
            each_live(rows_in, i, s, wait=False)

        each_live(rows_in, i, s, wait=True)

        @pl.when(i + 1 < nsteps)
        def _():
            each_live(rows_in, i + 1, 1 - s, wait=False)

        @pl.when(first_ref[i] == 1)
        def _():
            e = step_e_ref[i]
            ws = slot_ref[i]

            @pl.when(i == 0)
            def _():
                for c in weights(e, ws):
                    c.start()

            for c in weights(e, ws):
                c.wait()
            w1b[...] = wf1[ws].astype(BF16)
            w3b[...] = wf3[ws].astype(BF16)
            w2b[...] = wf2[ws].astype(BF16)

            @pl.when(next_e_ref[i] >= 0)
            def _():
                for c in weights(next_e_ref[i], 1 - ws):
                    c.start()

        @pl.when(i >= 2)
        def _():
            each_live(rows_out, i - 2, s, wait=True)

        x = jnp.concatenate([c.astype(BF16) for c in _load_slab_rows(xbuf.at[s], slabs)], axis=1)
        h1 = _dot(x, w1b[...])
        h3 = _dot(x, w3b[...])
        a = (_silu(h1) * h3).astype(BF16)
        _store_slab_rows(ybuf.at[s], _dot(a, w2b[...]))
        each_live(rows_out, i, s, wait=False)

        @pl.when(i == nsteps - 1)
        def _():
            each_live(rows_out, i, s, wait=True)

            @pl.when(i >= 1)
            def _():
                each_live(rows_out, i - 1, 1 - s, wait=True)

            zero_tail(wait=True)


def _experts(pstart, pend, xs, w1, w3, w2, slabs):
    n_rows = xs.shape[0] // slabs
    e, d, f = w1.shape
    padded = pend - pstart
    steps_e = (padded + EXPERT_BLOCK - 1) // EXPERT_BLOCK
    step_end = jnp.cumsum(steps_e).astype(I32)
    step_start = step_end - steps_e
    ns = -(-n_rows // EXPERT_BLOCK) + -(-e * (EXPERT_BLOCK - ROW_PAD) // EXPERT_BLOCK)
    nsteps = step_end[-1:]
    step = jnp.arange(ns, dtype=I32)
    step_e = jnp.minimum(jnp.sum((step_end[None, :] <= step[:, None]).astype(I32), axis=1), e - 1)
    live = step < nsteps[0]
    row0 = jnp.where(live, pstart[step_e] + EXPERT_BLOCK * (step - step_start[step_e]), 0).astype(I32)
    nsub = jnp.where(live, jnp.clip((pend[step_e] - row0) // ROW_PAD, 0, EXPERT_BLOCK // ROW_PAD), 0).astype(I32)
    first = ((step == 0) | (step_e != jnp.roll(step_e, 1))) & live
    slot = (jnp.cumsum(first.astype(I32)) - 1) & 1
    opens_at = jnp.where(first, step, ns)
    next_open = jnp.concatenate([lax.cummin(opens_at, reverse=True)[1:], jnp.full((1,), ns, I32)])
    next_e = jnp.where(next_open < ns, step_e[jnp.minimum(next_open, ns - 1)], -1).astype(I32)
    tail = (pend[-1:] // ROW_PAD).astype(I32)

    any_space = pl.BlockSpec(memory_space=pl.ANY)
    window = EXPERT_BLOCK * slabs
    return pl.pallas_call(
        functools.partial(_expert_kernel, slabs=slabs),
        out_shape=jax.ShapeDtypeStruct((n_rows * slabs, LANES), F32),
        grid_spec=pltpu.PrefetchScalarGridSpec(
            num_scalar_prefetch=8,
            grid=(ns,),
            in_specs=[any_space, any_space, any_space, any_space],
            out_specs=any_space,
            scratch_shapes=[pltpu.VMEM((2, window, LANES), F32), pltpu.VMEM((2, window, LANES), F32),
                            pltpu.VMEM((ROW_PAD * slabs, LANES), F32),
                            pltpu.VMEM((2, d, f), F32), pltpu.VMEM((2, d, f), F32),
                            pltpu.VMEM((2, f, d), F32),
                            pltpu.VMEM((d, f), BF16), pltpu.VMEM((d, f), BF16),
                            pltpu.VMEM((f, d), BF16),
                            pltpu.SemaphoreType.DMA((2,)), pltpu.SemaphoreType.DMA((2,)),
                            pltpu.SemaphoreType.DMA((2,)), pltpu.SemaphoreType.DMA],
        ),
        compiler_params=_cparams(("arbitrary",)),
        name="experts",
    )(step_e, nsteps, first.astype(I32), slot.astype(I32), next_e, row0, nsub, tail, xs, w1, w3, w2)


def _final_kernel(loff_ref, gate_ref, seg_local, seg_table, seg_cnt, seg_local_nx, seg_table_nx, seg_cnt_nx,
                  u_ref, x1_ref, ada_ref, ws1_ref, ws3_ref, ws2_ref, g2_ref, b2_ref, ys_ref, o_ref,
                  lbuf, rbuf, sem, *, slabs):
    i = pl.program_id(0)
    last = pl.num_programs(0) - 1
    tf = u_ref.shape[0] // slabs

    def from_table(local, table, rows):
        return pltpu.make_async_copy(
            ys_ref.at[pl.ds(pl.multiple_of(table, slabs), rows * slabs)],
            lbuf.at[pl.ds(pl.multiple_of(local, slabs), rows * slabs)], sem)

    def fetch_tile(local_r, table_r, cnt_r):
        def fetch(e, c):
            _start_segment(e, local_r, table_r, cnt_r, from_table, slabs, FETCH_CHUNK)
            return c

        lax.fori_loop(0, N_EXPERTS, fetch, 0)

    @pl.when(i == 0)
    def _():
        fetch_tile(seg_local, seg_table, seg_cnt)

    pltpu.make_async_copy(ys_ref.at[pl.ds(0, lbuf.shape[0])], lbuf, sem).wait()

    def combine(t, c):
        rows = [lbuf[pl.ds(pl.multiple_of(loff_ref[t * TOP_K + k], slabs), slabs), :] * gate_ref[t * TOP_K + k]
                for k in range(TOP_K)]
        while len(rows) > 1:
            rows = [a + b for a, b in zip(rows[0::2], rows[1::2])]
        _slab(rbuf, t, slabs)[...] = rows[0]
        return c

    lax.fori_loop(0, tf, combine, 0, unroll=MOE_ROW_UNROLL)

    @pl.when(i < last)
    def _():
        fetch_tile(seg_local_nx, seg_table_nx, seg_cnt_nx)

    u = jnp.concatenate([c.astype(BF16) for c in _load_slab_rows(u_ref, slabs)], axis=1)
    shared = _dot((_silu(_dot(u, ws1_ref[...])) * _dot(u, ws3_ref[...])).astype(BF16), ws2_ref[...])
    routed = jnp.concatenate(_load_slab_rows(rbuf, slabs), axis=1)
    gate2 = ada_ref[0, 5:6, :]
    y = shared + routed
    o_ref[...] = _layer_norm(DEEPNORM_ALPHA * x1_ref[...] + (1.0 + gate2) * y) * g2_ref[...] + b2_ref[...]


def _final(loff, gate, seg_local, seg_table, seg_cnt, u2s, x1, ada, ws1b, ws3b, ws2b, g2, b2, ys,
           seq, slabs):
    n, d = x1.shape
    tf = T_MOE
    n_tiles = n // tf
    steps_per_seq = seq // tf
    smem_tok = pl.BlockSpec((TOP_K * tf,), lambda i, *_: (i,), memory_space=pltpu.SMEM)
    smem_tile = pl.BlockSpec((N_EXPERTS,), lambda i, *_: (i,), memory_space=pltpu.SMEM)
    smem_next = pl.BlockSpec((N_EXPERTS,), lambda i, *_: (jnp.minimum(i + 1, n_tiles - 1),),
                             memory_space=pltpu.SMEM)
    const2 = lambda i, *_: (0, 0)
    return pl.pallas_call(
        functools.partial(_final_kernel, slabs=slabs),
        out_shape=jax.ShapeDtypeStruct((n, d), F32),
        grid_spec=pltpu.PrefetchScalarGridSpec(
            num_scalar_prefetch=0,
            grid=(n_tiles,),
            in_specs=[smem_tok, smem_tok, smem_tile, smem_tile, smem_tile, smem_next, smem_next, smem_next,
                      pl.BlockSpec((tf * slabs, LANES), lambda i, *_: (i, 0)),
                      pl.BlockSpec((tf, d), lambda i, *_: (i, 0)),
                      pl.BlockSpec((1,) + ada.shape[1:], lambda i, *_: (i // steps_per_seq, 0, 0)),
                      pl.BlockSpec(ws1b.shape, const2),
                      pl.BlockSpec(ws3b.shape, const2),
                      pl.BlockSpec(ws2b.shape, const2),
                      pl.BlockSpec(g2.shape, const2),
                      pl.BlockSpec(b2.shape, const2),
                      pl.BlockSpec(memory_space=pl.ANY)],
            out_specs=pl.BlockSpec((tf, d), lambda i, *_: (i, 0)),
            scratch_shapes=[pltpu.VMEM((TOP_K * tf * slabs, LANES), F32),
                            pltpu.VMEM((tf * slabs, LANES), F32),
                            pltpu.SemaphoreType.DMA],
        ),
        compiler_params=_cparams(("arbitrary",)),
        name="final",
    )(loff, gate, seg_local, seg_table, seg_cnt, seg_local, seg_table, seg_cnt, u2s, x1, ada, ws1b, ws3b,
      ws2b, g2, b2, ys)


def _swa_head_permutation():
    per_group = SWA_HEADS // SWA_KV_HEADS
    cols = []
    for c in range(per_group):
        for g in range(SWA_KV_HEADS):
            h = c + per_group * g
            cols.extend(range(h * HEAD_DIM, (h + 1) * HEAD_DIM))
    return np.asarray(cols)


def _layer(x, c, positions, w_ada, b_ada, w_in, sinks, g_swa, g_sb, w_out, ln1_g, ln1_b,
           w_router, e_bias, w1, w3, w2, ws1, ws3, ws2, ln2_g, ln2_b):
    batch, seq, d = x.shape
    n = batch * seq
    x2 = x.reshape(n, d)
    perm = _swa_head_permutation()
    per_group = SWA_HEADS // SWA_KV_HEADS

    ada = _ada(c, w_ada, b_ada)

    o_q = SWA_WIDTH + 2 * SWA_KV_WIDTH
    w_in_b = jnp.concatenate(
        [w_in[:, :SWA_WIDTH][:, perm], w_in[:, SWA_WIDTH:o_q],
         w_in[:, o_q:o_q + SB_WIDTH] * (HEAD_DIM ** -0.5), w_in[:, o_q + SB_WIDTH:]], axis=1).astype(BF16)
    half = HEAD_DIM // 2
    inv_freq = ROPE_THETA ** (-jnp.arange(half, dtype=F32) * 2.0 / HEAD_DIM)
    invf = jnp.tile(inv_freq, LANES // half).reshape(1, LANES)
    pos_col = positions.reshape(n, 1).astype(F32)
    qa, ka, va, qs, ks, vs = _inproj(x2, ada, pos_col, invf, w_in_b, seq)

    head_of_block = np.asarray([c_ + per_group * g for g in range(SWA_KV_HEADS) for c_ in range(per_group)])
    sink_col = jnp.repeat(sinks.astype(F32)[head_of_block], WINDOW).reshape(SWA_HEADS * WINDOW, 1)
    oa = _swa(qa, ka, va, sink_col, batch, seq)
    ob = _sb(qs, ks, vs, batch, seq)

    w_out_b = jnp.concatenate([w_out[:SWA_WIDTH][perm], w_out[SWA_WIDTH:]], axis=0).astype(BF16)
    wr_t = w_router.T.astype(F32)
    wr_hi = wr_t.astype(BF16)
    wr_lo = (wr_t - wr_hi.astype(F32)).astype(BF16)
    x1, u2s, lpos_t, gate_t, seglo, cnt, seen_b = _post_route(
        oa, ob, x2, ada, g_swa[perm].reshape(1, -1), g_sb.reshape(1, -1), w_out_b,
        ln1_g.reshape(1, d), ln1_b.reshape(1, d), wr_hi, wr_lo, e_bias, seq)

    counts = seen_b[-1] + cnt[-1]
    padded = (counts + ROW_PAD - 1) // ROW_PAD * ROW_PAD
    pend = jnp.cumsum(padded).astype(I32)
    pstart = pend - padded
    n_rows = n * TOP_K + N_EXPERTS * ROW_PAD
    nused = (pend[-1:] // ZERO_ROWS).astype(I32)
    slabs = d // LANES
    loff = (lpos_t * slabs).T.reshape(-1)
    gate = gate_t.T.reshape(-1)
    seg_local = (seglo[::8] * slabs).reshape(-1)
    seg_table = ((pstart[None, :] + seen_b[::8]) * slabs).reshape(-1)
    seg_cnt = cnt[::8].reshape(-1)

    xs = _dispatch(pstart + counts, pend, nused, loff, seg_local, seg_table, seg_cnt, u2s, n_rows, slabs)
    ys = _experts(pstart, pend, xs, w1, w3, w2, slabs)
    out = _final(loff, gate, seg_local, seg_table, seg_cnt, u2s, x1, ada, ws1.astype(BF16),
                 ws3.astype(BF16), ws2.astype(BF16), ln2_g.reshape(1, d), ln2_b.reshape(1, d), ys, seq, slabs)
    return out.reshape(batch, seq, d)


def kernel(x, c, positions, w_ada, b_ada, w_in, attn_sinks, g_swa, g_sb, w_out, ln1_g, ln1_b,
           w_router, e_bias, w1, w3, w2, ws1, ws3, ws2, ln2_g, ln2_b):
    assert w_ada.shape[0] == DEPTH
    for l in range(DEPTH):
        x = _layer(x, c, positions, w_ada[l], b_ada[l], w_in[l], attn_sinks[l], g_swa[l], g_sb[l],
                   w_out[l], ln1_g[l], ln1_b[l], w_router[l], e_bias[l], w1[l], w3[l], w2[l],
                   ws1[l], ws3[l], ws2[l], ln2_g[l], ln2_b[l])
    return x
```
